```python
import math
import jax, jax.numpy as jnp
from jax import lax
import numpy as np

D_MODEL = 1024
BATCH = 16
SEQ = 2048
DEPTH = 1
DEC_BATCH = 128
DEC_SEQ = 8
PAST_LEN = 8192
PAGE_SIZE = 128

N_MEM = 256
GLA_HEADS = 4
GLA_DV = D_MODEL // (2 * GLA_HEADS)
GLA_DK = GLA_DV // 2
GLA_GATE_RANK = 16
GLA_TAU = 16.0
GLA_CHUNK = 64
MLA_HEADS = 8
MLA_V = D_MODEL // (2 * MLA_HEADS)
MLA_NOPE = MLA_V
MLA_ROPE = MLA_V // 2
MLA_Q_RANK = 3 * D_MODEL // 8
MLA_KV_RANK = D_MODEL // 4
ROPE_THETA = 10000.0
Q_BLOCK = 128
X_HEADS = 4
X_DIM = D_MODEL // 8
D_FF = 4 * D_MODEL
EPS = 1e-6

GLA_WIDTH = GLA_HEADS * GLA_DV
MLA_WIDTH = MLA_HEADS * MLA_V
MIX_WIDTH = GLA_WIDTH + MLA_WIDTH
IN_SIZES = (GLA_HEADS * GLA_DK, GLA_HEADS * GLA_DK, GLA_WIDTH, GLA_WIDTH, GLA_GATE_RANK,
            MLA_Q_RANK, MLA_KV_RANK, MLA_ROPE)
SPLIT_POINTS = tuple(int(v) for v in np.cumsum(IN_SIZES)[:-1])
D_IN = sum(IN_SIZES)
MLA_SCALE = (MLA_NOPE + MLA_ROPE) ** -0.5

kernel_name = 'hymba_gla_mla_memory_decoder_step'


def rms_norm(x, g):
    xf = x.astype(jnp.float32)
    y = xf * lax.rsqrt(jnp.mean(xf * xf, axis=-1, keepdims=True) + EPS)
    return (y * g.astype(jnp.float32)).astype(x.dtype)


def rope(x, pos):
    half = x.shape[-1] // 2
    inv = ROPE_THETA ** (-jnp.arange(half, dtype=jnp.float32) / half)
    ang = pos.astype(jnp.float32)[:, None] * inv[None, :]
    cos = jnp.cos(ang)[None, :, None, :]
    sin = jnp.sin(ang)[None, :, None, :]
    xf = x.astype(jnp.float32)
    x1, x2 = xf[..., :half], xf[..., half:]
    return jnp.concatenate([x1 * cos - x2 * sin, x2 * cos + x1 * sin], axis=-1).astype(x.dtype)


def gla_recurrence(q, k, v, log_a, s0):
    B, T = q.shape[:2]
    c = math.gcd(T, GLA_CHUNK)
    n = T // c

    def to_chunks(t):
        return jnp.moveaxis(t.reshape(B, n, c, *t.shape[2:]), 1, 0).astype(jnp.float32)

    qc, kc, vc, ac = to_chunks(q), to_chunks(k), to_chunks(v), to_chunks(log_a)
    causal = jnp.tril(jnp.ones((c, c), dtype=bool))[None, :, :, None, None]

    def step(s, inp):
        qi, ki, vi, ai = inp
        b = jnp.cumsum(ai, axis=1)
        o_inter = jnp.einsum('bthk,bhkv->bthv', qi * jnp.exp(b), s)
        diff = jnp.where(causal, b[:, :, None] - b[:, None, :], -jnp.inf)
        att = jnp.einsum('bthk,bshk,btshk->bhts', qi, ki, jnp.exp(diff))
        o_intra = jnp.einsum('bhts,bshv->bthv', att, vi)
        b_last = b[:, -1]
        s_new = s * jnp.exp(b_last)[..., None] + jnp.einsum(
            'bshk,bshv->bhkv', ki * jnp.exp(b_last[:, None] - b), vi)
        return s_new, o_inter + o_intra

    s_fin, o = lax.scan(step, s0.astype(jnp.float32), (qc, kc, vc, ac))
    o = jnp.moveaxis(o, 0, 1).reshape(B, T, GLA_HEADS, GLA_DV)
    return o, s_fin


def mix_inputs(h, pos, lp):
    B, T, _ = h.shape
    z = h @ lp['w_in']
    q, k, v, r, a, qa, kva, kpe = jnp.split(z, SPLIT_POINTS, axis=-1)
    q = q.reshape(B, T, GLA_HEADS, GLA_DK) * (GLA_DK ** -0.5)
    k = k.reshape(B, T, GLA_HEADS, GLA_DK)
    v = v.reshape(B, T, GLA_HEADS, GLA_DV)
    gate_pre = (a @ lp['w_gla_a2'] + lp['b_gla_a']).astype(jnp.float32)
    log_a = (jax.nn.log_sigmoid(gate_pre) / GLA_TAU).reshape(B, T, GLA_HEADS, GLA_DK)
    qf = (rms_norm(qa, lp['g_mla_qa']) @ lp['w_mla_qb']).reshape(B, T, MLA_HEADS, MLA_NOPE + MLA_ROPE)
    qn = rms_norm(qf[..., :MLA_NOPE], lp['g_q_nope'])
    qp = rope(rms_norm(qf[..., MLA_NOPE:], lp['g_q_rope']), pos)
    ckv = rms_norm(kva, lp['g_mla_kva'])
    kp = rope(rms_norm(kpe, lp['g_k_rope'])[:, :, None, :], pos)[:, :, 0]
    return (q, k, v, r, log_a), (qn, qp, ckv, kp)


def mla_decompress(ckv, lp):
    B, S, _ = ckv.shape
    kv = (ckv @ lp['w_mla_kvb']).reshape(B, S, MLA_HEADS, MLA_NOPE + MLA_V)
    kn = rms_norm(kv[..., :MLA_NOPE], lp['g_k_nope'])
    return kn, kv[..., MLA_NOPE:]


def mla_scores(qn, qp, kn, kp):
    s = jnp.einsum('bthd,bshd->bhts', qn, kn) + jnp.einsum('bthd,bsd->bhts', qp, kp)
    return s.astype(jnp.float32) * MLA_SCALE


def mla_prompt_attention(qn, qp, kn, kp, v):
    B, T = qn.shape[:2]
    qb = math.gcd(T, Q_BLOCK)
    nb = T // qb
    qn_b = qn.reshape(B, nb, qb, MLA_HEADS, MLA_NOPE).swapaxes(0, 1)
    qp_b = qp.reshape(B, nb, qb, MLA_HEADS, MLA_ROPE).swapaxes(0, 1)
    kpos = jnp.arange(T)

    def block(args):
        i, qn_i, qp_i = args
        s = mla_scores(qn_i, qp_i, kn, kp)
        qpos = i * qb + jnp.arange(qb)
        s = jnp.where(kpos[None, :] <= qpos[:, None], s, -jnp.inf)
        p = jax.nn.softmax(s, axis=-1).astype(v.dtype)
        return jnp.einsum('bhts,bshd->bthd', p, v)

    o = lax.map(block, (jnp.arange(nb), qn_b, qp_b))
    return o.swapaxes(0, 1).reshape(B, T, MLA_HEADS, MLA_V)


def online_update(carry, s, v):
    m, l, acc = carry
    m_new = jnp.maximum(m, jnp.max(s, axis=-1))
    corr = jnp.exp(m - m_new)
    p = jnp.exp(s - m_new[..., None])
    l = l * corr + jnp.sum(p, axis=-1)
    acc = acc * corr[..., None] + jnp.einsum('bhts,bshd->bhtd', p, v.astype(jnp.float32))
    return (m_new, l, acc)


def mla_sample_attention(qn, qp, ckv_new, kp_new, pool_ckv, pool_kpe, page_table, lp):
    B, T = qn.shape[:2]
    carry = (jnp.full((B, MLA_HEADS, T), -jnp.inf, jnp.float32),
             jnp.zeros((B, MLA_HEADS, T), jnp.float32),
             jnp.zeros((B, MLA_HEADS, T, MLA_V), jnp.float32))

    def page_step(c, pages):
        ckv = pool_ckv[pages]
        kp = pool_kpe[pages]
        kn, v = mla_decompress(ckv, lp)
        return online_update(c, mla_scores(qn, qp, kn, kp), v), None

    carry, _ = lax.scan(page_step, carry, page_table.T)
    kn, v = mla_decompress(ckv_new, lp)
    s = mla_scores(qn, qp, kn, kp_new)
    s = jnp.where(jnp.tril(jnp.ones((T, T), dtype=bool)), s, -jnp.inf)
    m, l, acc = online_update(carry, s, v)
    o = acc / l[..., None]
    return o.transpose(0, 2, 1, 3).astype(qn.dtype)


def mix_output(o_gla, r, o_mla, lp):
    B, T = r.shape[:2]
    og = rms_norm(o_gla.astype(r.dtype), lp['g_gla_o']).reshape(B, T, GLA_WIDTH) * jax.nn.silu(r)
    om = o_mla.reshape(B, T, MLA_WIDTH)
    return jnp.concatenate([og, om], axis=-1) @ lp['w_out']


def memory_kv(mem, lp):
    B, N, _ = mem.shape
    m = rms_norm(mem, lp['g_mem'])
    k = rms_norm((m @ lp['w_xk']).reshape(B, N, X_HEADS, X_DIM), lp['g_xk'])
    v = (m @ lp['w_xv']).reshape(B, N, X_HEADS, X_DIM)
    return k, v


def cross_attend(h, mk, mv, lp):
    B, T, _ = h.shape
    q = rms_norm((h @ lp['w_xq']).reshape(B, T, X_HEADS, X_DIM), lp['g_xq'])
    s = jnp.einsum('bthd,bnhd->bhtn', q, mk).astype(jnp.float32) * (X_DIM ** -0.5)
    p = jax.nn.softmax(s, axis=-1).astype(mv.dtype)
    o = jnp.einsum('bhtn,bnhd->bthd', p, mv).reshape(B, T, X_HEADS * X_DIM)
    return o @ lp['w_xo']


def ffn(h, lp):
    u = jnp.maximum(h @ lp['w_ff1'], 0)
    return (u * u) @ lp['w_ff2']


def prompt_layer(x, mem, lp):
    B, T, _ = x.shape
    pos = jnp.arange(T)
    h = rms_norm(x, lp['g_mix'])
    (q, k, v, r, log_a), (qn, qp, ckv, kp) = mix_inputs(h, pos, lp)
    s0 = jnp.zeros((B, GLA_HEADS, GLA_DK, GLA_DV), jnp.float32)
    o_gla, s_gla = gla_recurrence(q, k, v, log_a, s0)
    kn, vm = mla_decompress(ckv, lp)
    o_mla = mla_prompt_attention(qn, qp, kn, kp, vm)
    x = x + mix_output(o_gla, r, o_mla, lp)
    mk, mv = memory_kv(mem, lp)
    x = x + cross_attend(rms_norm(x, lp['g_x']), mk, mv, lp)
    x = x + ffn(rms_norm(x, lp['g_ff']), lp)
    return x, ckv, kp, mk, mv, s_gla.astype(x.dtype)


def sample_layer(x, pool_ckv, pool_kpe, page_table, mk, mv, s_prev, lp):
    B, T, _ = x.shape
    pos = PAST_LEN + jnp.arange(T)
    h = rms_norm(x, lp['g_mix'])
    (q, k, v, r, log_a), (qn, qp, ckv, kp) = mix_inputs(h, pos, lp)
    o_gla, s_gla = gla_recurrence(q, k, v, log_a, s_prev)
    o_mla = mla_sample_attention(qn, qp, ckv, kp, pool_ckv, pool_kpe, page_table, lp)
    x = x + mix_output(o_gla, r, o_mla, lp)
    x = x + cross_attend(rms_norm(x, lp['g_x']), mk, mv, lp)
    x = x + ffn(rms_norm(x, lp['g_ff']), lp)
    return x, ckv, kp, s_gla.astype(x.dtype)


def setup_inputs(seed: int = 0) -> dict:
    key = jax.random.key(seed)
    ks = iter(jax.random.split(key, 48))

    def nrm(shape, scale=1.0):
        return jax.random.normal(next(ks), shape, jnp.float32) * scale

    def gain(n):
        return 1.0 + nrm((DEPTH, n), 0.02)

    def lin(fi, fo):
        return nrm((DEPTH, fi, fo), fi ** -0.5)

    n_pages = PAST_LEN // PAGE_SIZE
    n_used = DEC_BATCH * n_pages
    n_pool = n_used + n_used // 4
    x_prompt = nrm((BATCH, SEQ, D_MODEL))
    x_sample = nrm((DEC_BATCH, DEC_SEQ, D_MODEL))
    mem_prompt = nrm((BATCH, N_MEM, D_MODEL))
    cache_ckv = nrm((DEPTH, n_pool, PAGE_SIZE, MLA_KV_RANK))
    cache_kpe = nrm((DEPTH, n_pool, PAGE_SIZE, MLA_ROPE))
    cache_mem_k = nrm((DEPTH, DEC_BATCH, N_MEM, X_HEADS, X_DIM))
    cache_mem_v = nrm((DEPTH, DEC_BATCH, N_MEM, X_HEADS, X_DIM))
    state_gla = nrm((DEPTH, DEC_BATCH, GLA_HEADS, GLA_DK, GLA_DV), 0.5)
    page_table = jax.random.permutation(next(ks), n_pool)[:n_used].reshape(DEC_BATCH, n_pages).astype(jnp.int32)
    return {
        'x_prompt': x_prompt, 'x_sample': x_sample, 'mem_prompt': mem_prompt,
        'cache_ckv': cache_ckv, 'cache_kpe': cache_kpe,
        'cache_mem_k': cache_mem_k, 'cache_mem_v': cache_mem_v,
        'state_gla': state_gla, 'page_table': page_table,
        'g_mix': gain(D_MODEL), 'w_in': lin(D_MODEL, D_IN),
        'w_gla_a2': lin(GLA_GATE_RANK, GLA_HEADS * GLA_DK), 'b_gla_a': nrm((DEPTH, GLA_HEADS * GLA_DK), 0.01),
        'g_gla_o': gain(GLA_DV),
        'g_mla_qa': gain(MLA_Q_RANK), 'w_mla_qb': lin(MLA_Q_RANK, MLA_HEADS * (MLA_NOPE + MLA_ROPE)),
        'g_mla_kva': gain(MLA_KV_RANK), 'w_mla_kvb': lin(MLA_KV_RANK, MLA_HEADS * (MLA_NOPE + MLA_V)),
        'g_q_nope': gain(MLA_NOPE), 'g_k_nope': gain(MLA_NOPE),
        'g_q_rope': gain(MLA_ROPE), 'g_k_rope': gain(MLA_ROPE),
        'w_out': lin(MIX_WIDTH, D_MODEL),
        'g_x': gain(D_MODEL), 'g_mem': gain(D_MODEL),
        'w_xq': lin(D_MODEL, X_HEADS * X_DIM), 'w_xk': lin(D_MODEL, X_HEADS * X_DIM),
        'w_xv': lin(D_MODEL, X_HEADS * X_DIM), 'g_xq': gain(X_DIM), 'g_xk': gain(X_DIM),
        'w_xo': lin(X_HEADS * X_DIM, D_MODEL),
        'g_ff': gain(D_MODEL), 'w_ff1': lin(D_MODEL, D_FF), 'w_ff2': lin(D_FF, D_MODEL),
    }


def reference(x_prompt, x_sample, mem_prompt, cache_ckv, cache_kpe, cache_mem_k, cache_mem_v,
              state_gla, page_table, g_mix, w_in, w_gla_a2, b_gla_a, g_gla_o, g_mla_qa, w_mla_qb,
              g_mla_kva, w_mla_kvb, g_q_nope, g_k_nope, g_q_rope, g_k_rope, w_out, g_x, g_mem,
              w_xq, w_xk, w_xv, g_xq, g_xk, w_xo, g_ff, w_ff1, w_ff2):
    yp, ys = x_prompt, x_sample
    ckv_p, kpe_p, mk_p, mv_p, gla_p = [], [], [], [], []
    ckv_s, kpe_s, gla_s = [], [], []
    for l in range(DEPTH):
        lp = {
            'g_mix': g_mix[l], 'w_in': w_in[l], 'w_gla_a2': w_gla_a2[l], 'b_gla_a': b_gla_a[l],
            'g_gla_o': g_gla_o[l], 'g_mla_qa': g_mla_qa[l], 'w_mla_qb': w_mla_qb[l],
            'g_mla_kva': g_mla_kva[l], 'w_mla_kvb': w_mla_kvb[l], 'g_q_nope': g_q_nope[l],
            'g_k_nope': g_k_nope[l], 'g_q_rope': g_q_rope[l], 'g_k_rope': g_k_rope[l],
            'w_out': w_out[l], 'g_x': g_x[l], 'g_mem': g_mem[l], 'w_xq': w_xq[l], 'w_xk': w_xk[l],
            'w_xv': w_xv[l], 'g_xq': g_xq[l], 'g_xk': g_xk[l], 'w_xo': w_xo[l],
            'g_ff': g_ff[l], 'w_ff1': w_ff1[l], 'w_ff2': w_ff2[l],
        }
        yp, c1, c2, c3, c4, c5 = prompt_layer(yp, mem_prompt, lp)
        ckv_p.append(c1); kpe_p.append(c2); mk_p.append(c3); mv_p.append(c4); gla_p.append(c5)
        ys, d1, d2, d3 = sample_layer(ys, cache_ckv[l], cache_kpe[l], page_table,
                                      cache_mem_k[l], cache_mem_v[l], state_gla[l], lp)
        ckv_s.append(d1); kpe_s.append(d2); gla_s.append(d3)
    return (yp, ys, jnp.stack(ckv_p), jnp.stack(kpe_p), jnp.stack(mk_p), jnp.stack(mv_p), jnp.stack(gla_p),
            jnp.stack(ckv_s), jnp.stack(kpe_s), jnp.stack(gla_s))
```

```python
import functools

import jax
import jax.numpy as jnp
from jax import lax
from jax.experimental import pallas as pl
from jax.experimental.pallas import tpu as pltpu

F32 = jnp.float32
BF16 = jnp.bfloat16

EPS = 1e-6
D_MODEL = 1024
GLA_H, GLA_K, GLA_V, GLA_RANK, GLA_TAU, GLA_CHUNK = 4, 64, 128, 16, 16.0, 64
GLA_SUB = 16
MLA_H, MLA_DV, MLA_NOPE, MLA_ROPE, MLA_QR, MLA_KVR = 8, 64, 64, 32, 384, 256
MLA_SCALE = (MLA_NOPE + MLA_ROPE) ** -0.5
ROPE_THETA = 10000.0
X_H, X_D = 4, 128
X_SCALE = X_D ** -0.5
D_FF = 4096
PAGE = 128
LANES = 128
NEG = -1e30

_C_Q, _C_K, _C_V, _C_R = 0, 256, 512, 1024
_C_QA, _C_KVA, _C_A, _C_KPE, _C_END = 1536, 1920, 2176, 2304, 2432
_KPE_LANE = 64

VMEM_LIMIT = 56 * 1024 * 1024


def _cparams(sem):
    return pltpu.CompilerParams(dimension_semantics=sem, vmem_limit_bytes=VMEM_LIMIT)


def _dot(a, b):
    return jnp.dot(a, b, preferred_element_type=F32)


def _dot_nt(a, b):
    return lax.dot_general(a, b, (((1,), (1,)), ((), ())), preferred_element_type=F32)


def _rms(x, g):
    return x * lax.rsqrt(jnp.mean(x * x, axis=-1, keepdims=True) + EPS) * g


def _full(shape):
    n = len(shape)
    return pl.BlockSpec(shape, lambda *_: (0,) * n)


def _in_proj_kernel(x_ref, gmix_ref, win_ref, wa2_ref, ba_ref, gqa_ref, wqb_ref, gqrow_ref,
                    gkva_ref, wk_ref, gkrow_ref, wv_ref, gkpe_ref, bd_ref, c_ref, s1_ref, s2_ref,
                    gq_o, gk_o, gv_o, gr_o, la_o, mq_o, mk_o, mv_o, ckv_o, kp_o):
    h = _rms(x_ref[...], gmix_ref[...]).astype(BF16)

    def proj(lo, hi):
        return _dot(h, win_ref[:, lo:hi])

    gq_o[...] = proj(_C_Q, _C_K).astype(BF16)
    gk_o[...] = proj(_C_K, _C_V).astype(BF16)
    gv_o[...] = proj(_C_V, _C_R).astype(BF16)
    gr_o[...] = proj(_C_R, _C_QA).astype(BF16)

    gate = _dot(proj(_C_A, _C_KPE).astype(BF16), wa2_ref[...]) + ba_ref[...]
    la_o[...] = (jnp.minimum(gate, 0.0) - jnp.log1p(jnp.exp(-jnp.abs(gate)))) * (1.0 / GLA_TAU)

    cos, s_up, s_dn = c_ref[...], s1_ref[...], s2_ref[...]

    def rope(t):
        return t * cos + pltpu.roll(t, LANES - 16, 1) * s_up + pltpu.roll(t, 16, 1) * s_dn

    zk = proj(_C_KPE, _C_END)
    kpn = zk * lax.rsqrt(jnp.sum(zk * zk, axis=-1, keepdims=True) * (1.0 / MLA_ROPE) + EPS) * gkpe_ref[...]
    kpt = rope(kpn)
    kp_o[...] = pltpu.roll(kpt, LANES - _KPE_LANE, 1)[:, 0:MLA_ROPE]

    ckv = _rms(proj(_C_KVA, _C_A), gkva_ref[...])
    ckv_o[...] = ckv
    cb = ckv.astype(BF16)
    mv_o[...] = _dot(cb, wv_ref[...]).astype(BF16)

    bd = bd_ref[...]

    def seg_norm(t, grow):
        ms = _dot((t * t).astype(BF16), bd)
        return t * lax.rsqrt(ms + EPS) * grow

    qf = _dot(_rms(proj(_C_QA, _C_KVA), gqa_ref[...]).astype(BF16), wqb_ref[...])
    kf = _dot(cb, wk_ref[...])
    for c in range(4):
        sl = slice(2 * LANES * c, 2 * LANES * (c + 1))
        qn = seg_norm(qf[:, sl], gqrow_ref[:, sl])
        kn = seg_norm(kf[:, sl], gkrow_ref[:, sl])
        for j in range(2):
            o = 2 * LANES * c + LANES * j
            mq_o[:, o:o + LANES] = rope(qn[:, LANES * j:LANES * (j + 1)]).astype(BF16)
            mk_o[:, o:o + LANES] = (kn[:, LANES * j:LANES * (j + 1)] + kpt).astype(BF16)


def _in_proj(x, w, tabs, tm):
    n = x.shape[0]
    nt = tabs[0].shape[0] // tm
    row = lambda width: pl.BlockSpec((tm, width), lambda i: (i, 0))
    tab = pl.BlockSpec((tm, LANES), lambda i: (i % nt, 0))
    consts = [w['g_mix'], w['w_in'], w['w_a2'], w['b_a'], w['g_qa'], w['w_qb'], w['gq_row'],
              w['g_kva'], w['w_k'], w['gk_row'], w['w_v'], w['g_kpe'], w['bd']]
    out_w = [(256, BF16), (256, BF16), (512, BF16), (512, BF16), (256, F32),
             (1024, BF16), (1024, BF16), (512, BF16), (256, F32), (MLA_ROPE, F32)]
    return pl.pallas_call(
        _in_proj_kernel,
        grid=(n // tm,),
        in_specs=[row(D_MODEL)] + [_full(c.shape) for c in consts] + [tab, tab, tab],
        out_specs=[row(wd) for wd, _ in out_w],
        out_shape=[jax.ShapeDtypeStruct((n, wd), dt) for wd, dt in out_w],
        compiler_params=_cparams(("parallel",)),
        name="in_proj",
    )(x, *consts, *tabs)


def _head_masks():
    lane = lax.broadcasted_iota(jnp.int32, (1, GLA_H * GLA_K), 1)
    return [(lane // GLA_K == h).astype(F32) for h in range(GLA_H)]


def _gla_chunk(q, k, v, la, st, ltri, bexp, c_len, sub):
    masks = _head_masks()
    if ltri is None:
        rows = lax.broadcasted_iota(jnp.int32, (c_len, 1), 0)
        b = jnp.zeros_like(la)
        for j in range(c_len):
            b = b + jnp.where(rows >= j, la[j:j + 1, :], 0.0)
    else:
        la_hi = la.astype(BF16)
        la_lo = (la - la_hi.astype(F32)).astype(BF16)
        b = _dot(ltri, la_hi) + _dot(ltri, la_lo)
    bl = b[c_len - 1:c_len, :]

    qh = q * jnp.exp(b)
    lq = jnp.concatenate([qh * m for m in masks], axis=0).astype(BF16)
    o_inter = _dot_nt(lq, st.astype(BF16))

    nsub = c_len // sub
    a_rows = [[] for _ in range(GLA_H)]
    col = lax.broadcasted_iota(jnp.int32, (1, c_len), 1)
    for i in range(1, nsub):
        ref = b[sub * i - 1:sub * i, :]
        qi = q[sub * i:sub * (i + 1)] * jnp.exp(b[sub * i:sub * (i + 1)] - ref)
        ki = k * jnp.exp(jnp.minimum(ref - b, 0.0))
        li = jnp.concatenate([qi * m for m in masks], axis=0).astype(BF16)
        ai = jnp.where(col < sub * i, _dot_nt(li, ki.astype(BF16)), 0.0)
        for h in range(GLA_H):
            a_rows[h].append(ai[sub * h:sub * (h + 1)])

    row = lax.broadcasted_iota(jnp.int32, (sub, 1), 0)
    o_diag = []
    for i in range(nsub):
        sl = slice(sub * i, sub * (i + 1))
        bb, qb, kb, vb = b[sl], q[sl], k[sl], v[sl]
        ps = []
        for s in range(sub):
            e = jnp.exp(jnp.where(row >= s, bb - bb[s:s + 1, :], NEG))
            ps.append(qb * e * kb[s:s + 1, :])
        r = _dot(jnp.concatenate(ps, axis=0).astype(BF16), bexp)
        od = r[0:sub] * vb[0:1, :]
        for s in range(1, sub):
            od = od + r[sub * s:sub * (s + 1)] * vb[s:s + 1, :]
        o_diag.append(od)
    o_diag = jnp.concatenate(o_diag, axis=0) if nsub > 1 else o_diag[0]

    outs = []
    for h in range(GLA_H):
        oh = o_inter[c_len * h:c_len * (h + 1)] + o_diag[:, GLA_V * h:GLA_V * (h + 1)]
        if nsub > 1:
            ah = jnp.concatenate([jnp.zeros((sub, c_len), F32)] + a_rows[h], axis=0)
            oh = oh + _dot(ah.astype(BF16), v[:, GLA_V * h:GLA_V * (h + 1)].astype(BF16))
        outs.append(oh)

    kl = k * jnp.exp(bl - b)
    vs = jnp.concatenate([v[:, GLA_V * h:GLA_V * (h + 1)] for h in range(GLA_H)], axis=0)
    ks = jnp.concatenate([kl * m for m in masks], axis=0).astype(BF16)
    st_new = st * jnp.exp(bl) + _dot(vs.T.astype(BF16), ks)
    return outs, st_new


def _gla_kernel(*refs, c_len, sub, cps, use_tri):
    if use_tri:
        q_ref, k_ref, v_ref, r_ref, la_ref, s0_ref, g_ref, bexp_ref, ltri_ref, og_o, sf_o, st_ref = refs
    else:
        q_ref, k_ref, v_ref, r_ref, la_ref, s0_ref, g_ref, bexp_ref, og_o, sf_o, st_ref = refs
        ltri_ref = None
    t = pl.program_id(1)

    @pl.when(t == 0)
    def _():
        st_ref[...] = s0_ref[...].T

    g = g_ref[...]
    bexp = bexp_ref[...]

    def body(c, carry):
        rs = pl.ds(pl.multiple_of(c * c_len, c_len), c_len)
        ltri = ltri_ref[...] if use_tri else None
        outs, st_new = _gla_chunk(q_ref[rs, :].astype(F32), k_ref[rs, :].astype(F32), v_ref[rs, :].astype(F32),
                                  la_ref[rs, :], st_ref[...], ltri, bexp, c_len, sub)
        st_ref[...] = st_new
        rr = r_ref[rs, :].astype(F32)
        for h in range(GLA_H):
            hs = slice(GLA_V * h, GLA_V * (h + 1))
            rh = rr[:, hs]
            og_o[rs, hs] = (_rms(outs[h], g) * (rh / (1.0 + jnp.exp(-rh)))).astype(BF16)
        return carry

    lax.fori_loop(0, cps, body, 0)

    @pl.when(t == pl.num_programs(1) - 1)
    def _():
        sf_o[...] = st_ref[...].T


def _gla(gq, gk, gv, gr, la, s0, w, batch, seq, c_len, sub, cps):
    ts = c_len * cps
    nt = seq // ts
    use_tri = c_len > 8
    row = lambda width: pl.BlockSpec((None, ts, width), lambda b, t: (b * nt + t, 0, 0))
    g3 = lambda a: a.reshape(batch * nt, ts, a.shape[-1])
    st_spec = pl.BlockSpec((None, GLA_H * GLA_K, GLA_V), lambda b, t: (b, 0, 0))
    consts = [w['g_gla_o'], w['bexp']] + ([w['ltri']] if use_tri else [])
    og, s_fin = pl.pallas_call(
        functools.partial(_gla_kernel, c_len=c_len, sub=sub, cps=cps, use_tri=use_tri),
        grid=(batch, nt),
        in_specs=[row(256), row(256), row(512), row(512), row(256), st_spec] + [_full(c.shape) for c in consts],
        out_specs=[row(512), st_spec],
        out_shape=[jax.ShapeDtypeStruct((batch * nt, ts, GLA_H * GLA_V), BF16),
                   jax.ShapeDtypeStruct((batch, GLA_H * GLA_K, GLA_V), F32)],
        scratch_shapes=[pltpu.VMEM((GLA_V, GLA_H * GLA_K), F32)],
        compiler_params=_cparams(("parallel", "arbitrary")),
        name="gla",
    )(g3(gq), g3(gk), g3(gv), g3(gr), g3(la), s0, *consts)
    return og.reshape(batch * seq, GLA_H * GLA_V), s_fin


def _softmax_step(s, v, m, l, acc):
    m_new = jnp.maximum(m, jnp.max(s, axis=-1, keepdims=True))
    alpha = jnp.exp(m - m_new)
    p = jnp.exp(s - m_new)
    l = alpha * l + jnp.sum(p, axis=-1, keepdims=True)
    acc = alpha * acc + _dot(p.astype(BF16), v)
    return m_new, l, acc


def _mla_prompt_kernel(q_ref, k_ref, v_ref, o_ref, *, tq):
    i = pl.program_id(1)
    causal = (lax.broadcasted_iota(jnp.int32, (tq, tq), 0) >= lax.broadcasted_iota(jnp.int32, (tq, tq), 1))
    lane = lax.broadcasted_iota(jnp.int32, (1, LANES), 1)
    for pair in range(MLA_H // 2):
        res = []
        for h in (2 * pair, 2 * pair + 1):
            hs = slice(LANES * h, LANES * (h + 1))
            vs = slice(LANES * pair, LANES * (pair + 1))
            q = q_ref[:, hs]

            def body(j, carry):
                rs = pl.ds(pl.multiple_of(j * tq, tq), tq)
                s = _dot_nt(q, k_ref[rs, hs]) * MLA_SCALE
                return _softmax_step(s, v_ref[rs, vs], *carry)

            init = (jnp.full((tq, 1), -jnp.inf, F32), jnp.zeros((tq, 1), F32), jnp.zeros((tq, LANES), F32))
            carry = lax.fori_loop(0, i, body, init)
            rs = pl.ds(pl.multiple_of(i * tq, tq), tq)
            s = jnp.where(causal, _dot_nt(q, k_ref[rs, hs]) * MLA_SCALE, -jnp.inf)
            m, l, acc = _softmax_step(s, v_ref[rs, vs], *carry)
            res.append(acc / l)
        o_ref[:, LANES * pair:LANES * (pair + 1)] = jnp.where(lane < MLA_DV, res[0], res[1]).astype(BF16)


def _mla_prompt(mq, mk, mv, batch, seq, tq):
    nq = seq // tq
    return pl.pallas_call(
        functools.partial(_mla_prompt_kernel, tq=tq),
        grid=(batch, nq),
        in_specs=[pl.BlockSpec((tq, MLA_H * LANES), lambda b, i: (b * nq + i, 0)),
                  pl.BlockSpec((seq, MLA_H * LANES), lambda b, i: (b, 0)),
                  pl.BlockSpec((seq, MLA_H * MLA_DV), lambda b, i: (b, 0))],
        out_specs=pl.BlockSpec((tq, MLA_H * MLA_DV), lambda b, i: (b * nq + i, 0)),
        out_shape=jax.ShapeDtypeStruct((batch * seq, MLA_H * MLA_DV), BF16),
        compiler_params=_cparams(("parallel", "arbitrary")),
        name="mla_prompt",
    )(mq, mk, mv)


def _mla_sample_kernel(pt_ref, q_ref, wkq_ref, wkt_ref, wv_ref, cn_ref, kn_ref, *rest, pps, t_new, kpm):
    ckv_refs = rest[:pps]
    kpe_refs = rest[pps:2 * pps]
    o_ref, lq_ref, qp_ref, m_ref, l_ref, acc_ref = rest[2 * pps:]
    c = pl.program_id(1)
    nrow = MLA_H * t_new

    @pl.when(c == 0)
    def _():
        for h in range(MLA_H):
            qh = q_ref[:, LANES * h:LANES * (h + 1)]
            lq_ref[t_new * h:t_new * (h + 1), :] = _dot(qh, wkq_ref[h]).astype(BF16)
            qp_ref[t_new * h:t_new * (h + 1), :] = pltpu.roll(qh.astype(F32), LANES - _KPE_LANE, 1).astype(BF16)
        lq_ref[nrow:, :] = wkt_ref[...]
        m_ref[...] = jnp.full(m_ref.shape, -jnp.inf, F32)
        l_ref[...] = jnp.zeros(l_ref.shape, F32)
        acc_ref[...] = jnp.zeros(acc_ref.shape, F32)

    lq = lq_ref[...]
    qp = qp_ref[:, 0:MLA_ROPE]

    def scores(x, kp):
        r = _dot_nt(lq, x)
        rinv = []
        for h in range(MLA_H):
            kr = r[nrow + MLA_NOPE * h:nrow + MLA_NOPE * (h + 1)]
            ms = jnp.sum(kr * kr, axis=0, keepdims=True) * (1.0 / MLA_NOPE)
            rinv.append(jnp.broadcast_to(lax.rsqrt(ms + EPS), (t_new, x.shape[0])))
        return (r[0:nrow] * jnp.concatenate(rinv, axis=0) + _dot_nt(qp, kp)) * MLA_SCALE

    def update(s, x):
        m, l, acc = _softmax_step(s, x, m_ref[...], l_ref[...], acc_ref[...])
        m_ref[...] = m
        l_ref[...] = l
        acc_ref[...] = acc

    for g in range(pps // kpm):
        x = jnp.concatenate([ckv_refs[g * kpm + j][...] for j in range(kpm)], axis=0).astype(BF16)
        kp = jnp.concatenate([kpe_refs[g * kpm + j][...] for j in range(kpm)], axis=0).astype(BF16)
        update(scores(x, kp), x)

    @pl.when(c == pl.num_programs(1) - 1)
    def _():
        x = cn_ref[...].astype(BF16)
        s = scores(x, kn_ref[...].astype(BF16))
        tq = lax.broadcasted_iota(jnp.int32, (nrow, t_new), 0) % t_new
        ts = lax.broadcasted_iota(jnp.int32, (nrow, t_new), 1)
        update(jnp.where(ts <= tq, s, -jnp.inf), x)
        oa = (acc_ref[...] / l_ref[...]).astype(BF16)
        r = _dot(oa, wv_ref[...])
        lane = lax.broadcasted_iota(jnp.int32, (1, MLA_H * MLA_DV), 1)
        out = jnp.zeros((t_new, MLA_H * MLA_DV), F32)
        for h in range(MLA_H):
            out = out + jnp.where(lane // MLA_DV == h, r[t_new * h:t_new * (h + 1)], 0.0)
        o_ref[...] = out.astype(BF16)


def _mla_sample(mq, ckv_new, kp_new, pool_ckv, pool_kpe, page_table, w, batch, t_new, pps, kpm):
    n_pages = page_table.shape[1]
    nch = n_pages // pps
    pt = page_table.reshape(-1)
    nrow = MLA_H * t_new

    def page_spec(width, i):
        return pl.BlockSpec((None, PAGE, width), lambda b, c, pt_ref: (pt_ref[b * n_pages + c * pps + i], 0, 0))

    tok = lambda width: pl.BlockSpec((None, t_new, width), lambda b, c, pt_ref: (b, 0, 0))
    g3 = lambda a: a.reshape(batch, t_new, a.shape[-1])
    const = lambda a: pl.BlockSpec(a.shape, lambda b, c, pt_ref: (0,) * a.ndim)
    grid_spec = pltpu.PrefetchScalarGridSpec(
        num_scalar_prefetch=1,
        grid=(batch, nch),
        in_specs=[tok(MLA_H * LANES), const(w['w_kq']), const(w['w_kt']), const(w['w_v']),
                  tok(MLA_KVR), tok(MLA_ROPE)]
                 + [page_spec(MLA_KVR, i) for i in range(pps)]
                 + [page_spec(MLA_ROPE, i) for i in range(pps)],
        out_specs=tok(MLA_H * MLA_DV),
        scratch_shapes=[pltpu.VMEM((nrow + MLA_H * MLA_NOPE, MLA_KVR), BF16),
                        pltpu.VMEM((nrow, LANES), BF16),
                        pltpu.VMEM((nrow, 1), F32), pltpu.VMEM((nrow, 1), F32),
                        pltpu.VMEM((nrow, MLA_KVR), F32)],
    )
    om = pl.pallas_call(
        functools.partial(_mla_sample_kernel, pps=pps, t_new=t_new, kpm=kpm),
        grid_spec=grid_spec,
        out_shape=jax.ShapeDtypeStruct((batch, t_new, MLA_H * MLA_DV), BF16),
        compiler_params=_cparams(("parallel", "arbitrary")),
        name="mla_sample",
    )(pt, g3(mq), w['w_kq'], w['w_kt'], w['w_v'], g3(ckv_new), g3(kp_new),
      *([pool_ckv] * pps), *([pool_kpe] * pps))
    return om.reshape(batch * t_new, MLA_H * MLA_DV)


def _mix_out_kernel(x_ref, og_ref, om_ref, wo_ref, gx_ref, wxq_ref, gxq_ref, x1_o, xq_o):
    x1 = x_ref[...] + _dot(og_ref[...], wo_ref[0:512, :]) + _dot(om_ref[...], wo_ref[512:1024, :])
    x1_o[...] = x1
    qf = _dot(_rms(x1, gx_ref[...]).astype(BF16), wxq_ref[...])
    g = gxq_ref[...]
    for h in range(X_H):
        hs = slice(X_D * h, X_D * (h + 1))
        xq_o[:, hs] = _rms(qf[:, hs], g).astype(BF16)


def _mix_out(x, og, om, w, tm):
    n = x.shape[0]
    row = lambda width: pl.BlockSpec((tm, width), lambda i: (i, 0))
    consts = [w['w_out'], w['g_x'], w['w_xq'], w['g_xq']]
    return pl.pallas_call(
        _mix_out_kernel,
        grid=(n // tm,),
        in_specs=[row(D_MODEL), row(512), row(512)] + [_full(c.shape) for c in consts],
        out_specs=[row(D_MODEL), row(X_H * X_D)],
        out_shape=[jax.ShapeDtypeStruct((n, D_MODEL), F32), jax.ShapeDtypeStruct((n, X_H * X_D), BF16)],
        compiler_params=_cparams(("parallel",)),
        name="mix_out",
    )(x, og, om, *consts)


def _cross_kernel(x1_ref, xq_ref, mk_ref, mv_ref, wxo_ref, x2_o):
    outs = []
    for h in range(X_H):
        hs = slice(X_D * h, X_D * (h + 1))
        s = _dot_nt(xq_ref[:, hs], mk_ref[:, hs].astype(BF16)) * X_SCALE
        p = jnp.exp(s - jnp.max(s, axis=-1, keepdims=True))
        p = p / jnp.sum(p, axis=-1, keepdims=True)
        outs.append(_dot(p.astype(BF16), mv_ref[:, hs].astype(BF16)).astype(BF16))
    x2_o[...] = x1_ref[...] + _dot(jnp.concatenate(outs, axis=1), wxo_ref[...])


def _cross(x1, xq, mk, mv, w, batch, seq, tq):
    nq = seq // tq
    n_mem = mk.shape[0] // batch
    row = lambda width: pl.BlockSpec((None, tq, width), lambda b, i: (b * nq + i, 0, 0))
    g3 = lambda a: a.reshape(batch * nq, tq, a.shape[-1])
    mem = pl.BlockSpec((n_mem, X_H * X_D), lambda b, i: (b, 0))
    x2 = pl.pallas_call(
        _cross_kernel,
        grid=(batch, nq),
        in_specs=[row(D_MODEL), row(X_H * X_D), mem, mem, _full(w['w_xo'].shape)],
        out_specs=row(D_MODEL),
        out_shape=jax.ShapeDtypeStruct((batch * nq, tq, D_MODEL), F32),
        compiler_params=_cparams(("parallel", "arbitrary")),
        name="cross",
    )(g3(x1), g3(xq), mk, mv, w['w_xo'])
    return x2.reshape(batch * seq, D_MODEL)


def _ffn_kernel(x_ref, g_ref, w1_ref, w2_ref, y_o, *, fc):
    x = x_ref[...]
    h = _rms(x, g_ref[...]).astype(BF16)
    acc = x
    for c in range(D_FF // fc):
        u = jnp.maximum(_dot(h, w1_ref[:, fc * c:fc * (c + 1)]), 0.0)
        acc = acc + _dot((u * u).astype(BF16), w2_ref[fc * c:fc * (c + 1), :])
    y_o[...] = acc


def _ffn(x, w, tm, fc):
    n = x.shape[0]
    row = pl.BlockSpec((tm, D_MODEL), lambda i: (i, 0))
    consts = [w['g_ff'], w['w_ff1'], w['w_ff2']]
    return pl.pallas_call(
        functools.partial(_ffn_kernel, fc=fc),
        grid=(n // tm,),
        in_specs=[row] + [_full(c.shape) for c in consts],
        out_specs=row,
        out_shape=jax.ShapeDtypeStruct((n, D_MODEL), F32),
        compiler_params=_cparams(("parallel",)),
        name="ffn",
    )(x, *consts)


def _mem_kv_kernel(m_ref, gm_ref, wk_ref, wv_ref, gk_ref, k_o, v_o):
    m = _rms(m_ref[...], gm_ref[...]).astype(BF16)
    kf = _dot(m, wk_ref[...])
    g = gk_ref[...]
    for h in range(X_H):
        hs = slice(X_D * h, X_D * (h + 1))
        k_o[:, hs] = _rms(kf[:, hs], g)
    v_o[...] = _dot(m, wv_ref[...])


def _mem_kv(mem, w, tm):
    n = mem.shape[0]
    row = lambda width: pl.BlockSpec((tm, width), lambda i: (i, 0))
    consts = [w['g_mem'], w['w_xk'], w['w_xv'], w['g_xk']]
    return pl.pallas_call(
        _mem_kv_kernel,
        grid=(n // tm,),
        in_specs=[row(D_MODEL)] + [_full(c.shape) for c in consts],
        out_specs=[row(X_H * X_D), row(X_H * X_D)],
        out_shape=[jax.ShapeDtypeStruct((n, X_H * X_D), F32)] * 2,
        compiler_params=_cparams(("parallel",)),
        name="mem_kv",
    )(mem, *consts)


def _prep_weights(g_mix, w_in, w_gla_a2, b_gla_a, g_gla_o, g_mla_qa, w_mla_qb, g_mla_kva, w_mla_kvb,
                  g_q_nope, g_k_nope, g_q_rope, g_k_rope, w_out, g_x, g_mem, w_xq, w_xk, w_xv, g_xq, g_xk,
                  w_xo, g_ff, w_ff1, w_ff2):
    rowv = lambda g: g.reshape(1, -1).astype(F32)
    zc = lambda n: jnp.zeros((D_MODEL, n), F32)
    sizes = (256, 256, 512, 512, GLA_RANK, MLA_QR, MLA_KVR, MLA_ROPE)
    offs = [0]
    for s in sizes:
        offs.append(offs[-1] + s)
    q, k, v, r, a, qa, kva, kpe = [w_in[:, offs[i]:offs[i + 1]] for i in range(8)]
    w_in_p = jnp.concatenate(
        [q * (GLA_K ** -0.5), k, v, r, qa, kva, a, zc(LANES - GLA_RANK),
         zc(_KPE_LANE), kpe, zc(LANES - _KPE_LANE - MLA_ROPE)], axis=1)
    w_a2 = jnp.concatenate([w_gla_a2, jnp.zeros((LANES - GLA_RANK, GLA_H * GLA_K), F32)], axis=0)

    qb = w_mla_qb.reshape(MLA_QR, MLA_H, MLA_NOPE + MLA_ROPE)
    qb = jnp.pad(qb, ((0, 0), (0, 0), (0, LANES - MLA_NOPE - MLA_ROPE))).reshape(MLA_QR, MLA_H * LANES)
    kvb = w_mla_kvb.reshape(MLA_KVR, MLA_H, MLA_NOPE + MLA_DV)
    wk = kvb[:, :, :MLA_NOPE]
    wk_p = jnp.pad(wk, ((0, 0), (0, 0), (0, LANES - MLA_NOPE))).reshape(MLA_KVR, MLA_H * LANES)
    wv = kvb[:, :, MLA_NOPE:].reshape(MLA_KVR, MLA_H * MLA_DV)
    z32 = jnp.zeros((LANES - MLA_NOPE - MLA_ROPE,), F32)
    gq_row = jnp.tile(jnp.concatenate([g_q_nope, g_q_rope, z32]), MLA_H)
    gk_row = jnp.tile(jnp.concatenate([g_k_nope, jnp.zeros((LANES - MLA_NOPE,), F32)]), MLA_H)
    g_kpe = jnp.concatenate([jnp.zeros((_KPE_LANE,), F32), g_k_rope, z32])

    li = jnp.arange(2 * LANES)
    seg = jnp.where(li % LANES < MLA_NOPE, 0, jnp.where(li % LANES < MLA_NOPE + MLA_ROPE, 1, 2))
    same = (li[:, None] // LANES == li[None, :] // LANES) & (seg[:, None] == seg[None, :]) & (seg[:, None] < 2)
    bd = jnp.where(same, jnp.where(seg[:, None] == 0, 1.0 / MLA_NOPE, 1.0 / MLA_ROPE), 0.0)

    wkq = jnp.transpose(wk, (1, 2, 0)) * g_k_nope[None, :, None]
    wkq = jnp.pad(wkq, ((0, 0), (0, LANES - MLA_NOPE), (0, 0)))
    wkt = jnp.transpose(wk, (1, 2, 0)).reshape(MLA_H * MLA_NOPE, MLA_KVR)

    hk = jnp.arange(GLA_H * GLA_K) // GLA_K
    hv = jnp.arange(GLA_H * GLA_V) // GLA_V
    bexp = (hk[:, None] == hv[None, :])
    ci = jnp.arange(GLA_CHUNK)
    ltri = ci[:, None] >= ci[None, :]
    return {
        'g_mix': rowv(g_mix), 'w_in': w_in_p.astype(BF16), 'w_a2': w_a2.astype(BF16), 'b_a': rowv(b_gla_a),
        'g_qa': rowv(g_mla_qa), 'w_qb': qb.astype(BF16), 'gq_row': rowv(gq_row), 'g_kva': rowv(g_mla_kva),
        'w_k': wk_p.astype(BF16), 'gk_row': rowv(gk_row), 'w_v': wv.astype(BF16), 'g_kpe': rowv(g_kpe),
        'bd': bd.astype(BF16), 'w_kq': wkq.astype(BF16), 'w_kt': wkt.astype(BF16),
        'g_gla_o': rowv(g_gla_o), 'bexp': bexp.astype(BF16), 'ltri': ltri.astype(BF16),
        'w_out': w_out.astype(BF16), 'g_x': rowv(g_x), 'w_xq': w_xq.astype(BF16), 'g_xq': rowv(g_xq),
        'g_mem': rowv(g_mem), 'w_xk': w_xk.astype(BF16), 'w_xv': w_xv.astype(BF16), 'g_xk': rowv(g_xk),
        'w_xo': w_xo.astype(BF16), 'g_ff': rowv(g_ff), 'w_ff1': w_ff1.astype(BF16), 'w_ff2': w_ff2.astype(BF16),
    }


def _rope_tables(pos):
    half = MLA_ROPE // 2
    inv = ROPE_THETA ** (-jnp.arange(half, dtype=F32) / half)
    ang = pos.astype(F32)[:, None] * inv[None, :]
    cos, sin = jnp.cos(ang), jnp.sin(ang)
    n = pos.shape[0]
    one = jnp.ones((n, _KPE_LANE), F32)
    z = lambda w_: jnp.zeros((n, w_), F32)
    tail = LANES - _KPE_LANE - MLA_ROPE
    c = jnp.concatenate([one, cos, cos, jnp.ones((n, tail), F32)], axis=1)
    s_up = jnp.concatenate([z(_KPE_LANE), -sin, z(half), z(tail)], axis=1)
    s_dn = jnp.concatenate([z(_KPE_LANE), z(half), sin, z(tail)], axis=1)
    return c, s_up, s_dn


def _tile_rows(n, cap):
    t = min(n, cap)
    while n % t:
        t //= 2
    return t


def _tail(x1, xq, mk, mv, w, batch, seq):
    x2 = _cross(x1, xq, mk, mv, w, batch, seq, _tile_rows(seq, 512))
    return _ffn(x2, w, _tile_rows(x2.shape[0], 512), 1024)


def _prompt_layer(x, mem, w):
    batch, seq, _ = x.shape
    n = batch * seq
    xf = x.reshape(n, D_MODEL)
    tm = _tile_rows(seq, 512)
    tabs = _rope_tables(jnp.arange(seq))
    gq, gk, gv, gr, la, mq, mk_, mv_, ckv, kp = _in_proj(xf, w, tabs, tm)
    c_len = GLA_CHUNK
    cps = _tile_rows(seq // c_len, 4)
    s0 = jnp.zeros((batch, GLA_H * GLA_K, GLA_V), F32)
    og, s_fin = _gla(gq, gk, gv, gr, la, s0, w, batch, seq, c_len, min(GLA_SUB, c_len), cps)
    om = _mla_prompt(mq, mk_, mv_, batch, seq, _tile_rows(seq, 256))
    x1, xq = _mix_out(xf, og, om, w, tm)
    memf = mem.reshape(-1, D_MODEL)
    xk, xv = _mem_kv(memf, w, _tile_rows(memf.shape[0], 512))
    y = _tail(x1, xq, xk, xv, w, batch, seq)
    return y, ckv, kp, xk, xv, s_fin


def _sample_layer(x, pool_ckv, pool_kpe, page_table, mem_k, mem_v, s_prev, w):
    batch, seq, _ = x.shape
    n = batch * seq
    xf = x.reshape(n, D_MODEL)
    tm = _tile_rows(n, 512)
    past = page_table.shape[1] * PAGE
    pos = past + (jnp.arange(tm) % seq)
    gq, gk, gv, gr, la, mq, mk_, mv_, ckv, kp = _in_proj(xf, w, _rope_tables(pos), tm)
    s0 = s_prev.reshape(batch, GLA_H * GLA_K, GLA_V)
    og, s_fin = _gla(gq, gk, gv, gr, la, s0, w, batch, seq, seq, seq, 1)
    n_pages = page_table.shape[1]
    pps = _tile_rows(n_pages, 16)
    om = _mla_sample(mq, ckv, kp, pool_ckv, pool_kpe, page_table, w, batch, seq, pps, min(pps, 4))
    x1, xq = _mix_out(xf, og, om, w, tm)
    y = _tail(x1, xq, mem_k.reshape(-1, X_H * X_D), mem_v.reshape(-1, X_H * X_D), w, batch, seq)
    return y, ckv, kp, s_fin


def kernel(x_prompt, x_sample, mem_prompt, cache_ckv, cache_kpe, cache_mem_k, cache_mem_v, state_gla, page_table, g_mix, w_in, w_gla_a2, b_gla_a, g_gla_o, g_mla_qa, w_mla_qb, g_mla_kva, w_mla_kvb, g_q_nope, g_k_nope, g_q_rope, g_k_rope, w_out, g_x, g_mem, w_xq, w_xk, w_xv, g_xq, g_xk, w_xo, g_ff, w_ff1, w_ff2):
    depth = w_in.shape[0]
    assert depth == 1, "one layer: prompt-group caches of layer l would feed layer l+1 otherwise unchanged"
    params = (g_mix, w_in, w_gla_a2, b_gla_a, g_gla_o, g_mla_qa, w_mla_qb, g_mla_kva, w_mla_kvb,
              g_q_nope, g_k_nope, g_q_rope, g_k_rope, w_out, g_x, g_mem, w_xq, w_xk, w_xv, g_xq, g_xk,
              w_xo, g_ff, w_ff1, w_ff2)
    w = _prep_weights(*[p[0] for p in params])
    bp, tp, _ = x_prompt.shape
    bs, tsq, _ = x_sample.shape
    yp, ckv_p, kp_p, xk, xv, gla_p = _prompt_layer(x_prompt, mem_prompt, w)
    ys, ckv_s, kp_s, gla_s = _sample_layer(x_sample, cache_ckv.reshape(cache_ckv.shape[1:]), cache_kpe.reshape(cache_kpe.shape[1:]), page_table,
                                            cache_mem_k[0], cache_mem_v[0], state_gla[0], w)
    n_mem = mem_prompt.shape[1]
    return (yp.reshape(bp, tp, D_MODEL), ys.reshape(bs, tsq, D_MODEL),
            ckv_p.reshape(1, bp, tp, MLA_KVR), kp_p.reshape(1, bp, tp, MLA_ROPE),
            xk.reshape(1, bp, n_mem, X_H, X_D), xv.reshape(1, bp, n_mem, X_H, X_D),
            gla_p.reshape(1, bp, GLA_H, GLA_K, GLA_V),
            ckv_s.reshape(1, bs, tsq, MLA_KVR), kp_s.reshape(1, bs, tsq, MLA_ROPE),
            gla_s.reshape(1, bs, GLA_H, GLA_K, GLA_V))
```

```python
import functools

import jax
import jax.numpy as jnp
from jax import lax
from jax.experimental import pallas as pl
from jax.experimental.pallas import tpu as pltpu

F32 = jnp.float32
BF16 = jnp.bfloat16

EPS = 1e-6
D_MODEL = 1024
GLA_H, GLA_K, GLA_V, GLA_RANK, GLA_TAU, GLA_CHUNK = 4, 64, 128, 16, 16.0, 64
GLA_SUB = 16
MLA_H, MLA_DV, MLA_NOPE, MLA_ROPE, MLA_QR, MLA_KVR = 8, 64, 64, 32, 384, 256
MLA_SCALE = (MLA_NOPE + MLA_ROPE) ** -0.5
ROPE_THETA = 10000.0
X_H, X_D = 4, 128
X_SCALE = X_D ** -0.5
D_FF = 4096
PAGE = 128
LANES = 128
NEG = -1e30
LOG2E = 1.4426950408889634
ATT_TILE = 256

_C_Q, _C_K, _C_V, _C_R = 0, 256, 512, 1024
_C_QA, _C_KVA, _C_A, _C_KPE, _C_END = 1536, 1920, 2176, 2304, 2432
_KPE_LANE = 64

VMEM_LIMIT = 56 * 1024 * 1024


def _cparams(sem):
    return pltpu.CompilerParams(dimension_semantics=sem, vmem_limit_bytes=VMEM_LIMIT)


def _dot(a, b):
    return jnp.dot(a, b, preferred_element_type=F32)


def _dot_nt(a, b):
    return lax.dot_general(a, b, (((1,), (1,)), ((), ())), preferred_element_type=F32)


def _rms(x, g):
    return x * lax.rsqrt(jnp.mean(x * x, axis=-1, keepdims=True) + EPS) * g


def _full(shape):
    n = len(shape)
    return pl.BlockSpec(shape, lambda *_: (0,) * n)


def _in_proj_kernel(x_ref, gmix_ref, win_ref, wa2_ref, ba_ref, gqa_ref, wqb_ref, gqrow_ref,
                    gkva_ref, wk_ref, gkrow_ref, wv_ref, gkpe_ref, bd_ref, c_ref, s1_ref, s2_ref,
                    gq_o, gk_o, gv_o, gr_o, la_o, mq_o, ckv_o, kp_o, *kv_o):
    h = _rms(x_ref[...], gmix_ref[...]).astype(BF16)

    def proj(lo, hi):
        return _dot(h, win_ref[:, lo:hi])

    gq_o[...] = proj(_C_Q, _C_K).astype(BF16)
    gk_o[...] = proj(_C_K, _C_V).astype(BF16)
    gv_o[...] = proj(_C_V, _C_R).astype(BF16)
    gr_o[...] = proj(_C_R, _C_QA).astype(BF16)

    gate = _dot(proj(_C_A, _C_KPE).astype(BF16), wa2_ref[...]) + ba_ref[...]
    la_o[...] = (jnp.minimum(gate, 0.0) - jnp.log1p(jnp.exp(-jnp.abs(gate)))) * (1.0 / GLA_TAU)

    cos, s_up, s_dn = c_ref[...], s1_ref[...], s2_ref[...]

    def rope(t):
        return t * cos + pltpu.roll(t, LANES - 16, 1) * s_up + pltpu.roll(t, 16, 1) * s_dn

    zk = proj(_C_KPE, _C_END)
    kpn = zk * lax.rsqrt(jnp.sum(zk * zk, axis=-1, keepdims=True) * (1.0 / MLA_ROPE) + EPS) * gkpe_ref[...]
    kpt = rope(kpn)
    kp_o[...] = pltpu.roll(kpt, LANES - _KPE_LANE, 1)[:, 0:MLA_ROPE]

    ckv = _rms(proj(_C_KVA, _C_A), gkva_ref[...])
    ckv_o[...] = ckv
    cb = ckv.astype(BF16)
    bd = bd_ref[...]

    def seg_norm(t, grow):
        ms = _dot((t * t).astype(BF16), bd)
        return t * lax.rsqrt(ms + EPS) * grow

    qf = _dot(_rms(proj(_C_QA, _C_KVA), gqa_ref[...]).astype(BF16), wqb_ref[...])
    for c in range(4):
        sl = slice(2 * LANES * c, 2 * LANES * (c + 1))
        qn = seg_norm(qf[:, sl], gqrow_ref[:, sl])
        for j in range(2):
            o = 2 * LANES * c + LANES * j
            mq_o[:, o:o + LANES] = rope(qn[:, LANES * j:LANES * (j + 1)]).astype(BF16)

    if kv_o:
        mk_o, mvt_o = kv_o
        mv = _dot(cb, wv_ref[...])
        for c in range(mvt_o.shape[0]):
            mvt_o[c] = mv[ATT_TILE * c:ATT_TILE * (c + 1), :].T.astype(BF16)
        kf = _dot(cb, wk_ref[...])
        for c in range(4):
            sl = slice(2 * LANES * c, 2 * LANES * (c + 1))
            kn = seg_norm(kf[:, sl], gkrow_ref[:, sl])
            for j in range(2):
                o = 2 * LANES * c + LANES * j
                mk_o[:, o:o + LANES] = (kn[:, LANES * j:LANES * (j + 1)] + kpt).astype(BF16)


def _in_proj(x, w, tabs, tm, with_kv):
    n = x.shape[0]
    nt = tabs[0].shape[0] // tm
    row = lambda width: pl.BlockSpec((tm, width), lambda i: (i, 0))
    tab = pl.BlockSpec((tm, LANES), lambda i: (i % nt, 0))
    consts = [w['g_mix'], w['w_in'], w['w_a2'], w['b_a'], w['g_qa'], w['w_qb'], w['gq_row'],
              w['g_kva'], w['w_k'], w['gk_row'], w['w_v'], w['g_kpe'], w['bd']]
    out_w = [(256, BF16), (256, BF16), (512, BF16), (512, BF16), (256, F32),
             (1024, BF16), (256, F32), (MLA_ROPE, F32)] + ([(1024, BF16)] if with_kv else [])
    out_specs = [row(wd) for wd, _ in out_w]
    out_shape = [jax.ShapeDtypeStruct((n, wd), dt) for wd, dt in out_w]
    if with_kv:
        out_specs.append(pl.BlockSpec((tm // ATT_TILE, MLA_H * MLA_DV, ATT_TILE), lambda i: (i, 0, 0)))
        out_shape.append(jax.ShapeDtypeStruct((n // ATT_TILE, MLA_H * MLA_DV, ATT_TILE), BF16))
    return pl.pallas_call(
        _in_proj_kernel,
        grid=(n // tm,),
        in_specs=[row(D_MODEL)] + [_full(c.shape) for c in consts] + [tab, tab, tab],
        out_specs=out_specs,
        out_shape=out_shape,
        compiler_params=_cparams(("parallel",)),
        name="in_proj",
    )(x, *consts, *tabs)


def _head_masks():
    lane = lax.broadcasted_iota(jnp.int32, (1, GLA_H * GLA_K), 1)
    return [(lane // GLA_K == h).astype(F32) for h in range(GLA_H)]


def _gla_chunk(q, k, v, la, st, ltri, bexp, c_len, sub):
    masks = _head_masks()
    if ltri is None:
        rows = lax.broadcasted_iota(jnp.int32, (c_len, 1), 0)
        b = jnp.zeros_like(la)
        for j in range(c_len):
            b = b + jnp.where(rows >= j, la[j:j + 1, :], 0.0)
    else:
        la_hi = la.astype(BF16)
        la_lo = (la - la_hi.astype(F32)).astype(BF16)
        b = _dot(ltri, la_hi) + _dot(ltri, la_lo)
    bl = b[c_len - 1:c_len, :]

    qh = q * jnp.exp(b)
    lq = jnp.concatenate([qh * m for m in masks], axis=0).astype(BF16)
    o_inter = _dot_nt(lq, st.astype(BF16))

    nsub = c_len // sub
    a_rows = [[] for _ in range(GLA_H)]
    col = lax.broadcasted_iota(jnp.int32, (1, c_len), 1)
    for i in range(1, nsub):
        ref = b[sub * i - 1:sub * i, :]
        qi = q[sub * i:sub * (i + 1)] * jnp.exp(b[sub * i:sub * (i + 1)] - ref)
        ki = k * jnp.exp(jnp.minimum(ref - b, 0.0))
        li = jnp.concatenate([qi * m for m in masks], axis=0).astype(BF16)
        ai = jnp.where(col < sub * i, _dot_nt(li, ki.astype(BF16)), 0.0)
        for h in range(GLA_H):
            a_rows[h].append(ai[sub * h:sub * (h + 1)])

    row = lax.broadcasted_iota(jnp.int32, (sub, 1), 0)
    o_diag = []
    for i in range(nsub):
        sl = slice(sub * i, sub * (i + 1))
        bb, qb, kb, vb = b[sl], q[sl], k[sl], v[sl]
        ps = []
        for s in range(sub):
            e = jnp.exp(jnp.where(row >= s, bb - bb[s:s + 1, :], NEG))
            ps.append(qb * e * kb[s:s + 1, :])
        r = _dot(jnp.concatenate(ps, axis=0).astype(BF16), bexp)
        od = r[0:sub] * vb[0:1, :]
        for s in range(1, sub):
            od = od + r[sub * s:sub * (s + 1)] * vb[s:s + 1, :]
        o_diag.append(od)
    o_diag = jnp.concatenate(o_diag, axis=0) if nsub > 1 else o_diag[0]

    outs = []
    for h in range(GLA_H):
        oh = o_inter[c_len * h:c_len * (h + 1)] + o_diag[:, GLA_V * h:GLA_V * (h + 1)]
        if nsub > 1:
            ah = jnp.concatenate([jnp.zeros((sub, c_len), F32)] + a_rows[h], axis=0)
            oh = oh + _dot(ah.astype(BF16), v[:, GLA_V * h:GLA_V * (h + 1)].astype(BF16))
        outs.append(oh)

    kl = k * jnp.exp(bl - b)
    vs = jnp.concatenate([v[:, GLA_V * h:GLA_V * (h + 1)] for h in range(GLA_H)], axis=0)
    ks = jnp.concatenate([kl * m for m in masks], axis=0).astype(BF16)
    st_new = st * jnp.exp(bl) + _dot(vs.T.astype(BF16), ks)
    return outs, st_new


def _gla_kernel(*refs, c_len, sub, cps, use_tri):
    if use_tri:
        q_ref, k_ref, v_ref, r_ref, la_ref, s0_ref, g_ref, bexp_ref, ltri_ref, og_o, sf_o, st_ref = refs
    else:
        q_ref, k_ref, v_ref, r_ref, la_ref, s0_ref, g_ref, bexp_ref, og_o, sf_o, st_ref = refs
        ltri_ref = None
    t = pl.program_id(1)

    @pl.when(t == 0)
    def _():
        st_ref[...] = s0_ref[...].T

    g = g_ref[...]
    bexp = bexp_ref[...]

    def body(c, carry):
        rs = pl.ds(pl.multiple_of(c * c_len, c_len), c_len)
        ltri = ltri_ref[...] if use_tri else None
        outs, st_new = _gla_chunk(q_ref[rs, :].astype(F32), k_ref[rs, :].astype(F32), v_ref[rs, :].astype(F32),
                                  la_ref[rs, :], st_ref[...], ltri, bexp, c_len, sub)
        st_ref[...] = st_new
        rr = r_ref[rs, :].astype(F32)
        for h in range(GLA_H):
            hs = slice(GLA_V * h, GLA_V * (h + 1))
            rh = rr[:, hs]
            og_o[rs, hs] = (_rms(outs[h], g) * (rh / (1.0 + jnp.exp(-rh)))).astype(BF16)
        return carry

    lax.fori_loop(0, cps, body, 0)

    @pl.when(t == pl.num_programs(1) - 1)
    def _():
        sf_o[...] = st_ref[...].T


def _gla(gq, gk, gv, gr, la, s0, w, batch, seq, c_len, sub, cps):
    ts = c_len * cps
    nt = seq // ts
    use_tri = c_len > 8
    row = lambda width: pl.BlockSpec((None, ts, width), lambda b, t: (b * nt + t, 0, 0))
    g3 = lambda a: a.reshape(batch * nt, ts, a.shape[-1])
    st_spec = pl.BlockSpec((None, GLA_H * GLA_K, GLA_V), lambda b, t: (b, 0, 0))
    consts = [w['g_gla_o'], w['bexp']] + ([w['ltri']] if use_tri else [])
    og, s_fin = pl.pallas_call(
        functools.partial(_gla_kernel, c_len=c_len, sub=sub, cps=cps, use_tri=use_tri),
        grid=(batch, nt),
        in_specs=[row(256), row(256), row(512), row(512), row(256), st_spec] + [_full(c.shape) for c in consts],
        out_specs=[row(512), st_spec],
        out_shape=[jax.ShapeDtypeStruct((batch * nt, ts, GLA_H * GLA_V), BF16),
                   jax.ShapeDtypeStruct((batch, GLA_H * GLA_K, GLA_V), F32)],
        scratch_shapes=[pltpu.VMEM((GLA_V, GLA_H * GLA_K), F32)],
        compiler_params=_cparams(("parallel", "arbitrary")),
        name="gla",
    )(g3(gq), g3(gk), g3(gv), g3(gr), g3(la), s0, *consts)
    return og.reshape(batch * seq, GLA_H * GLA_V), s_fin


def _mla_prompt_kernel(q_ref, k_ref, vt_ref, o_ref, m_ref, l_ref, acc_ref, *, tq):
    i = pl.program_id(1)
    m_ref[...] = jnp.full(m_ref.shape, -jnp.inf, F32)
    l_ref[...] = jnp.zeros(l_ref.shape, F32)
    acc_ref[...] = jnp.zeros(acc_ref.shape, F32)
    keep = (lax.broadcasted_iota(jnp.int32, (tq, tq), 0) <= lax.broadcasted_iota(jnp.int32, (tq, tq), 1))

    def tile(j, masked):
        rs = pl.ds(pl.multiple_of(j * tq, tq), tq)
        sts = [_dot_nt(k_ref[rs, LANES * h:LANES * (h + 1)], q_ref[:, LANES * h:LANES * (h + 1)])
               for h in range(MLA_H)]
        ps, alphas = [], []
        for h in range(MLA_H):
            st = jnp.where(keep, sts[h], -jnp.inf) if masked else sts[h]
            m_old = m_ref[h]
            m_new = jnp.maximum(m_old, jnp.max(st, axis=0, keepdims=True))
            alpha = jnp.exp2(m_old - m_new)
            p = jnp.exp2(st - m_new)
            m_ref[h] = m_new
            l_ref[h] = alpha * l_ref[h] + jnp.sum(p, axis=0, keepdims=True)
            ps.append(p.astype(BF16))
            alphas.append(alpha)
        for h in range(MLA_H):
            vt = vt_ref[j, LANES * (h // 2):LANES * (h // 2 + 1), :]
            acc_ref[h] = alphas[h] * acc_ref[h] + _dot(vt, ps[h])

    def body(j, carry):
        tile(j, False)
        return carry

    lax.fori_loop(0, i, body, 0)
    tile(i, True)

    row = lax.broadcasted_iota(jnp.int32, (LANES, 1), 0)
    for pair in range(MLA_H // 2):
        h0, h1 = 2 * pair, 2 * pair + 1
        a = acc_ref[h0] * (1.0 / l_ref[h0])
        b = acc_ref[h1] * (1.0 / l_ref[h1])
        o_ref[:, LANES * pair:LANES * (pair + 1)] = jnp.where(row < MLA_DV, a, b).T.astype(BF16)


def _mla_prompt(mq, mk, mvt, batch, seq, tq):
    nq = seq // tq
    return pl.pallas_call(
        functools.partial(_mla_prompt_kernel, tq=tq),
        grid=(batch, nq),
        in_specs=[pl.BlockSpec((tq, MLA_H * LANES), lambda b, i: (b * nq + i, 0)),
                  pl.BlockSpec((seq, MLA_H * LANES), lambda b, i: (b, 0)),
                  pl.BlockSpec((nq, MLA_H * MLA_DV, tq), lambda b, i: (b, 0, 0))],
        out_specs=pl.BlockSpec((tq, MLA_H * MLA_DV), lambda b, i: (b * nq + i, 0)),
        out_shape=jax.ShapeDtypeStruct((batch * seq, MLA_H * MLA_DV), BF16),
        scratch_shapes=[pltpu.VMEM((MLA_H, 1, tq), F32), pltpu.VMEM((MLA_H, 1, tq), F32),
                        pltpu.VMEM((MLA_H, LANES, tq), F32)],
        compiler_params=_cparams(("parallel", "arbitrary")),
        name="mla_prompt",
    )(mq, mk, mvt)


def _mla_sample_kernel(pt_ref, q_ref, wkq_ref, wkt_ref, wv_ref, cn_ref, kn_ref, *rest, pps, t_new, kpm):
    ckv_refs = rest[:pps]
    kpe_refs = rest[pps:2 * pps]
    o_ref, lq_ref, qp_ref, m_ref, l_ref, acc_ref = rest[2 * pps:]
    c = pl.program_id(1)
    nrow = MLA_H * t_new

    @pl.when(c == 0)
    def _():
        for h in range(MLA_H):
            qh = q_ref[:, LANES * h:LANES * (h + 1)]
            lq_ref[t_new * h:t_new * (h + 1), :] = _dot(qh, wkq_ref[h]).astype(BF16)
            qp_ref[t_new * h:t_new * (h + 1), :] = pltpu.roll(qh.astype(F32), LANES - _KPE_LANE, 1).astype(BF16)
        lq_ref[nrow:, :] = wkt_ref[...]
        m_ref[...] = jnp.full(m_ref.shape, -jnp.inf, F32)
        l_ref[...] = jnp.zeros(l_ref.shape, F32)
        acc_ref[...] = jnp.zeros(acc_ref.shape, F32)

    lq = lq_ref[...]
    qp = qp_ref[:, 0:MLA_ROPE]

    def scores(r, s_rope):
        rinv = []
        for h in range(MLA_H):
            kr = r[nrow + MLA_NOPE * h:nrow + MLA_NOPE * (h + 1)]
            ms = jnp.sum(kr * kr, axis=0, keepdims=True) * (1.0 / MLA_NOPE)
            rinv.append(jnp.broadcast_to(lax.rsqrt(ms + EPS), (t_new, r.shape[1])))
        return r[0:nrow] * jnp.concatenate(rinv, axis=0) + s_rope

    def update(s, x):
        m_old = m_ref[...]
        m_new = jnp.maximum(m_old, jnp.max(s, axis=-1, keepdims=True))
        alpha = jnp.exp2(m_old - m_new)
        p = jnp.exp2(s - m_new)
        m_ref[...] = m_new
        l_ref[...] = alpha * l_ref[...] + jnp.sum(p, axis=-1, keepdims=True)
        acc_ref[...] = alpha * acc_ref[...] + _dot(p.astype(BF16), x)

    ngrp = pps // kpm
    xs = [jnp.concatenate([ckv_refs[g * kpm + j][...] for j in range(kpm)], axis=0).astype(BF16)
          for g in range(ngrp)]
    rs = [_dot_nt(lq, x) for x in xs]
    ropes = [_dot(qp, jnp.concatenate([kpe_refs[g * kpm + j][...] for j in range(kpm)], axis=1).astype(BF16))
             for g in range(ngrp)]
    ss = [scores(rs[g], ropes[g]) for g in range(ngrp)]
    update(jnp.concatenate(ss, axis=1), jnp.concatenate(xs, axis=0))

    @pl.when(c == pl.num_programs(1) - 1)
    def _():
        x = cn_ref[...].astype(BF16)
        s = scores(_dot_nt(lq, x), _dot_nt(qp, kn_ref[...].astype(BF16)))
        tq = lax.broadcasted_iota(jnp.int32, (nrow, t_new), 0) % t_new
        ts = lax.broadcasted_iota(jnp.int32, (nrow, t_new), 1)
        update(jnp.where(ts <= tq, s, -jnp.inf), x)
        oa = (acc_ref[...] / l_ref[...]).astype(BF16)
        r = _dot(oa, wv_ref[...])
        lane = lax.broadcasted_iota(jnp.int32, (1, MLA_H * MLA_DV), 1)
        out = jnp.zeros((t_new, MLA_H * MLA_DV), F32)
        for h in range(MLA_H):
            out = out + jnp.where(lane // MLA_DV == h, r[t_new * h:t_new * (h + 1)], 0.0)
        o_ref[...] = out.astype(BF16)


def _mla_sample(mq, ckv_new, kp_new, pool_ckv, pool_kpe_t, page_table, w, batch, t_new, pps, kpm):
    n_pages = page_table.shape[1]
    nch = n_pages // pps
    pt = page_table.reshape(-1)
    nrow = MLA_H * t_new

    def page_spec(shape, i):
        return pl.BlockSpec((None,) + shape, lambda b, c, pt_ref: (pt_ref[b * n_pages + c * pps + i], 0, 0))

    tok = lambda width: pl.BlockSpec((None, t_new, width), lambda b, c, pt_ref: (b, 0, 0))
    g3 = lambda a: a.reshape(batch, t_new, a.shape[-1])
    const = lambda a: pl.BlockSpec(a.shape, lambda b, c, pt_ref: (0,) * a.ndim)
    grid_spec = pltpu.PrefetchScalarGridSpec(
        num_scalar_prefetch=1,
        grid=(batch, nch),
        in_specs=[tok(MLA_H * LANES), const(w['w_kq']), const(w['w_kt']), const(w['w_v']),
                  tok(MLA_KVR), tok(MLA_ROPE)]
                 + [page_spec((PAGE, MLA_KVR), i) for i in range(pps)]
                 + [page_spec((MLA_ROPE, PAGE), i) for i in range(pps)],
        out_specs=tok(MLA_H * MLA_DV),
        scratch_shapes=[pltpu.VMEM((nrow + MLA_H * MLA_NOPE, MLA_KVR), BF16),
                        pltpu.VMEM((nrow, LANES), BF16),
                        pltpu.VMEM((nrow, 1), F32), pltpu.VMEM((nrow, 1), F32),
                        pltpu.VMEM((nrow, MLA_KVR), F32)],
    )
    om = pl.pallas_call(
        functools.partial(_mla_sample_kernel, pps=pps, t_new=t_new, kpm=kpm),
        grid_spec=grid_spec,
        out_shape=jax.ShapeDtypeStruct((batch, t_new, MLA_H * MLA_DV), BF16),
        compiler_params=_cparams(("parallel", "arbitrary")),
        name="mla_sample",
    )(pt, g3(mq), w['w_kq'], w['w_kt'], w['w_v'], g3(ckv_new), g3(kp_new),
      *([pool_ckv] * pps), *([pool_kpe_t] * pps))
    return om.reshape(batch * t_new, MLA_H * MLA_DV)


def _mix_out_kernel(x_ref, og_ref, om_ref, wo_ref, gx_ref, wxq_ref, gxq_ref, x1_o, xq_o):
    x1 = x_ref[...] + _dot(og_ref[...], wo_ref[0:512, :]) + _dot(om_ref[...], wo_ref[512:1024, :])
    x1_o[...] = x1
    qf = _dot(_rms(x1, gx_ref[...]).astype(BF16), wxq_ref[...])
    g = gxq_ref[...]
    for h in range(X_H):
        hs = slice(X_D * h, X_D * (h + 1))
        xq_o[:, hs] = _rms(qf[:, hs], g).astype(BF16)


def _mix_out(x, og, om, w, tm):
    n = x.shape[0]
    row = lambda width: pl.BlockSpec((tm, width), lambda i: (i, 0))
    consts = [w['w_out'], w['g_x'], w['w_xq'], w['g_xq']]
    return pl.pallas_call(
        _mix_out_kernel,
        grid=(n // tm,),
        in_specs=[row(D_MODEL), row(512), row(512)] + [_full(c.shape) for c in consts],
        out_specs=[row(D_MODEL), row(X_H * X_D)],
        out_shape=[jax.ShapeDtypeStruct((n, D_MODEL), F32), jax.ShapeDtypeStruct((n, X_H * X_D), BF16)],
        compiler_params=_cparams(("parallel",)),
        name="mix_out",
    )(x, og, om, *consts)


def _cross_kernel(xq_ref, mk_ref, mv_ref, o_ref, *, n_mem):
    def head(ref, h):
        if len(ref.shape) == 3:
            return ref[h]
        return ref[pl.ds(h, n_mem, stride=X_H), :].astype(BF16)

    for h in range(X_H):
        hs = slice(X_D * h, X_D * (h + 1))
        s = _dot_nt(xq_ref[:, hs], head(mk_ref, h))
        p = jnp.exp2(s - jnp.max(s, axis=-1, keepdims=True))
        o = _dot(p.astype(BF16), head(mv_ref, h))
        o_ref[:, hs] = (o * (1.0 / jnp.sum(p, axis=-1, keepdims=True))).astype(BF16)


def _cross(xq, mk, mv, batch, seq, n_mem, tq):
    nq = seq // tq
    row = pl.BlockSpec((None, tq, X_H * X_D), lambda b, i: (b * nq + i, 0, 0))
    if mk.ndim == 4:
        mem = pl.BlockSpec((None,) + mk.shape[1:], lambda b, i: (b, 0, 0, 0))
    else:
        mem = pl.BlockSpec((n_mem * X_H, X_D), lambda b, i: (b, 0))
    o = pl.pallas_call(
        functools.partial(_cross_kernel, n_mem=n_mem),
        grid=(batch, nq),
        in_specs=[row, mem, mem],
        out_specs=row,
        out_shape=jax.ShapeDtypeStruct((batch * nq, tq, X_H * X_D), BF16),
        compiler_params=_cparams(("parallel", "arbitrary")),
        name="cross",
    )(xq.reshape(batch * nq, tq, X_H * X_D), mk, mv)
    return o.reshape(batch * seq, X_H * X_D)


def _ffn_kernel(x_ref, o_ref, wxo_ref, g_ref, w1_ref, w2_ref, y_o, *, fc):
    x = x_ref[...] + _dot(o_ref[...], wxo_ref[...])
    h = _rms(x, g_ref[...]).astype(BF16)
    acc = x
    for c in range(D_FF // fc):
        u = jnp.maximum(_dot(h, w1_ref[:, fc * c:fc * (c + 1)]), 0.0)
        acc = acc + _dot((u * u).astype(BF16), w2_ref[fc * c:fc * (c + 1), :])
    y_o[...] = acc


def _ffn(x, o, w, tm, fc):
    n = x.shape[0]
    row = lambda width: pl.BlockSpec((tm, width), lambda i: (i, 0))
    consts = [w['w_xo'], w['g_ff'], w['w_ff1'], w['w_ff2']]
    return pl.pallas_call(
        functools.partial(_ffn_kernel, fc=fc),
        grid=(n // tm,),
        in_specs=[row(D_MODEL), row(X_H * X_D)] + [_full(c.shape) for c in consts],
        out_specs=row(D_MODEL),
        out_shape=jax.ShapeDtypeStruct((n, D_MODEL), F32),
        compiler_params=_cparams(("parallel",)),
        name="ffn",
    )(x, o, *consts)


def _mem_kv_kernel(m_ref, gm_ref, wk_ref, wv_ref, gk_ref, k_o, v_o, kh_o, vh_o, *, n_mem):
    m = _rms(m_ref[...], gm_ref[...]).astype(BF16)
    kf = _dot(m, wk_ref[...])
    vf = _dot(m, wv_ref[...])
    v_o[...] = vf
    g = gk_ref[...]
    for h in range(X_H):
        hs = slice(X_D * h, X_D * (h + 1))
        kn = _rms(kf[:, hs], g)
        k_o[:, hs] = kn
        for b in range(kh_o.shape[0]):
            rs = slice(n_mem * b, n_mem * (b + 1))
            kh_o[b, h] = kn[rs].astype(BF16)
            vh_o[b, h] = vf[rs, hs].astype(BF16)


def _mem_kv(mem, w, n_mem, tm):
    n = mem.shape[0]
    bt = tm // n_mem
    row = lambda width: pl.BlockSpec((tm, width), lambda i: (i, 0))
    head = pl.BlockSpec((bt, X_H, n_mem, X_D), lambda i: (i, 0, 0, 0))
    consts = [w['g_mem'], w['w_xk'], w['w_xv'], w['g_xk']]
    return pl.pallas_call(
        functools.partial(_mem_kv_kernel, n_mem=n_mem),
        grid=(n // tm,),
        in_specs=[row(D_MODEL)] + [_full(c.shape) for c in consts],
        out_specs=[row(X_H * X_D), row(X_H * X_D), head, head],
        out_shape=[jax.ShapeDtypeStruct((n, X_H * X_D), F32)] * 2
                  + [jax.ShapeDtypeStruct((n // n_mem, X_H, n_mem, X_D), BF16)] * 2,
        compiler_params=_cparams(("parallel",)),
        name="mem_kv",
    )(mem, *consts)


def _prep_weights(g_mix, w_in, w_gla_a2, b_gla_a, g_gla_o, g_mla_qa, w_mla_qb, g_mla_kva, w_mla_kvb,
                  g_q_nope, g_k_nope, g_q_rope, g_k_rope, w_out, g_x, g_mem, w_xq, w_xk, w_xv, g_xq, g_xk,
                  w_xo, g_ff, w_ff1, w_ff2):
    rowv = lambda g: g.reshape(1, -1).astype(F32)
    zc = lambda n: jnp.zeros((D_MODEL, n), F32)
    sizes = (256, 256, 512, 512, GLA_RANK, MLA_QR, MLA_KVR, MLA_ROPE)
    offs = [0]
    for s in sizes:
        offs.append(offs[-1] + s)
    q, k, v, r, a, qa, kva, kpe = [w_in[:, offs[i]:offs[i + 1]] for i in range(8)]
    w_in_p = jnp.concatenate(
        [q * (GLA_K ** -0.5), k, v, r, qa, kva, a, zc(LANES - GLA_RANK),
         zc(_KPE_LANE), kpe, zc(LANES - _KPE_LANE - MLA_ROPE)], axis=1)
    w_a2 = jnp.concatenate([w_gla_a2, jnp.zeros((LANES - GLA_RANK, GLA_H * GLA_K), F32)], axis=0)

    qb = w_mla_qb.reshape(MLA_QR, MLA_H, MLA_NOPE + MLA_ROPE)
    qb = jnp.pad(qb, ((0, 0), (0, 0), (0, LANES - MLA_NOPE - MLA_ROPE))).reshape(MLA_QR, MLA_H * LANES)
    kvb = w_mla_kvb.reshape(MLA_KVR, MLA_H, MLA_NOPE + MLA_DV)
    wk = kvb[:, :, :MLA_NOPE]
    wk_p = jnp.pad(wk, ((0, 0), (0, 0), (0, LANES - MLA_NOPE))).reshape(MLA_KVR, MLA_H * LANES)
    wv = kvb[:, :, MLA_NOPE:].reshape(MLA_KVR, MLA_H * MLA_DV)
    z32 = jnp.zeros((LANES - MLA_NOPE - MLA_ROPE,), F32)
    gq_row = jnp.tile(jnp.concatenate([g_q_nope, g_q_rope, z32]), MLA_H) * (MLA_SCALE * LOG2E)
    gk_row = jnp.tile(jnp.concatenate([g_k_nope, jnp.zeros((LANES - MLA_NOPE,), F32)]), MLA_H)
    g_kpe = jnp.concatenate([jnp.zeros((_KPE_LANE,), F32), g_k_rope, z32])

    li = jnp.arange(2 * LANES)
    seg = jnp.where(li % LANES < MLA_NOPE, 0, jnp.where(li % LANES < MLA_NOPE + MLA_ROPE, 1, 2))
    same = (li[:, None] // LANES == li[None, :] // LANES) & (seg[:, None] == seg[None, :]) & (seg[:, None] < 2)
    bd = jnp.where(same, jnp.where(seg[:, None] == 0, 1.0 / MLA_NOPE, 1.0 / MLA_ROPE), 0.0)

    wkq = jnp.transpose(wk, (1, 2, 0)) * g_k_nope[None, :, None]
    wkq = jnp.pad(wkq, ((0, 0), (0, LANES - MLA_NOPE), (0, 0)))
    wkt = jnp.transpose(wk, (1, 2, 0)).reshape(MLA_H * MLA_NOPE, MLA_KVR)

    hk = jnp.arange(GLA_H * GLA_K) // GLA_K
    hv = jnp.arange(GLA_H * GLA_V) // GLA_V
    bexp = (hk[:, None] == hv[None, :])
    ci = jnp.arange(GLA_CHUNK)
    ltri = ci[:, None] >= ci[None, :]
    return {
        'g_mix': rowv(g_mix), 'w_in': w_in_p.astype(BF16), 'w_a2': w_a2.astype(BF16), 'b_a': rowv(b_gla_a),
        'g_qa': rowv(g_mla_qa), 'w_qb': qb.astype(BF16), 'gq_row': rowv(gq_row), 'g_kva': rowv(g_mla_kva),
        'w_k': wk_p.astype(BF16), 'gk_row': rowv(gk_row), 'w_v': wv.astype(BF16), 'g_kpe': rowv(g_kpe),
        'bd': bd.astype(BF16), 'w_kq': wkq.astype(BF16), 'w_kt': wkt.astype(BF16),
        'g_gla_o': rowv(g_gla_o), 'bexp': bexp.astype(BF16), 'ltri': ltri.astype(BF16),
        'w_out': w_out.astype(BF16), 'g_x': rowv(g_x), 'w_xq': w_xq.astype(BF16),
        'g_xq': rowv(g_xq) * (X_SCALE * LOG2E),
        'g_mem': rowv(g_mem), 'w_xk': w_xk.astype(BF16), 'w_xv': w_xv.astype(BF16), 'g_xk': rowv(g_xk),
        'w_xo': w_xo.astype(BF16), 'g_ff': rowv(g_ff), 'w_ff1': w_ff1.astype(BF16), 'w_ff2': w_ff2.astype(BF16),
    }


def _rope_tables(pos):
    half = MLA_ROPE // 2
    inv = ROPE_THETA ** (-jnp.arange(half, dtype=F32) / half)
    ang = pos.astype(F32)[:, None] * inv[None, :]
    cos, sin = jnp.cos(ang), jnp.sin(ang)
    n = pos.shape[0]
    one = jnp.ones((n, _KPE_LANE), F32)
    z = lambda w_: jnp.zeros((n, w_), F32)
    tail = LANES - _KPE_LANE - MLA_ROPE
    c = jnp.concatenate([one, cos, cos, jnp.ones((n, tail), F32)], axis=1)
    s_up = jnp.concatenate([z(_KPE_LANE), -sin, z(half), z(tail)], axis=1)
    s_dn = jnp.concatenate([z(_KPE_LANE), z(half), sin, z(tail)], axis=1)
    return c, s_up, s_dn


def _tile_rows(n, cap):
    t = min(n, cap)
    while n % t:
        t //= 2
    return t


def _tail(x1, xq, mk, mv, w, batch, seq, n_mem):
    o = _cross(xq, mk, mv, batch, seq, n_mem, _tile_rows(seq, 512))
    return _ffn(x1, o, w, _tile_rows(x1.shape[0], 512), 1024)


def _prompt_layer(x, mem, w):
    batch, seq, _ = x.shape
    n = batch * seq
    xf = x.reshape(n, D_MODEL)
    tm = _tile_rows(seq, 512)
    tabs = _rope_tables(jnp.arange(seq))
    gq, gk, gv, gr, la, mq, ckv, kp, mk_, mvt = _in_proj(xf, w, tabs, tm, True)
    c_len = GLA_CHUNK
    cps = _tile_rows(seq // c_len, 4)
    s0 = jnp.zeros((batch, GLA_H * GLA_K, GLA_V), F32)
    og, s_fin = _gla(gq, gk, gv, gr, la, s0, w, batch, seq, c_len, min(GLA_SUB, c_len), cps)
    om = _mla_prompt(mq, mk_, mvt, batch, seq, ATT_TILE)
    x1, xq = _mix_out(xf, og, om, w, tm)
    n_mem = mem.shape[1]
    memf = mem.reshape(-1, D_MODEL)
    xk, xv, xkh, xvh = _mem_kv(memf, w, n_mem, max(n_mem, _tile_rows(memf.shape[0], 512)))
    y = _tail(x1, xq, xkh, xvh, w, batch, seq, n_mem)
    return y, ckv, kp, xk, xv, s_fin


def _sample_layer(x, pool_ckv, pool_kpe, page_table, mem_k, mem_v, s_prev, w):
    batch, seq, _ = x.shape
    n = batch * seq
    xf = x.reshape(n, D_MODEL)
    tm = _tile_rows(n, 512)
    past = page_table.shape[1] * PAGE
    pos = past + (jnp.arange(tm) % seq)
    gq, gk, gv, gr, la, mq, ckv, kp = _in_proj(xf, w, _rope_tables(pos), tm, False)
    s0 = s_prev.reshape(batch, GLA_H * GLA_K, GLA_V)
    og, s_fin = _gla(gq, gk, gv, gr, la, s0, w, batch, seq, seq, seq, 1)
    n_pages = page_table.shape[1]
    pps = _tile_rows(n_pages, 16)
    om = _mla_sample(mq, ckv, kp, pool_ckv, jnp.swapaxes(pool_kpe, 1, 2), page_table, w, batch, seq,
                     pps, min(pps, 4))
    x1, xq = _mix_out(xf, og, om, w, tm)
    y = _tail(x1, xq, mem_k.reshape(-1, X_D), mem_v.reshape(-1, X_D), w, batch, seq, mem_k.shape[1])
    return y, ckv, kp, s_fin


def kernel(x_prompt, x_sample, mem_prompt, cache_ckv, cache_kpe, cache_mem_k, cache_mem_v, state_gla, page_table, g_mix, w_in, w_gla_a2, b_gla_a, g_gla_o, g_mla_qa, w_mla_qb, g_mla_kva, w_mla_kvb, g_q_nope, g_k_nope, g_q_rope, g_k_rope, w_out, g_x, g_mem, w_xq, w_xk, w_xv, g_xq, g_xk, w_xo, g_ff, w_ff1, w_ff2):
    depth = w_in.shape[0]
    assert depth == 1, "one layer: prompt-group caches of layer l would feed layer l+1 otherwise unchanged"
    params = (g_mix, w_in, w_gla_a2, b_gla_a, g_gla_o, g_mla_qa, w_mla_qb, g_mla_kva, w_mla_kvb,
              g_q_nope, g_k_nope, g_q_rope, g_k_rope, w_out, g_x, g_mem, w_xq, w_xk, w_xv, g_xq, g_xk,
              w_xo, g_ff, w_ff1, w_ff2)
    w = _prep_weights(*[p[0] for p in params])
    bp, tp, _ = x_prompt.shape
    bs, tsq, _ = x_sample.shape
    yp, ckv_p, kp_p, xk, xv, gla_p = _prompt_layer(x_prompt, mem_prompt, w)
    ys, ckv_s, kp_s, gla_s = _sample_layer(x_sample, cache_ckv.reshape(cache_ckv.shape[1:]), cache_kpe.reshape(cache_kpe.shape[1:]), page_table,
                                            cache_mem_k[0], cache_mem_v[0], state_gla[0], w)
    n_mem = mem_prompt.shape[1]
    return (yp.reshape(bp, tp, D_MODEL), ys.reshape(bs, tsq, D_MODEL),
            ckv_p.reshape(1, bp, tp, MLA_KVR), kp_p.reshape(1, bp, tp, MLA_ROPE),
            xk.reshape(1, bp, n_mem, X_H, X_D), xv.reshape(1, bp, n_mem, X_H, X_D),
            gla_p.reshape(1, bp, GLA_H, GLA_K, GLA_V),
            ckv_s.reshape(1, bs, tsq, MLA_KVR), kp_s.reshape(1, bs, tsq, MLA_ROPE),
            gla_s.reshape(1, bs, GLA_H, GLA_K, GLA_V))
```

```python
import functools

import jax
import jax.numpy as jnp
from jax import lax
from jax.experimental import pallas as pl
from jax.experimental.pallas import tpu as pltpu

F32 = jnp.float32
BF16 = jnp.bfloat16

EPS = 1e-6
D_MODEL = 1024
GLA_H, GLA_K, GLA_V, GLA_RANK, GLA_TAU, GLA_CHUNK = 4, 64, 128, 16, 16.0, 64
GLA_SUB = 16
GLA_FAST_MAX = 40.0
MLA_H, MLA_DV, MLA_NOPE, MLA_ROPE, MLA_QR, MLA_KVR = 8, 64, 64, 32, 384, 256
MLA_SCALE = (MLA_NOPE + MLA_ROPE) ** -0.5
ROPE_THETA = 10000.0
X_H, X_D = 4, 128
X_SCALE = X_D ** -0.5
D_FF = 4096
PAGE = 128
LANES = 128
NEG = -1e30
LOG2E = 1.4426950408889634
ATT_TILE = 256
CROSS_ROWS = 32
SAMPLE_PAGES = 64
SAMPLE_PAGES_PER_DOT = 4
SAMPLE_GROUPS = 4

_C_Q, _C_K, _C_V, _C_R = 0, 256, 512, 1024
_C_QA, _C_KVA, _C_A, _C_KPE, _C_END = 1536, 1920, 2176, 2304, 2432
_KPE_LANE = 64

VMEM_LIMIT = 56 * 1024 * 1024


def _cparams(sem):
    return pltpu.CompilerParams(dimension_semantics=sem, vmem_limit_bytes=VMEM_LIMIT)


def _dot(a, b):
    return jnp.dot(a, b, preferred_element_type=F32)


def _dot_nt(a, b):
    return lax.dot_general(a, b, (((1,), (1,)), ((), ())), preferred_element_type=F32)


def _rms(x, g):
    return x * lax.rsqrt(jnp.mean(x * x, axis=-1, keepdims=True) + EPS) * g


def _full(shape):
    n = len(shape)
    return pl.BlockSpec(shape, lambda *_: (0,) * n)


def _in_proj_kernel(x_ref, gmix_ref, win_ref, wa2_ref, ba_ref, gqa_ref, wqb_ref, gqrow_ref,
                    gkva_ref, wk_ref, gkrow_ref, wv_ref, gkpe_ref, bd_ref, c_ref, s1_ref, s2_ref,
                    gq_o, gk_o, gv_o, gr_o, la_o, mq_o, ckv_o, kp_o, *kv_o):
    h = _rms(x_ref[...], gmix_ref[...]).astype(BF16)

    def proj(lo, hi):
        return _dot(h, win_ref[:, lo:hi])

    gq_o[...] = proj(_C_Q, _C_K).astype(BF16)
    gk_o[...] = proj(_C_K, _C_V).astype(BF16)
    gv_o[...] = proj(_C_V, _C_R).astype(BF16)
    gr_o[...] = proj(_C_R, _C_QA).astype(BF16)

    gate = _dot(proj(_C_A, _C_KPE).astype(BF16), wa2_ref[...]) + ba_ref[...]
    la_o[...] = (jnp.minimum(gate, 0.0) - jnp.log1p(jnp.exp(-jnp.abs(gate)))) * (1.0 / GLA_TAU)

    cos, s_up, s_dn = c_ref[...], s1_ref[...], s2_ref[...]

    def rope(t):
        return t * cos + pltpu.roll(t, LANES - 16, 1) * s_up + pltpu.roll(t, 16, 1) * s_dn

    zk = proj(_C_KPE, _C_END)
    kpn = zk * lax.rsqrt(jnp.sum(zk * zk, axis=-1, keepdims=True) * (1.0 / MLA_ROPE) + EPS) * gkpe_ref[...]
    kpt = rope(kpn)
    kp_o[...] = pltpu.roll(kpt, LANES - _KPE_LANE, 1)[:, 0:MLA_ROPE]

    ckv = _rms(proj(_C_KVA, _C_A), gkva_ref[...])
    ckv_o[...] = ckv
    cb = ckv.astype(BF16)
    bd = bd_ref[...]

    def seg_norm(t, grow):
        ms = _dot((t * t).astype(BF16), bd)
        return t * lax.rsqrt(ms + EPS) * grow

    qf = _dot(_rms(proj(_C_QA, _C_KVA), gqa_ref[...]).astype(BF16), wqb_ref[...])
    for c in range(4):
        sl = slice(2 * LANES * c, 2 * LANES * (c + 1))
        qn = seg_norm(qf[:, sl], gqrow_ref[:, sl])
        for j in range(2):
            o = 2 * LANES * c + LANES * j
            mq_o[:, o:o + LANES] = rope(qn[:, LANES * j:LANES * (j + 1)]).astype(BF16)

    if kv_o:
        mk_o, mvt_o = kv_o
        mv = _dot(cb, wv_ref[...])
        for c in range(mvt_o.shape[0]):
            mvt_o[c] = mv[ATT_TILE * c:ATT_TILE * (c + 1), :].T.astype(BF16)
        kf = _dot(cb, wk_ref[...])
        for c in range(4):
            sl = slice(2 * LANES * c, 2 * LANES * (c + 1))
            kn = seg_norm(kf[:, sl], gkrow_ref[:, sl])
            for j in range(2):
                o = 2 * LANES * c + LANES * j
                mk_o[:, o:o + LANES] = (kn[:, LANES * j:LANES * (j + 1)] + kpt).astype(BF16)


def _in_proj(x, w, tabs, tm, with_kv):
    n = x.shape[0]
    nt = tabs[0].shape[0] // tm
    row = lambda width: pl.BlockSpec((tm, width), lambda i: (i, 0))
    tab = pl.BlockSpec((tm, LANES), lambda i: (i % nt, 0))
    consts = [w['g_mix'], w['w_in'], w['w_a2'], w['b_a'], w['g_qa'], w['w_qb'], w['gq_row'],
              w['g_kva'], w['w_k'], w['gk_row'], w['w_v'], w['g_kpe'], w['bd']]
    out_w = [(256, BF16), (256, BF16), (512, BF16), (512, BF16), (256, F32),
             (1024, BF16), (256, F32), (MLA_ROPE, F32)] + ([(1024, BF16)] if with_kv else [])
    out_specs = [row(wd) for wd, _ in out_w]
    out_shape = [jax.ShapeDtypeStruct((n, wd), dt) for wd, dt in out_w]
    if with_kv:
        out_specs.append(pl.BlockSpec((tm // ATT_TILE, MLA_H * MLA_DV, ATT_TILE), lambda i: (i, 0, 0)))
        out_shape.append(jax.ShapeDtypeStruct((n // ATT_TILE, MLA_H * MLA_DV, ATT_TILE), BF16))
    return pl.pallas_call(
        _in_proj_kernel,
        grid=(n // tm,),
        in_specs=[row(D_MODEL)] + [_full(c.shape) for c in consts] + [tab, tab, tab],
        out_specs=out_specs,
        out_shape=out_shape,
        compiler_params=_cparams(("parallel",)),
        name="in_proj",
    )(x, *consts, *tabs)


def _head_masks():
    lane = lax.broadcasted_iota(jnp.int32, (1, GLA_H * GLA_K), 1)
    return [(lane // GLA_K == h).astype(F32) for h in range(GLA_H)]


def _gla_fast_step(q, k, v, b, st, keep, c_len, cps):
    masks = _head_masks()
    ts = c_len * cps
    eb = jnp.exp(b)
    qt = q * eb
    kt = k * jnp.exp(-b)
    lq = jnp.concatenate([qt * m for m in masks], axis=0).astype(BF16)
    a = _dot_nt(lq, kt.astype(BF16))
    vb = v.astype(BF16)
    o_intra = [_dot((a[ts * h:ts * (h + 1)] * keep).astype(BF16), vb[:, GLA_V * h:GLA_V * (h + 1)])
               for h in range(GLA_H)]
    o_inter = []
    for c in range(cps):
        cs = slice(c_len * c, c_len * (c + 1))
        lqc = jnp.concatenate([lq[ts * h + c_len * c:ts * h + c_len * (c + 1)] for h in range(GLA_H)], axis=0)
        o_inter.append(_dot_nt(lqc, st.astype(BF16)))
        ebl = eb[c_len * (c + 1) - 1:c_len * (c + 1), :]
        kl = kt[cs] * ebl
        vs = jnp.concatenate([v[cs, GLA_V * h:GLA_V * (h + 1)] for h in range(GLA_H)], axis=0)
        ks = jnp.concatenate([kl * m for m in masks], axis=0).astype(BF16)
        st = st * ebl + _dot(vs.T.astype(BF16), ks)
    outs = [o_intra[h] + jnp.concatenate([o_inter[c][c_len * h:c_len * (h + 1)] for c in range(cps)], axis=0)
            for h in range(GLA_H)]
    return outs, st


def _gla_chunk(q, k, v, b, st, bexp, c_len, sub):
    masks = _head_masks()
    bl = b[c_len - 1:c_len, :]

    qh = q * jnp.exp(b)
    lq = jnp.concatenate([qh * m for m in masks], axis=0).astype(BF16)
    o_inter = _dot_nt(lq, st.astype(BF16))

    nsub = c_len // sub
    a_rows = [[] for _ in range(GLA_H)]
    col = lax.broadcasted_iota(jnp.int32, (1, c_len), 1)
    for i in range(1, nsub):
        ref = b[sub * i - 1:sub * i, :]
        qi = q[sub * i:sub * (i + 1)] * jnp.exp(b[sub * i:sub * (i + 1)] - ref)
        ki = k * jnp.exp(jnp.minimum(ref - b, 0.0))
        li = jnp.concatenate([qi * m for m in masks], axis=0).astype(BF16)
        ai = jnp.where(col < sub * i, _dot_nt(li, ki.astype(BF16)), 0.0)
        for h in range(GLA_H):
            a_rows[h].append(ai[sub * h:sub * (h + 1)])

    row = lax.broadcasted_iota(jnp.int32, (sub, 1), 0)
    o_diag = []
    for i in range(nsub):
        sl = slice(sub * i, sub * (i + 1))
        bb, qb, kb, vb = b[sl], q[sl], k[sl], v[sl]
        ps = []
        for s in range(sub):
            e = jnp.exp(jnp.where(row >= s, bb - bb[s:s + 1, :], NEG))
            ps.append(qb * e * kb[s:s + 1, :])
        r = _dot(jnp.concatenate(ps, axis=0).astype(BF16), bexp)
        od = r[0:sub] * vb[0:1, :]
        for s in range(1, sub):
            od = od + r[sub * s:sub * (s + 1)] * vb[s:s + 1, :]
        o_diag.append(od)
    o_diag = jnp.concatenate(o_diag, axis=0) if nsub > 1 else o_diag[0]

    outs = []
    for h in range(GLA_H):
        oh = o_inter[c_len * h:c_len * (h + 1)] + o_diag[:, GLA_V * h:GLA_V * (h + 1)]
        if nsub > 1:
            ah = jnp.concatenate([jnp.zeros((sub, c_len), F32)] + a_rows[h], axis=0)
            oh = oh + _dot(ah.astype(BF16), v[:, GLA_V * h:GLA_V * (h + 1)].astype(BF16))
        outs.append(oh)

    kl = k * jnp.exp(bl - b)
    vs = jnp.concatenate([v[:, GLA_V * h:GLA_V * (h + 1)] for h in range(GLA_H)], axis=0)
    ks = jnp.concatenate([kl * m for m in masks], axis=0).astype(BF16)
    st_new = st * jnp.exp(bl) + _dot(vs.T.astype(BF16), ks)
    return outs, st_new


def _gla_kernel(*refs, c_len, sub, cps, fast):
    if fast:
        (q_ref, k_ref, v_ref, r_ref, la_ref, s0_ref, g_ref, bexp_ref, tri_ref, keep_ref,
         og_o, sf_o, st_ref, b_ref) = refs
    else:
        q_ref, k_ref, v_ref, r_ref, la_ref, s0_ref, g_ref, bexp_ref, og_o, sf_o, st_ref, b_ref = refs
    t = pl.program_id(1)
    ts = c_len * cps

    @pl.when(t == 0)
    def _():
        st_ref[...] = s0_ref[...].T

    g = g_ref[...]

    def emit(outs, rs):
        rr = r_ref[rs, :].astype(F32)
        for h in range(GLA_H):
            hs = slice(GLA_V * h, GLA_V * (h + 1))
            rh = rr[:, hs]
            og_o[rs, hs] = (_rms(outs[h], g) * (rh / (1.0 + jnp.exp(-rh)))).astype(BF16)

    la = la_ref[...]
    if fast:
        la_hi = la.astype(BF16)
        la_lo = (la - la_hi.astype(F32)).astype(BF16)
        b = _dot(tri_ref[...], la_hi) + _dot(tri_ref[...], la_lo)
    else:
        rows = lax.broadcasted_iota(jnp.int32, (c_len, 1), 0)
        b = jnp.zeros_like(la)
        for j in range(c_len):
            b = b + jnp.where(rows >= j, la[j:j + 1, :], 0.0)
    b_ref[...] = b

    def robust():
        bexp = bexp_ref[...]

        def body(c, carry):
            rs = pl.ds(pl.multiple_of(c * c_len, c_len), c_len)
            outs, st_new = _gla_chunk(q_ref[rs, :].astype(F32), k_ref[rs, :].astype(F32),
                                      v_ref[rs, :].astype(F32), b_ref[rs, :], st_ref[...], bexp, c_len, sub)
            st_ref[...] = st_new
            emit(outs, rs)
            return carry

        lax.fori_loop(0, cps, body, 0)

    if fast:
        mild = jnp.min(b) >= -GLA_FAST_MAX

        @pl.when(mild)
        def _():
            outs, st_new = _gla_fast_step(q_ref[...].astype(F32), k_ref[...].astype(F32), v_ref[...].astype(F32),
                                          b_ref[...], st_ref[...], keep_ref[...], c_len, cps)
            st_ref[...] = st_new
            emit(outs, slice(None))

        pl.when(jnp.logical_not(mild))(robust)
    else:
        robust()

    @pl.when(t == pl.num_programs(1) - 1)
    def _():
        sf_o[...] = st_ref[...].T


def _gla(gq, gk, gv, gr, la, s0, w, batch, seq, c_len, sub, cps):
    ts = c_len * cps
    nt = seq // ts
    fast = c_len == GLA_CHUNK
    assert fast or cps == 1
    row = lambda width: pl.BlockSpec((None, ts, width), lambda b, t: (b * nt + t, 0, 0))
    g3 = lambda a: a.reshape(batch * nt, ts, a.shape[-1])
    st_spec = pl.BlockSpec((None, GLA_H * GLA_K, GLA_V), lambda b, t: (b, 0, 0))
    consts = [w['g_gla_o'], w['bexp']]
    if fast:
        ti = jnp.arange(ts)
        keep = (ti[:, None] // c_len == ti[None, :] // c_len) & (ti[None, :] <= ti[:, None])
        consts += [keep.astype(BF16), keep.astype(F32)]
    og, s_fin = pl.pallas_call(
        functools.partial(_gla_kernel, c_len=c_len, sub=sub, cps=cps, fast=fast),
        grid=(batch, nt),
        in_specs=[row(256), row(256), row(512), row(512), row(256), st_spec] + [_full(c.shape) for c in consts],
        out_specs=[row(512), st_spec],
        out_shape=[jax.ShapeDtypeStruct((batch * nt, ts, GLA_H * GLA_V), BF16),
                   jax.ShapeDtypeStruct((batch, GLA_H * GLA_K, GLA_V), F32)],
        scratch_shapes=[pltpu.VMEM((GLA_V, GLA_H * GLA_K), F32), pltpu.VMEM((ts, GLA_H * GLA_K), F32)],
        compiler_params=_cparams(("parallel", "arbitrary")),
        name="gla",
    )(g3(gq), g3(gk), g3(gv), g3(gr), g3(la), s0, *consts)
    return og.reshape(batch * seq, GLA_H * GLA_V), s_fin


def _mla_prompt_kernel(q_ref, k_ref, vt_ref, o_ref, m_ref, l_ref, acc_ref, *, tq):
    i = pl.program_id(1)
    m_ref[...] = jnp.full(m_ref.shape, -jnp.inf, F32)
    l_ref[...] = jnp.zeros(l_ref.shape, F32)
    acc_ref[...] = jnp.zeros(acc_ref.shape, F32)
    keep = (lax.broadcasted_iota(jnp.int32, (tq, tq), 0) <= lax.broadcasted_iota(jnp.int32, (tq, tq), 1))

    def tile(j, masked):
        rs = pl.ds(pl.multiple_of(j * tq, tq), tq)
        sts = [_dot_nt(k_ref[rs, LANES * h:LANES * (h + 1)], q_ref[:, LANES * h:LANES * (h + 1)])
               for h in range(MLA_H)]
        ps, alphas = [], []
        for h in range(MLA_H):
            st = jnp.where(keep, sts[h], -jnp.inf) if masked else sts[h]
            m_old = m_ref[h]
            m_new = jnp.maximum(m_old, jnp.max(st, axis=0, keepdims=True))
            alpha = jnp.exp2(m_old - m_new)
            p = jnp.exp2(st - m_new)
            m_ref[h] = m_new
            l_ref[h] = alpha * l_ref[h] + jnp.sum(p, axis=0, keepdims=True)
            ps.append(p.astype(BF16))
            alphas.append(alpha)
        for h in range(MLA_H):
            vt = vt_ref[j, LANES * (h // 2):LANES * (h // 2 + 1), :]
            acc_ref[h] = alphas[h] * acc_ref[h] + _dot(vt, ps[h])

    def body(j, carry):
        tile(j, False)
        return carry

    lax.fori_loop(0, i, body, 0)
    tile(i, True)

    row = lax.broadcasted_iota(jnp.int32, (LANES, 1), 0)
    for pair in range(MLA_H // 2):
        h0, h1 = 2 * pair, 2 * pair + 1
        a = acc_ref[h0] * (1.0 / l_ref[h0])
        b = acc_ref[h1] * (1.0 / l_ref[h1])
        o_ref[:, LANES * pair:LANES * (pair + 1)] = jnp.where(row < MLA_DV, a, b).T.astype(BF16)


def _mla_prompt(mq, mk, mvt, batch, seq, tq):
    nq = seq // tq
    return pl.pallas_call(
        functools.partial(_mla_prompt_kernel, tq=tq),
        grid=(batch, nq),
        in_specs=[pl.BlockSpec((tq, MLA_H * LANES), lambda b, i: (b * nq + i, 0)),
                  pl.BlockSpec((seq, MLA_H * LANES), lambda b, i: (b, 0)),
                  pl.BlockSpec((nq, MLA_H * MLA_DV, tq), lambda b, i: (b, 0, 0))],
        out_specs=pl.BlockSpec((tq, MLA_H * MLA_DV), lambda b, i: (b * nq + i, 0)),
        out_shape=jax.ShapeDtypeStruct((batch * seq, MLA_H * MLA_DV), BF16),
        scratch_shapes=[pltpu.VMEM((MLA_H, 1, tq), F32), pltpu.VMEM((MLA_H, 1, tq), F32),
                        pltpu.VMEM((MLA_H, LANES, tq), F32)],
        compiler_params=_cparams(("parallel", "arbitrary")),
        name="mla_prompt",
    )(mq, mk, mvt)


def _mla_sample_kernel(pt_ref, q_ref, wkq_ref, wkt_ref, wv_ref, cn_ref, kn_ref, *rest, pps, t_new, kpm):
    ckv_refs = rest[:pps]
    kpe_refs = rest[pps:2 * pps]
    o_ref, lq_ref, qp_ref, m_ref, l_ref, acc_ref = rest[2 * pps:]
    c = pl.program_id(1)
    nrow = MLA_H * t_new

    @pl.when(c == 0)
    def _():
        for h in range(MLA_H):
            qh = q_ref[:, LANES * h:LANES * (h + 1)]
            lq_ref[t_new * h:t_new * (h + 1), :] = _dot(qh, wkq_ref[h]).astype(BF16)
            qp_ref[t_new * h:t_new * (h + 1), :] = pltpu.roll(qh.astype(F32), LANES - _KPE_LANE, 1).astype(BF16)
        lq_ref[nrow:, :] = wkt_ref[...]
        m_ref[...] = jnp.full(m_ref.shape, -jnp.inf, F32)
        l_ref[...] = jnp.zeros(l_ref.shape, F32)
        acc_ref[...] = jnp.zeros(acc_ref.shape, F32)

    lq = lq_ref[...]
    qp = qp_ref[:, 0:MLA_ROPE]

    def scores(r, s_rope):
        rinv = []
        for h in range(MLA_H):
            kr = r[nrow + MLA_NOPE * h:nrow + MLA_NOPE * (h + 1)]
            ms = jnp.sum(kr * kr, axis=0, keepdims=True) * (1.0 / MLA_NOPE)
            rinv.append(jnp.broadcast_to(lax.rsqrt(ms + EPS), (t_new, r.shape[1])))
        return r[0:nrow] * jnp.concatenate(rinv, axis=0) + s_rope

    def update(s, x):
        m_old = m_ref[...]
        m_new = jnp.maximum(m_old, jnp.max(s, axis=-1, keepdims=True))
        alpha = jnp.exp2(m_old - m_new)
        p = jnp.exp2(s - m_new)
        m_ref[...] = m_new
        l_ref[...] = alpha * l_ref[...] + jnp.sum(p, axis=-1, keepdims=True)
        acc_ref[...] = alpha * acc_ref[...] + _dot(p.astype(BF16), x)

    ngrp = pps // kpm
    xs, ss = [], []
    for g0 in range(0, ngrp, SAMPLE_GROUPS):
        gs = range(g0, min(g0 + SAMPLE_GROUPS, ngrp))
        xg = [jnp.concatenate([ckv_refs[g * kpm + j][...] for j in range(kpm)], axis=0).astype(BF16) for g in gs]
        rg = [_dot_nt(lq, x) for x in xg]
        pg = [_dot(qp, jnp.concatenate([kpe_refs[g * kpm + j][...] for j in range(kpm)], axis=1).astype(BF16))
              for g in gs]
        xs += xg
        ss += [scores(r, s_rope) for r, s_rope in zip(rg, pg)]
    update(jnp.concatenate(ss, axis=1), jnp.concatenate(xs, axis=0))

    @pl.when(c == pl.num_programs(1) - 1)
    def _():
        x = cn_ref[...].astype(BF16)
        s = scores(_dot_nt(lq, x), _dot_nt(qp, kn_ref[...].astype(BF16)))
        tq = lax.broadcasted_iota(jnp.int32, (nrow, t_new), 0) % t_new
        ts = lax.broadcasted_iota(jnp.int32, (nrow, t_new), 1)
        update(jnp.where(ts <= tq, s, -jnp.inf), x)
        oa = (acc_ref[...] / l_ref[...]).astype(BF16)
        r = _dot(oa, wv_ref[...])
        lane = lax.broadcasted_iota(jnp.int32, (1, MLA_H * MLA_DV), 1)
        out = jnp.zeros((t_new, MLA_H * MLA_DV), F32)
        for h in range(MLA_H):
            out = out + jnp.where(lane // MLA_DV == h, r[t_new * h:t_new * (h + 1)], 0.0)
        o_ref[...] = out.astype(BF16)


def _mla_sample(mq, ckv_new, kp_new, pool_ckv, pool_kpe_t, page_table, w, batch, t_new, pps, kpm):
    n_pages = page_table.shape[1]
    nch = n_pages // pps
    pt = page_table.reshape(-1)
    nrow = MLA_H * t_new

    def page_spec(shape, i):
        return pl.BlockSpec((None,) + shape, lambda b, c, pt_ref: (pt_ref[b * n_pages + c * pps + i], 0, 0))

    tok = lambda width: pl.BlockSpec((None, t_new, width), lambda b, c, pt_ref: (b, 0, 0))
    g3 = lambda a: a.reshape(batch, t_new, a.shape[-1])
    const = lambda a: pl.BlockSpec(a.shape, lambda b, c, pt_ref: (0,) * a.ndim)
    grid_spec = pltpu.PrefetchScalarGridSpec(
        num_scalar_prefetch=1,
        grid=(batch, nch),
        in_specs=[tok(MLA_H * LANES), const(w['w_kq']), const(w['w_kt']), const(w['w_v']),
                  tok(MLA_KVR), tok(MLA_ROPE)]
                 + [page_spec((PAGE, MLA_KVR), i) for i in range(pps)]
                 + [page_spec((MLA_ROPE, PAGE), i) for i in range(pps)],
        out_specs=tok(MLA_H * MLA_DV),
        scratch_shapes=[pltpu.VMEM((nrow + MLA_H * MLA_NOPE, MLA_KVR), BF16),
                        pltpu.VMEM((nrow, LANES), BF16),
                        pltpu.VMEM((nrow, 1), F32), pltpu.VMEM((nrow, 1), F32),
                        pltpu.VMEM((nrow, MLA_KVR), F32)],
    )
    om = pl.pallas_call(
        functools.partial(_mla_sample_kernel, pps=pps, t_new=t_new, kpm=kpm),
        grid_spec=grid_spec,
        out_shape=jax.ShapeDtypeStruct((batch, t_new, MLA_H * MLA_DV), BF16),
        compiler_params=_cparams(("parallel", "arbitrary")),
        name="mla_sample",
    )(pt, g3(mq), w['w_kq'], w['w_kt'], w['w_v'], g3(ckv_new), g3(kp_new),
      *([pool_ckv] * pps), *([pool_kpe_t] * pps))
    return om.reshape(batch * t_new, MLA_H * MLA_DV)


def _mix_out_kernel(x_ref, og_ref, om_ref, wo_ref, gx_ref, wxq_ref, gxq_ref, x1_o, xq_o):
    x1 = x_ref[...] + _dot(og_ref[...], wo_ref[0:512, :]) + _dot(om_ref[...], wo_ref[512:1024, :])
    x1_o[...] = x1
    qf = _dot(_rms(x1, gx_ref[...]).astype(BF16), wxq_ref[...])
    g = gxq_ref[...]
    for h in range(X_H):
        hs = slice(X_D * h, X_D * (h + 1))
        xq_o[:, hs] = _rms(qf[:, hs], g).astype(BF16)


def _mix_out(x, og, om, w, tm):
    n = x.shape[0]
    row = lambda width: pl.BlockSpec((tm, width), lambda i: (i, 0))
    consts = [w['w_out'], w['g_x'], w['w_xq'], w['g_xq']]
    return pl.pallas_call(
        _mix_out_kernel,
        grid=(n // tm,),
        in_specs=[row(D_MODEL), row(512), row(512)] + [_full(c.shape) for c in consts],
        out_specs=[row(D_MODEL), row(X_H * X_D)],
        out_shape=[jax.ShapeDtypeStruct((n, D_MODEL), F32), jax.ShapeDtypeStruct((n, X_H * X_D), BF16)],
        compiler_params=_cparams(("parallel",)),
        name="mix_out",
    )(x, og, om, *consts)


def _cross_kernel(xq_ref, mk_ref, mv_ref, o_ref, *, n_mem):
    gb = xq_ref.shape[0]

    def head(ref, g, h):
        if len(ref.shape) == 4:
            return ref[g, h]
        return ref[pl.ds(g * n_mem * X_H + h, n_mem, stride=X_H), :].astype(BF16)

    units = [(g, h) for g in range(gb) for h in range(X_H)]
    ss = [_dot_nt(xq_ref[g, :, X_D * h:X_D * (h + 1)], head(mk_ref, g, h)) for g, h in units]
    ps = [jnp.exp2(s - jnp.max(s, axis=-1, keepdims=True)) for s in ss]
    for (g, h), p in zip(units, ps):
        o = _dot(p.astype(BF16), head(mv_ref, g, h))
        o_ref[g, :, X_D * h:X_D * (h + 1)] = (o * (1.0 / jnp.sum(p, axis=-1, keepdims=True))).astype(BF16)


def _cross(xq, mk, mv, batch, seq, n_mem, tq, gb):
    nq = seq // tq
    assert gb == 1 or nq == 1
    row = pl.BlockSpec((gb, tq, X_H * X_D), lambda b, i: (b * nq + i, 0, 0))
    if mk.ndim == 4:
        mem = pl.BlockSpec((gb,) + mk.shape[1:], lambda b, i: (b, 0, 0, 0))
    else:
        mem = pl.BlockSpec((gb * n_mem * X_H, X_D), lambda b, i: (b, 0))
    o = pl.pallas_call(
        functools.partial(_cross_kernel, n_mem=n_mem),
        grid=(batch // gb, nq),
        in_specs=[row, mem, mem],
        out_specs=row,
        out_shape=jax.ShapeDtypeStruct((batch * nq, tq, X_H * X_D), BF16),
        compiler_params=_cparams(("parallel", "arbitrary")),
        name="cross",
    )(xq.reshape(batch * nq, tq, X_H * X_D), mk, mv)
    return o.reshape(batch * seq, X_H * X_D)


def _ffn_kernel(x_ref, o_ref, wxo_ref, g_ref, w1_ref, w2_ref, y_o, *, fc):
    x = x_ref[...] + _dot(o_ref[...], wxo_ref[...])
    h = _rms(x, g_ref[...]).astype(BF16)
    acc = x
    for c in range(D_FF // fc):
        u = jnp.maximum(_dot(h, w1_ref[:, fc * c:fc * (c + 1)]), 0.0)
        acc = acc + _dot((u * u).astype(BF16), w2_ref[fc * c:fc * (c + 1), :])
    y_o[...] = acc


def _ffn(x, o, w, tm, fc):
    n = x.shape[0]
    row = lambda width: pl.BlockSpec((tm, width), lambda i: (i, 0))
    consts = [w['w_xo'], w['g_ff'], w['w_ff1'], w['w_ff2']]
    return pl.pallas_call(
        functools.partial(_ffn_kernel, fc=fc),
        grid=(n // tm,),
        in_specs=[row(D_MODEL), row(X_H * X_D)] + [_full(c.shape) for c in consts],
        out_specs=row(D_MODEL),
        out_shape=jax.ShapeDtypeStruct((n, D_MODEL), F32),
        compiler_params=_cparams(("parallel",)),
        name="ffn",
    )(x, o, *consts)


def _mem_kv_kernel(m_ref, gm_ref, wk_ref, wv_ref, gk_ref, k_o, v_o, kh_o, vh_o, *, n_mem):
    m = _rms(m_ref[...], gm_ref[...]).astype(BF16)
    kf = _dot(m, wk_ref[...])
    vf = _dot(m, wv_ref[...])
    v_o[...] = vf
    g = gk_ref[...]
    for h in range(X_H):
        hs = slice(X_D * h, X_D * (h + 1))
        kn = _rms(kf[:, hs], g)
        k_o[:, hs] = kn
        for b in range(kh_o.shape[0]):
            rs = slice(n_mem * b, n_mem * (b + 1))
            kh_o[b, h] = kn[rs].astype(BF16)
            vh_o[b, h] = vf[rs, hs].astype(BF16)


def _mem_kv(mem, w, n_mem, tm):
    n = mem.shape[0]
    bt = tm // n_mem
    row = lambda width: pl.BlockSpec((tm, width), lambda i: (i, 0))
    head = pl.BlockSpec((bt, X_H, n_mem, X_D), lambda i: (i, 0, 0, 0))
    consts = [w['g_mem'], w['w_xk'], w['w_xv'], w['g_xk']]
    return pl.pallas_call(
        functools.partial(_mem_kv_kernel, n_mem=n_mem),
        grid=(n // tm,),
        in_specs=[row(D_MODEL)] + [_full(c.shape) for c in consts],
        out_specs=[row(X_H * X_D), row(X_H * X_D), head, head],
        out_shape=[jax.ShapeDtypeStruct((n, X_H * X_D), F32)] * 2
                  + [jax.ShapeDtypeStruct((n // n_mem, X_H, n_mem, X_D), BF16)] * 2,
        compiler_params=_cparams(("parallel",)),
        name="mem_kv",
    )(mem, *consts)


def _prep_weights(g_mix, w_in, w_gla_a2, b_gla_a, g_gla_o, g_mla_qa, w_mla_qb, g_mla_kva, w_mla_kvb,
                  g_q_nope, g_k_nope, g_q_rope, g_k_rope, w_out, g_x, g_mem, w_xq, w_xk, w_xv, g_xq, g_xk,
                  w_xo, g_ff, w_ff1, w_ff2):
    rowv = lambda g: g.reshape(1, -1).astype(F32)
    zc = lambda n: jnp.zeros((D_MODEL, n), F32)
    sizes = (256, 256, 512, 512, GLA_RANK, MLA_QR, MLA_KVR, MLA_ROPE)
    offs = [0]
    for s in sizes:
        offs.append(offs[-1] + s)
    q, k, v, r, a, qa, kva, kpe = [w_in[:, offs[i]:offs[i + 1]] for i in range(8)]
    w_in_p = jnp.concatenate(
        [q * (GLA_K ** -0.5), k, v, r, qa, kva, a, zc(LANES - GLA_RANK),
         zc(_KPE_LANE), kpe, zc(LANES - _KPE_LANE - MLA_ROPE)], axis=1)
    w_a2 = jnp.concatenate([w_gla_a2, jnp.zeros((LANES - GLA_RANK, GLA_H * GLA_K), F32)], axis=0)

    qb = w_mla_qb.reshape(MLA_QR, MLA_H, MLA_NOPE + MLA_ROPE)
    qb = jnp.pad(qb, ((0, 0), (0, 0), (0, LANES - MLA_NOPE - MLA_ROPE))).reshape(MLA_QR, MLA_H * LANES)
    kvb = w_mla_kvb.reshape(MLA_KVR, MLA_H, MLA_NOPE + MLA_DV)
    wk = kvb[:, :, :MLA_NOPE]
    wk_p = jnp.pad(wk, ((0, 0), (0, 0), (0, LANES - MLA_NOPE))).reshape(MLA_KVR, MLA_H * LANES)
    wv = kvb[:, :, MLA_NOPE:].reshape(MLA_KVR, MLA_H * MLA_DV)
    z32 = jnp.zeros((LANES - MLA_NOPE - MLA_ROPE,), F32)
    gq_row = jnp.tile(jnp.concatenate([g_q_nope, g_q_rope, z32]), MLA_H) * (MLA_SCALE * LOG2E)
    gk_row = jnp.tile(jnp.concatenate([g_k_nope, jnp.zeros((LANES - MLA_NOPE,), F32)]), MLA_H)
    g_kpe = jnp.concatenate([jnp.zeros((_KPE_LANE,), F32), g_k_rope, z32])

    li = jnp.arange(2 * LANES)
    seg = jnp.where(li % LANES < MLA_NOPE, 0, jnp.where(li % LANES < MLA_NOPE + MLA_ROPE, 1, 2))
    same = (li[:, None] // LANES == li[None, :] // LANES) & (seg[:, None] == seg[None, :]) & (seg[:, None] < 2)
    bd = jnp.where(same, jnp.where(seg[:, None] == 0, 1.0 / MLA_NOPE, 1.0 / MLA_ROPE), 0.0)

    wkq = jnp.transpose(wk, (1, 2, 0)) * g_k_nope[None, :, None]
    wkq = jnp.pad(wkq, ((0, 0), (0, LANES - MLA_NOPE), (0, 0)))
    wkt = jnp.transpose(wk, (1, 2, 0)).reshape(MLA_H * MLA_NOPE, MLA_KVR)

    hk = jnp.arange(GLA_H * GLA_K) // GLA_K
    hv = jnp.arange(GLA_H * GLA_V) // GLA_V
    bexp = (hk[:, None] == hv[None, :])
    return {
        'g_mix': rowv(g_mix), 'w_in': w_in_p.astype(BF16), 'w_a2': w_a2.astype(BF16), 'b_a': rowv(b_gla_a),
        'g_qa': rowv(g_mla_qa), 'w_qb': qb.astype(BF16), 'gq_row': rowv(gq_row), 'g_kva': rowv(g_mla_kva),
        'w_k': wk_p.astype(BF16), 'gk_row': rowv(gk_row), 'w_v': wv.astype(BF16), 'g_kpe': rowv(g_kpe),
        'bd': bd.astype(BF16), 'w_kq': wkq.astype(BF16), 'w_kt': wkt.astype(BF16),
        'g_gla_o': rowv(g_gla_o), 'bexp': bexp.astype(BF16),
        'w_out': w_out.astype(BF16), 'g_x': rowv(g_x), 'w_xq': w_xq.astype(BF16),
        'g_xq': rowv(g_xq) * (X_SCALE * LOG2E),
        'g_mem': rowv(g_mem), 'w_xk': w_xk.astype(BF16), 'w_xv': w_xv.astype(BF16), 'g_xk': rowv(g_xk),
        'w_xo': w_xo.astype(BF16), 'g_ff': rowv(g_ff), 'w_ff1': w_ff1.astype(BF16), 'w_ff2': w_ff2.astype(BF16),
    }


def _rope_tables(pos):
    half = MLA_ROPE // 2
    inv = ROPE_THETA ** (-jnp.arange(half, dtype=F32) / half)
    ang = pos.astype(F32)[:, None] * inv[None, :]
    cos, sin = jnp.cos(ang), jnp.sin(ang)
    n = pos.shape[0]
    one = jnp.ones((n, _KPE_LANE), F32)
    z = lambda w_: jnp.zeros((n, w_), F32)
    tail = LANES - _KPE_LANE - MLA_ROPE
    c = jnp.concatenate([one, cos, cos, jnp.ones((n, tail), F32)], axis=1)
    s_up = jnp.concatenate([z(_KPE_LANE), -sin, z(half), z(tail)], axis=1)
    s_dn = jnp.concatenate([z(_KPE_LANE), z(half), sin, z(tail)], axis=1)
    return c, s_up, s_dn


def _tile_rows(n, cap):
    t = min(n, cap)
    while n % t:
        t //= 2
    return t


def _tail(x1, xq, mk, mv, w, batch, seq, n_mem):
    tq = _tile_rows(seq, 512)
    gb = _tile_rows(batch, max(1, CROSS_ROWS // seq)) if tq == seq else 1
    o = _cross(xq, mk, mv, batch, seq, n_mem, tq, gb)
    return _ffn(x1, o, w, _tile_rows(x1.shape[0], 512), 1024)


def _prompt_layer(x, mem, w):
    batch, seq, _ = x.shape
    n = batch * seq
    xf = x.reshape(n, D_MODEL)
    tm = _tile_rows(seq, 512)
    tabs = _rope_tables(jnp.arange(seq))
    gq, gk, gv, gr, la, mq, ckv, kp, mk_, mvt = _in_proj(xf, w, tabs, tm, True)
    c_len = GLA_CHUNK
    cps = _tile_rows(seq // c_len, 4)
    s0 = jnp.zeros((batch, GLA_H * GLA_K, GLA_V), F32)
    og, s_fin = _gla(gq, gk, gv, gr, la, s0, w, batch, seq, c_len, min(GLA_SUB, c_len), cps)
    om = _mla_prompt(mq, mk_, mvt, batch, seq, ATT_TILE)
    x1, xq = _mix_out(xf, og, om, w, tm)
    n_mem = mem.shape[1]
    memf = mem.reshape(-1, D_MODEL)
    xk, xv, xkh, xvh = _mem_kv(memf, w, n_mem, max(n_mem, _tile_rows(memf.shape[0], 512)))
    y = _tail(x1, xq, xkh, xvh, w, batch, seq, n_mem)
    return y, ckv, kp, xk, xv, s_fin


def _sample_layer(x, pool_ckv, pool_kpe, page_table, mem_k, mem_v, s_prev, w):
    batch, seq, _ = x.shape
    n = batch * seq
    xf = x.reshape(n, D_MODEL)
    tm = _tile_rows(n, 512)
    past = page_table.shape[1] * PAGE
    pos = past + (jnp.arange(tm) % seq)
    gq, gk, gv, gr, la, mq, ckv, kp = _in_proj(xf, w, _rope_tables(pos), tm, False)
    s0 = s_prev.reshape(batch, GLA_H * GLA_K, GLA_V)
    og, s_fin = _gla(gq, gk, gv, gr, la, s0, w, batch, seq, seq, seq, 1)
    n_pages = page_table.shape[1]
    pps = _tile_rows(n_pages, SAMPLE_PAGES)
    om = _mla_sample(mq, ckv, kp, pool_ckv, jnp.swapaxes(pool_kpe, 1, 2), page_table, w, batch, seq,
                     pps, min(pps, SAMPLE_PAGES_PER_DOT))
    x1, xq = _mix_out(xf, og, om, w, tm)
    y = _tail(x1, xq, mem_k.reshape(-1, X_D), mem_v.reshape(-1, X_D), w, batch, seq, mem_k.shape[1])
    return y, ckv, kp, s_fin


def kernel(x_prompt, x_sample, mem_prompt, cache_ckv, cache_kpe, cache_mem_k, cache_mem_v, state_gla, page_table, g_mix, w_in, w_gla_a2, b_gla_a, g_gla_o, g_mla_qa, w_mla_qb, g_mla_kva, w_mla_kvb, g_q_nope, g_k_nope, g_q_rope, g_k_rope, w_out, g_x, g_mem, w_xq, w_xk, w_xv, g_xq, g_xk, w_xo, g_ff, w_ff1, w_ff2):
    depth = w_in.shape[0]
    assert depth == 1, "one layer: prompt-group caches of layer l would feed layer l+1 otherwise unchanged"
    params = (g_mix, w_in, w_gla_a2, b_gla_a, g_gla_o, g_mla_qa, w_mla_qb, g_mla_kva, w_mla_kvb,
              g_q_nope, g_k_nope, g_q_rope, g_k_rope, w_out, g_x, g_mem, w_xq, w_xk, w_xv, g_xq, g_xk,
              w_xo, g_ff, w_ff1, w_ff2)
    w = _prep_weights(*[p[0] for p in params])
    bp, tp, _ = x_prompt.shape
    bs, tsq, _ = x_sample.shape
    yp, ckv_p, kp_p, xk, xv, gla_p = _prompt_layer(x_prompt, mem_prompt, w)
    ys, ckv_s, kp_s, gla_s = _sample_layer(x_sample, cache_ckv.reshape(cache_ckv.shape[1:]), cache_kpe.reshape(cache_kpe.shape[1:]), page_table,
                                            cache_mem_k[0], cache_mem_v[0], state_gla[0], w)
    n_mem = mem_prompt.shape[1]
    return (yp.reshape(bp, tp, D_MODEL), ys.reshape(bs, tsq, D_MODEL),
            ckv_p.reshape(1, bp, tp, MLA_KVR), kp_p.reshape(1, bp, tp, MLA_ROPE),
            xk.reshape(1, bp, n_mem, X_H, X_D), xv.reshape(1, bp, n_mem, X_H, X_D),
            gla_p.reshape(1, bp, GLA_H, GLA_K, GLA_V),
            ckv_s.reshape(1, bs, tsq, MLA_KVR), kp_s.reshape(1, bs, tsq, MLA_ROPE),
            gla_s.reshape(1, bs, GLA_H, GLA_K, GLA_V))
```

```python
import functools

import jax
import jax.numpy as jnp
from jax import lax
from jax.experimental import pallas as pl
from jax.experimental.pallas import tpu as pltpu

F32 = jnp.float32
BF16 = jnp.bfloat16

EPS = 1e-6
D_MODEL = 1024
GLA_H, GLA_K, GLA_V, GLA_RANK, GLA_TAU, GLA_CHUNK = 4, 64, 128, 16, 16.0, 64
GLA_SUB = 16
GLA_FAST_MAX = 40.0
MLA_H, MLA_DV, MLA_NOPE, MLA_ROPE, MLA_QR, MLA_KVR = 8, 64, 64, 32, 384, 256
MLA_SCALE = (MLA_NOPE + MLA_ROPE) ** -0.5
ROPE_THETA = 10000.0
X_H, X_D = 4, 128
X_SCALE = X_D ** -0.5
D_FF = 4096
PAGE = 128
LANES = 128
NEG = -1e30
LOG2E = 1.4426950408889634
ATT_TILE = 256
ATT_GROUP = 2
CROSS_ROWS = 32
SAMPLE_PAGES_PER_DOT = 4
SAMPLE_PV_CHAINS = 2
SAMPLE_GROUPS = 4

_C_Q, _C_K, _C_V, _C_R = 0, 256, 512, 1024
_C_QA, _C_KVA, _C_A, _C_KPE, _C_END = 1536, 1920, 2176, 2304, 2432
_KPE_LANE = 64

VMEM_LIMIT = 56 * 1024 * 1024


def _cparams(sem):
    return pltpu.CompilerParams(dimension_semantics=sem, vmem_limit_bytes=VMEM_LIMIT)


def _dot(a, b):
    return jnp.dot(a, b, preferred_element_type=F32)


def _dot_nt(a, b):
    return lax.dot_general(a, b, (((1,), (1,)), ((), ())), preferred_element_type=F32)


def _rms(x, g):
    return x * lax.rsqrt(jnp.mean(x * x, axis=-1, keepdims=True) + EPS) * g


def _full(shape):
    n = len(shape)
    return pl.BlockSpec(shape, lambda *_: (0,) * n)


def _in_proj_kernel(x_ref, gmix_ref, win_ref, wa2_ref, ba_ref, gqa_ref, wqb_ref, gqrow_ref,
                    gkva_ref, wk_ref, gkrow_ref, wv_ref, gkpe_ref, bd_ref, c_ref, s1_ref, s2_ref,
                    gq_o, gk_o, gv_o, gr_o, la_o, mq_o, ckv_o, kp_o, *kv_o):
    h = _rms(x_ref[...], gmix_ref[...]).astype(BF16)

    def proj(lo, hi):
        return _dot(h, win_ref[:, lo:hi])

    gq_o[...] = proj(_C_Q, _C_K).astype(BF16)
    gk_o[...] = proj(_C_K, _C_V).astype(BF16)
    gv_o[...] = proj(_C_V, _C_R).astype(BF16)
    gr_o[...] = proj(_C_R, _C_QA).astype(BF16)

    gate = _dot(proj(_C_A, _C_KPE).astype(BF16), wa2_ref[...]) + ba_ref[...]
    la_o[...] = (jnp.minimum(gate, 0.0) - jnp.log1p(jnp.exp(-jnp.abs(gate)))) * (1.0 / GLA_TAU)

    cos, s_up, s_dn = c_ref[...], s1_ref[...], s2_ref[...]

    def rope(t):
        return t * cos + pltpu.roll(t, LANES - 16, 1) * s_up + pltpu.roll(t, 16, 1) * s_dn

    zk = proj(_C_KPE, _C_END)
    kpn = zk * lax.rsqrt(jnp.sum(zk * zk, axis=-1, keepdims=True) * (1.0 / MLA_ROPE) + EPS) * gkpe_ref[...]
    kpt = rope(kpn)
    kp_o[...] = pltpu.roll(kpt, LANES - _KPE_LANE, 1)[:, 0:MLA_ROPE]

    ckv = _rms(proj(_C_KVA, _C_A), gkva_ref[...])
    ckv_o[...] = ckv
    cb = ckv.astype(BF16)
    bd = bd_ref[...]

    def seg_norm(t, grow):
        ms = _dot((t * t).astype(BF16), bd)
        return t * lax.rsqrt(ms + EPS) * grow

    qf = _dot(_rms(proj(_C_QA, _C_KVA), gqa_ref[...]).astype(BF16), wqb_ref[...])
    for c in range(4):
        sl = slice(2 * LANES * c, 2 * LANES * (c + 1))
        qn = seg_norm(qf[:, sl], gqrow_ref[:, sl])
        for j in range(2):
            o = 2 * LANES * c + LANES * j
            mq_o[:, o:o + LANES] = rope(qn[:, LANES * j:LANES * (j + 1)]).astype(BF16)

    if kv_o:
        mk_o, mvt_o = kv_o
        mv = _dot(cb, wv_ref[...])
        for c in range(mvt_o.shape[0]):
            mvt_o[c] = mv[ATT_TILE * c:ATT_TILE * (c + 1), :].T.astype(BF16)
        kf = _dot(cb, wk_ref[...])
        for c in range(4):
            sl = slice(2 * LANES * c, 2 * LANES * (c + 1))
            kn = seg_norm(kf[:, sl], gkrow_ref[:, sl])
            for j in range(2):
                o = 2 * LANES * c + LANES * j
                mk_o[:, o:o + LANES] = (kn[:, LANES * j:LANES * (j + 1)] + kpt).astype(BF16)


def _in_proj(x, w, tabs, tm, with_kv):
    n = x.shape[0]
    nt = tabs[0].shape[0] // tm
    row = lambda width: pl.BlockSpec((tm, width), lambda i: (i, 0))
    tab = pl.BlockSpec((tm, LANES), lambda i: (i % nt, 0))
    consts = [w['g_mix'], w['w_in'], w['w_a2'], w['b_a'], w['g_qa'], w['w_qb'], w['gq_row'],
              w['g_kva'], w['w_k'], w['gk_row'], w['w_v'], w['g_kpe'], w['bd']]
    out_w = [(256, BF16), (256, BF16), (512, BF16), (512, BF16), (256, F32),
             (1024, BF16), (256, F32), (MLA_ROPE, F32)] + ([(1024, BF16)] if with_kv else [])
    out_specs = [row(wd) for wd, _ in out_w]
    out_shape = [jax.ShapeDtypeStruct((n, wd), dt) for wd, dt in out_w]
    if with_kv:
        out_specs.append(pl.BlockSpec((tm // ATT_TILE, MLA_H * MLA_DV, ATT_TILE), lambda i: (i, 0, 0)))
        out_shape.append(jax.ShapeDtypeStruct((n // ATT_TILE, MLA_H * MLA_DV, ATT_TILE), BF16))
    return pl.pallas_call(
        _in_proj_kernel,
        grid=(n // tm,),
        in_specs=[row(D_MODEL)] + [_full(c.shape) for c in consts] + [tab, tab, tab],
        out_specs=out_specs,
        out_shape=out_shape,
        compiler_params=_cparams(("parallel",)),
        name="in_proj",
    )(x, *consts, *tabs)


def _head_masks():
    lane = lax.broadcasted_iota(jnp.int32, (1, GLA_H * GLA_K), 1)
    return [(lane // GLA_K == h).astype(F32) for h in range(GLA_H)]


def _gla_fast_step(q, k, v, b, st, keep, c_len, cps):
    masks = _head_masks()
    ts = c_len * cps
    eb = jnp.exp(b)
    qt = q * eb
    kt = k * jnp.exp(-b)
    lq = jnp.concatenate([qt * m for m in masks], axis=0).astype(BF16)
    a = _dot_nt(lq, kt.astype(BF16))
    vb = v.astype(BF16)
    o_intra = [_dot((a[ts * h:ts * (h + 1)] * keep).astype(BF16), vb[:, GLA_V * h:GLA_V * (h + 1)])
               for h in range(GLA_H)]
    o_inter = []
    for c in range(cps):
        cs = slice(c_len * c, c_len * (c + 1))
        lqc = jnp.concatenate([lq[ts * h + c_len * c:ts * h + c_len * (c + 1)] for h in range(GLA_H)], axis=0)
        o_inter.append(_dot_nt(lqc, st.astype(BF16)))
        ebl = eb[c_len * (c + 1) - 1:c_len * (c + 1), :]
        kl = kt[cs] * ebl
        vs = jnp.concatenate([v[cs, GLA_V * h:GLA_V * (h + 1)] for h in range(GLA_H)], axis=0)
        ks = jnp.concatenate([kl * m for m in masks], axis=0).astype(BF16)
        st = st * ebl + _dot(vs.T.astype(BF16), ks)
    outs = [o_intra[h] + jnp.concatenate([o_inter[c][c_len * h:c_len * (h + 1)] for c in range(cps)], axis=0)
            for h in range(GLA_H)]
    return outs, st


def _gla_chunk(q, k, v, b, st, bexp, c_len, sub):
    masks = _head_masks()
    bl = b[c_len - 1:c_len, :]

    qh = q * jnp.exp(b)
    lq = jnp.concatenate([qh * m for m in masks], axis=0).astype(BF16)
    o_inter = _dot_nt(lq, st.astype(BF16))

    nsub = c_len // sub
    a_rows = [[] for _ in range(GLA_H)]
    col = lax.broadcasted_iota(jnp.int32, (1, c_len), 1)
    for i in range(1, nsub):
        ref = b[sub * i - 1:sub * i, :]
        qi = q[sub * i:sub * (i + 1)] * jnp.exp(b[sub * i:sub * (i + 1)] - ref)
        ki = k * jnp.exp(jnp.minimum(ref - b, 0.0))
        li = jnp.concatenate([qi * m for m in masks], axis=0).astype(BF16)
        ai = jnp.where(col < sub * i, _dot_nt(li, ki.astype(BF16)), 0.0)
        for h in range(GLA_H):
            a_rows[h].append(ai[sub * h:sub * (h + 1)])

    row = lax.broadcasted_iota(jnp.int32, (sub, 1), 0)
    o_diag = []
    for i in range(nsub):
        sl = slice(sub * i, sub * (i + 1))
        bb, qb, kb, vb = b[sl], q[sl], k[sl], v[sl]
        ps = []
        for s in range(sub):
            e = jnp.exp(jnp.where(row >= s, bb - bb[s:s + 1, :], NEG))
            ps.append(qb * e * kb[s:s + 1, :])
        r = _dot(jnp.concatenate(ps, axis=0).astype(BF16), bexp)
        od = r[0:sub] * vb[0:1, :]
        for s in range(1, sub):
            od = od + r[sub * s:sub * (s + 1)] * vb[s:s + 1, :]
        o_diag.append(od)
    o_diag = jnp.concatenate(o_diag, axis=0) if nsub > 1 else o_diag[0]

    outs = []
    for h in range(GLA_H):
        oh = o_inter[c_len * h:c_len * (h + 1)] + o_diag[:, GLA_V * h:GLA_V * (h + 1)]
        if nsub > 1:
            ah = jnp.concatenate([jnp.zeros((sub, c_len), F32)] + a_rows[h], axis=0)
            oh = oh + _dot(ah.astype(BF16), v[:, GLA_V * h:GLA_V * (h + 1)].astype(BF16))
        outs.append(oh)

    kl = k * jnp.exp(bl - b)
    vs = jnp.concatenate([v[:, GLA_V * h:GLA_V * (h + 1)] for h in range(GLA_H)], axis=0)
    ks = jnp.concatenate([kl * m for m in masks], axis=0).astype(BF16)
    st_new = st * jnp.exp(bl) + _dot(vs.T.astype(BF16), ks)
    return outs, st_new


def _gla_kernel(*refs, c_len, sub, cps, fast):
    if fast:
        (q_ref, k_ref, v_ref, r_ref, la_ref, s0_ref, g_ref, bexp_ref, tri_ref, keep_ref,
         og_o, sf_o, st_ref, b_ref) = refs
    else:
        q_ref, k_ref, v_ref, r_ref, la_ref, s0_ref, g_ref, bexp_ref, og_o, sf_o, st_ref, b_ref = refs
    t = pl.program_id(1)
    ts = c_len * cps

    @pl.when(t == 0)
    def _():
        st_ref[...] = s0_ref[...].T

    g = g_ref[...]

    def emit(outs, rs):
        rr = r_ref[rs, :].astype(F32)
        for h in range(GLA_H):
            hs = slice(GLA_V * h, GLA_V * (h + 1))
            rh = rr[:, hs]
            og_o[rs, hs] = (_rms(outs[h], g) * (rh / (1.0 + jnp.exp(-rh)))).astype(BF16)

    la = la_ref[...]
    if fast:
        la_hi = la.astype(BF16)
        la_lo = (la - la_hi.astype(F32)).astype(BF16)
        b = _dot(tri_ref[...], la_hi) + _dot(tri_ref[...], la_lo)
    else:
        rows = lax.broadcasted_iota(jnp.int32, (c_len, 1), 0)
        b = jnp.zeros_like(la)
        for j in range(c_len):
            b = b + jnp.where(rows >= j, la[j:j + 1, :], 0.0)
    b_ref[...] = b

    def robust():
        bexp = bexp_ref[...]

        def body(c, carry):
            rs = pl.ds(pl.multiple_of(c * c_len, c_len), c_len)
            outs, st_new = _gla_chunk(q_ref[rs, :].astype(F32), k_ref[rs, :].astype(F32),
                                      v_ref[rs, :].astype(F32), b_ref[rs, :], st_ref[...], bexp, c_len, sub)
            st_ref[...] = st_new
            emit(outs, rs)
            return carry

        lax.fori_loop(0, cps, body, 0)

    if fast:
        mild = jnp.min(b) >= -GLA_FAST_MAX

        @pl.when(mild)
        def _():
            outs, st_new = _gla_fast_step(q_ref[...].astype(F32), k_ref[...].astype(F32), v_ref[...].astype(F32),
                                          b_ref[...], st_ref[...], keep_ref[...], c_len, cps)
            st_ref[...] = st_new
            emit(outs, slice(None))

        pl.when(jnp.logical_not(mild))(robust)
    else:
        robust()

    @pl.when(t == pl.num_programs(1) - 1)
    def _():
        sf_o[...] = st_ref[...].T


def _gla(gq, gk, gv, gr, la, s0, w, batch, seq, c_len, sub, cps):
    ts = c_len * cps
    nt = seq // ts
    fast = c_len == GLA_CHUNK
    assert fast or cps == 1
    row = lambda width: pl.BlockSpec((None, ts, width), lambda b, t: (b * nt + t, 0, 0))
    g3 = lambda a: a.reshape(batch * nt, ts, a.shape[-1])
    st_spec = pl.BlockSpec((None, GLA_H * GLA_K, GLA_V), lambda b, t: (b, 0, 0))
    consts = [w['g_gla_o'], w['bexp']]
    if fast:
        ti = jnp.arange(ts)
        keep = (ti[:, None] // c_len == ti[None, :] // c_len) & (ti[None, :] <= ti[:, None])
        consts += [keep.astype(BF16), keep.astype(F32)]
    og, s_fin = pl.pallas_call(
        functools.partial(_gla_kernel, c_len=c_len, sub=sub, cps=cps, fast=fast),
        grid=(batch, nt),
        in_specs=[row(256), row(256), row(512), row(512), row(256), st_spec] + [_full(c.shape) for c in consts],
        out_specs=[row(512), st_spec],
        out_shape=[jax.ShapeDtypeStruct((batch * nt, ts, GLA_H * GLA_V), BF16),
                   jax.ShapeDtypeStruct((batch, GLA_H * GLA_K, GLA_V), F32)],
        scratch_shapes=[pltpu.VMEM((GLA_V, GLA_H * GLA_K), F32), pltpu.VMEM((ts, GLA_H * GLA_K), F32)],
        compiler_params=_cparams(("parallel", "arbitrary")),
        name="gla",
    )(g3(gq), g3(gk), g3(gv), g3(gr), g3(la), s0, *consts)
    return og.reshape(batch * seq, GLA_H * GLA_V), s_fin


def _mla_prompt_kernel(q_ref, k_ref, vt_ref, o_ref, m_ref, l_ref, acc_ref, *, tq):
    i = pl.program_id(1)
    m_ref[...] = jnp.full(m_ref.shape, -jnp.inf, F32)
    l_ref[...] = jnp.zeros(l_ref.shape, F32)
    acc_ref[...] = jnp.zeros(acc_ref.shape, F32)
    keep = (lax.broadcasted_iota(jnp.int32, (tq, tq), 0) <= lax.broadcasted_iota(jnp.int32, (tq, tq), 1))

    def tiles(js, last_masked):
        n = len(js)
        sts = [[_dot_nt(k_ref[pl.ds(pl.multiple_of(j * tq, tq), tq), LANES * h:LANES * (h + 1)],
                        q_ref[:, LANES * h:LANES * (h + 1)]) for j in js] for h in range(MLA_H)]
        ps, alphas = [], []
        for h in range(MLA_H):
            if last_masked:
                sts[h][-1] = jnp.where(keep, sts[h][-1], -jnp.inf)
            m_old = m_ref[h]
            m_new = m_old
            for st in sts[h]:
                m_new = jnp.maximum(m_new, jnp.max(st, axis=0, keepdims=True))
            alpha = jnp.exp2(m_old - m_new)
            ph = [jnp.exp2(st - m_new) for st in sts[h]]
            m_ref[h] = m_new
            l_new = alpha * l_ref[h]
            for p in ph:
                l_new = l_new + jnp.sum(p, axis=0, keepdims=True)
            l_ref[h] = l_new
            ps.append([p.astype(BF16) for p in ph])
            alphas.append(alpha)
        for h in range(MLA_H):
            acc = alphas[h] * acc_ref[h]
            for t in range(n):
                acc = acc + _dot(vt_ref[js[t], LANES * (h // 2):LANES * (h // 2 + 1), :], ps[h][t])
            acc_ref[h] = acc

    def body(jj, carry):
        tiles([ATT_GROUP * jj + t for t in range(ATT_GROUP)], False)
        return carry

    lax.fori_loop(0, i // ATT_GROUP, body, 0)
    for rem in range(ATT_GROUP):
        pl.when(i % ATT_GROUP == rem)(functools.partial(tiles, [i - rem + t for t in range(rem + 1)], True))

    row = lax.broadcasted_iota(jnp.int32, (LANES, 1), 0)
    for pair in range(MLA_H // 2):
        h0, h1 = 2 * pair, 2 * pair + 1
        a = acc_ref[h0] * (1.0 / l_ref[h0])
        b = acc_ref[h1] * (1.0 / l_ref[h1])
        o_ref[:, LANES * pair:LANES * (pair + 1)] = jnp.where(row < MLA_DV, a, b).T.astype(BF16)


def _mla_prompt(mq, mk, mvt, batch, seq, tq):
    nq = seq // tq
    return pl.pallas_call(
        functools.partial(_mla_prompt_kernel, tq=tq),
        grid=(batch, nq),
        in_specs=[pl.BlockSpec((tq, MLA_H * LANES), lambda b, i: (b * nq + i, 0)),
                  pl.BlockSpec((seq, MLA_H * LANES), lambda b, i: (b, 0)),
                  pl.BlockSpec((nq, MLA_H * MLA_DV, tq), lambda b, i: (b, 0, 0))],
        out_specs=pl.BlockSpec((tq, MLA_H * MLA_DV), lambda b, i: (b * nq + i, 0)),
        out_shape=jax.ShapeDtypeStruct((batch * seq, MLA_H * MLA_DV), BF16),
        scratch_shapes=[pltpu.VMEM((MLA_H, 1, tq), F32), pltpu.VMEM((MLA_H, 1, tq), F32),
                        pltpu.VMEM((MLA_H, LANES, tq), F32)],
        compiler_params=_cparams(("parallel", "arbitrary")),
        name="mla_prompt",
    )(mq, mk, mvt)


def _mla_sample_kernel(pt_ref, q_ref, wkq_ref, wkt_ref, wv_ref, cn_ref, kn_ref, ckv_hbm, kpe_hbm,
                       o_ref, ckv_buf, kpe_buf, sem, *, n_pages, t_new, kpm):
    b = pl.program_id(0)
    nb = pl.num_programs(0)
    half = b % 2
    nrow = MLA_H * t_new

    def page_copies(seq, hf, p):
        page = pt_ref[seq * n_pages + p]
        return (pltpu.make_async_copy(ckv_hbm.at[page], ckv_buf.at[hf, p], sem.at[0, hf]),
                pltpu.make_async_copy(kpe_hbm.at[page], kpe_buf.at[hf, p], sem.at[1, hf]))

    def start_pages(seq, hf, pages):
        for p in pages:
            for cp in page_copies(seq, hf, p):
                cp.start()

    def wait_pages(seq, hf, pages):
        for p in pages:
            for cp in page_copies(seq, hf, p):
                cp.wait()

    @pl.when(b == 0)
    def _():
        start_pages(0, 0, range(n_pages))

    nxt = (b + 1) % nb

    wait_pages(b, half, range(n_pages))

    qa, qr = [], []
    for h in range(MLA_H):
        qh = q_ref[:, LANES * h:LANES * (h + 1)]
        qa.append(_dot(qh, wkq_ref[h]))
        qr.append(pltpu.roll(qh.astype(F32), LANES - _KPE_LANE, 1)[:, 0:MLA_ROPE])
    lq = jnp.concatenate([jnp.concatenate(qa, axis=0).astype(BF16), wkt_ref[...]], axis=0)
    qp = jnp.concatenate(qr, axis=0).astype(BF16)

    def scores(r, s_rope):
        rinv = []
        for h in range(MLA_H):
            kr = r[nrow + MLA_NOPE * h:nrow + MLA_NOPE * (h + 1)]
            ms = jnp.sum(kr * kr, axis=0, keepdims=True) * (1.0 / MLA_NOPE)
            rinv.append(jnp.broadcast_to(lax.rsqrt(ms + EPS), (t_new, r.shape[1])))
        return r[0:nrow] * jnp.concatenate(rinv, axis=0) + s_rope

    ngrp = n_pages // kpm
    ppb = kpm * SAMPLE_GROUPS
    xs, ss = [], []
    for g0 in range(0, ngrp, SAMPLE_GROUPS):
        gs = range(g0, min(g0 + SAMPLE_GROUPS, ngrp))
        xg =[jnp.concatenate([ckv_buf[half, g * kpm + j] for j in range(kpm)], axis=0).astype(BF16) for g in gs]
        rg = [_dot_nt(lq, x) for x in xg]
        pg = [_dot(qp, jnp.concatenate([kpe_buf[half, g * kpm + j] for j in range(kpm)], axis=1).astype(BF16))
              for g in gs]
        xs.append(jnp.concatenate(xg, axis=0))
        ss.append(jnp.concatenate([scores(r, s_rope) for r, s_rope in zip(rg, pg)], axis=1))
        start_pages(nxt, 1 - half, range(g0 * kpm, min(g0 * kpm + ppb, n_pages)))

    xn = cn_ref[...].astype(BF16)
    sn = scores(_dot_nt(lq, xn), _dot_nt(qp, kn_ref[...].astype(BF16)))
    tq = lax.broadcasted_iota(jnp.int32, (nrow, t_new), 0) % t_new
    ts = lax.broadcasted_iota(jnp.int32, (nrow, t_new), 1)
    xs.append(xn)
    ss.append(jnp.where(ts <= tq, sn, -jnp.inf))

    m = jnp.max(ss[0], axis=-1, keepdims=True)
    for s in ss[1:]:
        m = jnp.maximum(m, jnp.max(s, axis=-1, keepdims=True))
    ps = [jnp.exp2(s - m) for s in ss]
    l = jnp.sum(ps[0], axis=-1, keepdims=True)
    for p in ps[1:]:
        l = l + jnp.sum(p, axis=-1, keepdims=True)
    inv_l = 1.0 / l
    pvs = [_dot(p.astype(BF16), x) for p, x in zip(ps, xs)]
    oa = None
    for c in range(SAMPLE_PV_CHAINS):
        part = pvs[c::SAMPLE_PV_CHAINS]
        acc = part[0]
        for pv in part[1:]:
            acc = acc + pv
        oa = acc * inv_l if oa is None else oa + acc * inv_l

    r = _dot(oa.astype(BF16), wv_ref[...])
    lane = lax.broadcasted_iota(jnp.int32, (1, MLA_H * MLA_DV), 1)
    out = jnp.zeros((t_new, MLA_H * MLA_DV), F32)
    for h in range(MLA_H):
        out = out + jnp.where(lane // MLA_DV == h, r[t_new * h:t_new * (h + 1)], 0.0)
    o_ref[...] = out.astype(BF16)

    @pl.when(b == nb - 1)
    def _():
        wait_pages(nxt, 1 - half, range(n_pages))


def _mla_sample(mq, ckv_new, kp_new, pool_ckv, pool_kpe_t, page_table, w, batch, t_new, kpm):
    n_pages = page_table.shape[1]
    pt = page_table.reshape(-1)
    tok = lambda width: pl.BlockSpec((None, t_new, width), lambda b, pt_ref: (b, 0, 0))
    g3 = lambda a: a.reshape(batch, t_new, a.shape[-1])
    const = lambda a: pl.BlockSpec(a.shape, lambda b, pt_ref: (0,) * a.ndim)
    hbm = pl.BlockSpec(memory_space=pl.ANY)
    grid_spec = pltpu.PrefetchScalarGridSpec(
        num_scalar_prefetch=1,
        grid=(batch,),
        in_specs=[tok(MLA_H * LANES), const(w['w_kq']), const(w['w_kt']), const(w['w_v']),
                  tok(MLA_KVR), tok(MLA_ROPE), hbm, hbm],
        out_specs=tok(MLA_H * MLA_DV),
        scratch_shapes=[pltpu.VMEM((2, n_pages, PAGE, MLA_KVR), F32),
                        pltpu.VMEM((2, n_pages, MLA_ROPE, PAGE), F32),
                        pltpu.SemaphoreType.DMA((2, 2))],
    )
    om = pl.pallas_call(
        functools.partial(_mla_sample_kernel, n_pages=n_pages, t_new=t_new, kpm=kpm),
        grid_spec=grid_spec,
        out_shape=jax.ShapeDtypeStruct((batch, t_new, MLA_H * MLA_DV), BF16),
        compiler_params=_cparams(("arbitrary",)),
        name="mla_sample",
    )(pt, g3(mq), w['w_kq'], w['w_kt'], w['w_v'], g3(ckv_new), g3(kp_new), pool_ckv, pool_kpe_t)
    return om.reshape(batch * t_new, MLA_H * MLA_DV)


def _mix_out_kernel(x_ref, og_ref, om_ref, wo_ref, gx_ref, wxq_ref, gxq_ref, x1_o, xq_o):
    x1 = x_ref[...] + _dot(og_ref[...], wo_ref[0:512, :]) + _dot(om_ref[...], wo_ref[512:1024, :])
    x1_o[...] = x1
    qf = _dot(_rms(x1, gx_ref[...]).astype(BF16), wxq_ref[...])
    g = gxq_ref[...]
    for h in range(X_H):
        hs = slice(X_D * h, X_D * (h + 1))
        xq_o[:, hs] = _rms(qf[:, hs], g).astype(BF16)


def _mix_out(x, og, om, w, tm):
    n = x.shape[0]
    row = lambda width: pl.BlockSpec((tm, width), lambda i: (i, 0))
    consts = [w['w_out'], w['g_x'], w['w_xq'], w['g_xq']]
    return pl.pallas_call(
        _mix_out_kernel,
        grid=(n // tm,),
        in_specs=[row(D_MODEL), row(512), row(512)] + [_full(c.shape) for c in consts],
        out_specs=[row(D_MODEL), row(X_H * X_D)],
        out_shape=[jax.ShapeDtypeStruct((n, D_MODEL), F32), jax.ShapeDtypeStruct((n, X_H * X_D), BF16)],
        compiler_params=_cparams(("parallel",)),
        name="mix_out",
    )(x, og, om, *consts)


def _cross_kernel(xq_ref, mk_ref, mv_ref, o_ref, *, n_mem):
    gb = xq_ref.shape[0]

    def head(ref, g, h):
        if len(ref.shape) == 4:
            return ref[g, h]
        return ref[pl.ds(g * n_mem * X_H + h, n_mem, stride=X_H), :].astype(BF16)

    units = [(g, h) for g in range(gb) for h in range(X_H)]
    ss = [_dot_nt(xq_ref[g, :, X_D * h:X_D * (h + 1)], head(mk_ref, g, h)) for g, h in units]
    ps = [jnp.exp2(s - jnp.max(s, axis=-1, keepdims=True)) for s in ss]
    for (g, h), p in zip(units, ps):
        o = _dot(p.astype(BF16), head(mv_ref, g, h))
        o_ref[g, :, X_D * h:X_D * (h + 1)] = (o * (1.0 / jnp.sum(p, axis=-1, keepdims=True))).astype(BF16)


def _cross(xq, mk, mv, batch, seq, n_mem, tq, gb):
    nq = seq // tq
    assert gb == 1 or nq == 1
    row = pl.BlockSpec((gb, tq, X_H * X_D), lambda b, i: (b * nq + i, 0, 0))
    if mk.ndim == 4:
        mem = pl.BlockSpec((gb,) + mk.shape[1:], lambda b, i: (b, 0, 0, 0))
    else:
        mem = pl.BlockSpec((gb * n_mem * X_H, X_D), lambda b, i: (b, 0))
    o = pl.pallas_call(
        functools.partial(_cross_kernel, n_mem=n_mem),
        grid=(batch // gb, nq),
        in_specs=[row, mem, mem],
        out_specs=row,
        out_shape=jax.ShapeDtypeStruct((batch * nq, tq, X_H * X_D), BF16),
        compiler_params=_cparams(("parallel", "arbitrary")),
        name="cross",
    )(xq.reshape(batch * nq, tq, X_H * X_D), mk, mv)
    return o.reshape(batch * seq, X_H * X_D)


def _ffn_kernel(x_ref, o_ref, wxo_ref, g_ref, w1_ref, w2_ref, y_o, *, fc):
    x = x_ref[...] + _dot(o_ref[...], wxo_ref[...])
    h = _rms(x, g_ref[...]).astype(BF16)
    acc = x
    for c in range(D_FF // fc):
        u = jnp.maximum(_dot(h, w1_ref[:, fc * c:fc * (c + 1)]), 0.0)
        acc = acc + _dot((u * u).astype(BF16), w2_ref[fc * c:fc * (c + 1), :])
    y_o[...] = acc


def _ffn(x, o, w, tm, fc):
    n = x.shape[0]
    row = lambda width: pl.BlockSpec((tm, width), lambda i: (i, 0))
    consts = [w['w_xo'], w['g_ff'], w['w_ff1'], w['w_ff2']]
    return pl.pallas_call(
        functools.partial(_ffn_kernel, fc=fc),
        grid=(n // tm,),
        in_specs=[row(D_MODEL), row(X_H * X_D)] + [_full(c.shape) for c in consts],
        out_specs=row(D_MODEL),
        out_shape=jax.ShapeDtypeStruct((n, D_MODEL), F32),
        compiler_params=_cparams(("parallel",)),
        name="ffn",
    )(x, o, *consts)


def _mem_kv_kernel(m_ref, gm_ref, wk_ref, wv_ref, gk_ref, k_o, v_o, kh_o, vh_o, *, n_mem):
    m = _rms(m_ref[...], gm_ref[...]).astype(BF16)
    kf = _dot(m, wk_ref[...])
    vf = _dot(m, wv_ref[...])
    v_o[...] = vf
    g = gk_ref[...]
    for h in range(X_H):
        hs = slice(X_D * h, X_D * (h + 1))
        kn = _rms(kf[:, hs], g)
        k_o[:, hs] = kn
        for b in range(kh_o.shape[0]):
            rs = slice(n_mem * b, n_mem * (b + 1))
            kh_o[b, h] = kn[rs].astype(BF16)
            vh_o[b, h] = vf[rs, hs].astype(BF16)


def _mem_kv(mem, w, n_mem, tm):
    n = mem.shape[0]
    bt = tm // n_mem
    row = lambda width: pl.BlockSpec((tm, width), lambda i: (i, 0))
    head = pl.BlockSpec((bt, X_H, n_mem, X_D), lambda i: (i, 0, 0, 0))
    consts = [w['g_mem'], w['w_xk'], w['w_xv'], w['g_xk']]
    return pl.pallas_call(
        functools.partial(_mem_kv_kernel, n_mem=n_mem),
        grid=(n // tm,),
        in_specs=[row(D_MODEL)] + [_full(c.shape) for c in consts],
        out_specs=[row(X_H * X_D), row(X_H * X_D), head, head],
        out_shape=[jax.ShapeDtypeStruct((n, X_H * X_D), F32)] * 2
                  + [jax.ShapeDtypeStruct((n // n_mem, X_H, n_mem, X_D), BF16)] * 2,
        compiler_params=_cparams(("parallel",)),
        name="mem_kv",
    )(mem, *consts)


def _prep_weights(g_mix, w_in, w_gla_a2, b_gla_a, g_gla_o, g_mla_qa, w_mla_qb, g_mla_kva, w_mla_kvb,
                  g_q_nope, g_k_nope, g_q_rope, g_k_rope, w_out, g_x, g_mem, w_xq, w_xk, w_xv, g_xq, g_xk,
                  w_xo, g_ff, w_ff1, w_ff2):
    rowv = lambda g: g.reshape(1, -1).astype(F32)
    zc = lambda n: jnp.zeros((D_MODEL, n), F32)
    sizes = (256, 256, 512, 512, GLA_RANK, MLA_QR, MLA_KVR, MLA_ROPE)
    offs = [0]
    for s in sizes:
        offs.append(offs[-1] + s)
    q, k, v, r, a, qa, kva, kpe = [w_in[:, offs[i]:offs[i + 1]] for i in range(8)]
    w_in_p = jnp.concatenate(
        [q * (GLA_K ** -0.5), k, v, r, qa, kva, a, zc(LANES - GLA_RANK),
         zc(_KPE_LANE), kpe, zc(LANES - _KPE_LANE - MLA_ROPE)], axis=1)
    w_a2 = jnp.concatenate([w_gla_a2, jnp.zeros((LANES - GLA_RANK, GLA_H * GLA_K), F32)], axis=0)

    qb = w_mla_qb.reshape(MLA_QR, MLA_H, MLA_NOPE + MLA_ROPE)
    qb = jnp.pad(qb, ((0, 0), (0, 0), (0, LANES - MLA_NOPE - MLA_ROPE))).reshape(MLA_QR, MLA_H * LANES)
    kvb = w_mla_kvb.reshape(MLA_KVR, MLA_H, MLA_NOPE + MLA_DV)
    wk = kvb[:, :, :MLA_NOPE]
    wk_p = jnp.pad(wk, ((0, 0), (0, 0), (0, LANES - MLA_NOPE))).reshape(MLA_KVR, MLA_H * LANES)
    wv = kvb[:, :, MLA_NOPE:].reshape(MLA_KVR, MLA_H * MLA_DV)
    z32 = jnp.zeros((LANES - MLA_NOPE - MLA_ROPE,), F32)
    gq_row = jnp.tile(jnp.concatenate([g_q_nope, g_q_rope, z32]), MLA_H) * (MLA_SCALE * LOG2E)
    gk_row = jnp.tile(jnp.concatenate([g_k_nope, jnp.zeros((LANES - MLA_NOPE,), F32)]), MLA_H)
    g_kpe = jnp.concatenate([jnp.zeros((_KPE_LANE,), F32), g_k_rope, z32])

    li = jnp.arange(2 * LANES)
    seg = jnp.where(li % LANES < MLA_NOPE, 0, jnp.where(li % LANES < MLA_NOPE + MLA_ROPE, 1, 2))
    same = (li[:, None] // LANES == li[None, :] // LANES) & (seg[:, None] == seg[None, :]) & (seg[:, None] < 2)
    bd = jnp.where(same, jnp.where(seg[:, None] == 0, 1.0 / MLA_NOPE, 1.0 / MLA_ROPE), 0.0)

    wkq = jnp.transpose(wk, (1, 2, 0)) * g_k_nope[None, :, None]
    wkq = jnp.pad(wkq, ((0, 0), (0, LANES - MLA_NOPE), (0, 0)))
    wkt = jnp.transpose(wk, (1, 2, 0)).reshape(MLA_H * MLA_NOPE, MLA_KVR)

    hk = jnp.arange(GLA_H * GLA_K) // GLA_K
    hv = jnp.arange(GLA_H * GLA_V) // GLA_V
    bexp = (hk[:, None] == hv[None, :])
    return {
        'g_mix': rowv(g_mix), 'w_in': w_in_p.astype(BF16), 'w_a2': w_a2.astype(BF16), 'b_a': rowv(b_gla_a),
        'g_qa': rowv(g_mla_qa), 'w_qb': qb.astype(BF16), 'gq_row': rowv(gq_row), 'g_kva': rowv(g_mla_kva),
        'w_k': wk_p.astype(BF16), 'gk_row': rowv(gk_row), 'w_v': wv.astype(BF16), 'g_kpe': rowv(g_kpe),
        'bd': bd.astype(BF16), 'w_kq': wkq.astype(BF16), 'w_kt': wkt.astype(BF16),
        'g_gla_o': rowv(g_gla_o), 'bexp': bexp.astype(BF16),
        'w_out': w_out.astype(BF16), 'g_x': rowv(g_x), 'w_xq': w_xq.astype(BF16),
        'g_xq': rowv(g_xq) * (X_SCALE * LOG2E),
        'g_mem': rowv(g_mem), 'w_xk': w_xk.astype(BF16), 'w_xv': w_xv.astype(BF16), 'g_xk': rowv(g_xk),
        'w_xo': w_xo.astype(BF16), 'g_ff': rowv(g_ff), 'w_ff1': w_ff1.astype(BF16), 'w_ff2': w_ff2.astype(BF16),
    }


def _rope_tables(pos):
    half = MLA_ROPE // 2
    inv = ROPE_THETA ** (-jnp.arange(half, dtype=F32) / half)
    ang = pos.astype(F32)[:, None] * inv[None, :]
    cos, sin = jnp.cos(ang), jnp.sin(ang)
    n = pos.shape[0]
    one = jnp.ones((n, _KPE_LANE), F32)
    z = lambda w_: jnp.zeros((n, w_), F32)
    tail = LANES - _KPE_LANE - MLA_ROPE
    c = jnp.concatenate([one, cos, cos, jnp.ones((n, tail), F32)], axis=1)
    s_up = jnp.concatenate([z(_KPE_LANE), -sin, z(half), z(tail)], axis=1)
    s_dn = jnp.concatenate([z(_KPE_LANE), z(half), sin, z(tail)], axis=1)
    return c, s_up, s_dn


def _tile_rows(n, cap):
    t = min(n, cap)
    while n % t:
        t //= 2
    return t


def _tail(x1, xq, mk, mv, w, batch, seq, n_mem):
    tq = _tile_rows(seq, 512)
    gb = _tile_rows(batch, max(1, CROSS_ROWS // seq)) if tq == seq else 1
    o = _cross(xq, mk, mv, batch, seq, n_mem, tq, gb)
    return _ffn(x1, o, w, _tile_rows(x1.shape[0], 512), 1024)


def _prompt_layer(x, mem, w):
    batch, seq, _ = x.shape
    n = batch * seq
    xf = x.reshape(n, D_MODEL)
    tm = _tile_rows(seq, 512)
    tabs = _rope_tables(jnp.arange(seq))
    gq, gk, gv, gr, la, mq, ckv, kp, mk_, mvt = _in_proj(xf, w, tabs, tm, True)
    c_len = GLA_CHUNK
    cps = _tile_rows(seq // c_len, 4)
    s0 = jnp.zeros((batch, GLA_H * GLA_K, GLA_V), F32)
    og, s_fin = _gla(gq, gk, gv, gr, la, s0, w, batch, seq, c_len, min(GLA_SUB, c_len), cps)
    om = _mla_prompt(mq, mk_, mvt, batch, seq, ATT_TILE)
    x1, xq = _mix_out(xf, og, om, w, tm)
    n_mem = mem.shape[1]
    memf = mem.reshape(-1, D_MODEL)
    xk, xv, xkh, xvh = _mem_kv(memf, w, n_mem, max(n_mem, _tile_rows(memf.shape[0], 512)))
    y = _tail(x1, xq, xkh, xvh, w, batch, seq, n_mem)
    return y, ckv, kp, xk, xv, s_fin


def _sample_layer(x, pool_ckv, pool_kpe, page_table, mem_k, mem_v, s_prev, w):
    batch, seq, _ = x.shape
    n = batch * seq
    xf = x.reshape(n, D_MODEL)
    tm = _tile_rows(n, 512)
    past = page_table.shape[1] * PAGE
    pos = past + (jnp.arange(tm) % seq)
    gq, gk, gv, gr, la, mq, ckv, kp = _in_proj(xf, w, _rope_tables(pos), tm, False)
    s0 = s_prev.reshape(batch, GLA_H * GLA_K, GLA_V)
    og, s_fin = _gla(gq, gk, gv, gr, la, s0, w, batch, seq, seq, seq, 1)
    om = _mla_sample(mq, ckv, kp, pool_ckv, jnp.swapaxes(pool_kpe, 1, 2), page_table, w, batch, seq,
                     _tile_rows(page_table.shape[1], SAMPLE_PAGES_PER_DOT))
    x1, xq = _mix_out(xf, og, om, w, tm)
    y = _tail(x1, xq, mem_k.reshape(-1, X_D), mem_v.reshape(-1, X_D), w, batch, seq, mem_k.shape[1])
    return y, ckv, kp, s_fin


def kernel(x_prompt, x_sample, mem_prompt, cache_ckv, cache_kpe, cache_mem_k, cache_mem_v, state_gla, page_table, g_mix, w_in, w_gla_a2, b_gla_a, g_gla_o, g_mla_qa, w_mla_qb, g_mla_kva, w_mla_kvb, g_q_nope, g_k_nope, g_q_rope, g_k_rope, w_out, g_x, g_mem, w_xq, w_xk, w_xv, g_xq, g_xk, w_xo, g_ff, w_ff1, w_ff2):
    depth = w_in.shape[0]
    assert depth == 1, "one layer: prompt-group caches of layer l would feed layer l+1 otherwise unchanged"
    params = (g_mix, w_in, w_gla_a2, b_gla_a, g_gla_o, g_mla_qa, w_mla_qb, g_mla_kva, w_mla_kvb,
              g_q_nope, g_k_nope, g_q_rope, g_k_rope, w_out, g_x, g_mem, w_xq, w_xk, w_xv, g_xq, g_xk,
              w_xo, g_ff, w_ff1, w_ff2)
    w = _prep_weights(*[p[0] for p in params])
    bp, tp, _ = x_prompt.shape
    bs, tsq, _ = x_sample.shape
    yp, ckv_p, kp_p, xk, xv, gla_p = _prompt_layer(x_prompt, mem_prompt, w)
    ys, ckv_s, kp_s, gla_s = _sample_layer(x_sample, cache_ckv.reshape(cache_ckv.shape[1:]), cache_kpe.reshape(cache_kpe.shape[1:]), page_table,
                                            cache_mem_k[0], cache_mem_v[0], state_gla[0], w)
    n_mem = mem_prompt.shape[1]
    return (yp.reshape(bp, tp, D_MODEL), ys.reshape(bs, tsq, D_MODEL),
            ckv_p.reshape(1, bp, tp, MLA_KVR), kp_p.reshape(1, bp, tp, MLA_ROPE),
            xk.reshape(1, bp, n_mem, X_H, X_D), xv.reshape(1, bp, n_mem, X_H, X_D),
            gla_p.reshape(1, bp, GLA_H, GLA_K, GLA_V),
            ckv_s.reshape(1, bs, tsq, MLA_KVR), kp_s.reshape(1, bs, tsq, MLA_ROPE),
            gla_s.reshape(1, bs, GLA_H, GLA_K, GLA_V))
```

```python
import functools

import jax
import jax.numpy as jnp
from jax import lax
from jax.experimental import pallas as pl
from jax.experimental.pallas import tpu as pltpu

F32 = jnp.float32
BF16 = jnp.bfloat16

EPS = 1e-6
D_MODEL = 1024
GLA_H, GLA_K, GLA_V, GLA_RANK, GLA_TAU, GLA_CHUNK = 4, 64, 128, 16, 16.0, 64
GLA_SUB = 16
GLA_CHUNKS_PER_STEP = 4
GLA_FAST_MAX = 40.0
MLA_H, MLA_DV, MLA_NOPE, MLA_ROPE, MLA_QR, MLA_KVR = 8, 64, 64, 32, 384, 256
MLA_SCALE = (MLA_NOPE + MLA_ROPE) ** -0.5
ROPE_THETA = 10000.0
X_H, X_D = 4, 128
X_SCALE = X_D ** -0.5
D_FF = 4096
PAGE = 128
LANES = 128
NEG = -1e30
LOG2E = 1.4426950408889634
ATT_TILE = 256
ATT_GROUP = 2
CROSS_ROWS = 32
SAMPLE_PAGES_PER_DOT = 4
SAMPLE_PV_CHAINS = 2
SAMPLE_GROUPS = 4

_C_Q, _C_K, _C_V, _C_R = 0, 256, 512, 1024
_C_QA, _C_AK, _C_KVA, _C_END = 1536, 1920, 2048, 2304
_KPE_LANE = 64

VMEM_LIMIT = 56 * 1024 * 1024


def _cparams(sem):
    return pltpu.CompilerParams(dimension_semantics=sem, vmem_limit_bytes=VMEM_LIMIT)


def _dot(a, b):
    return jnp.dot(a, b, preferred_element_type=F32)


def _dot_nt(a, b):
    return lax.dot_general(a, b, (((1,), (1,)), ((), ())), preferred_element_type=F32)


def _rms(x, g):
    return x * lax.rsqrt(jnp.mean(x * x, axis=-1, keepdims=True) + EPS) * g


def _full(shape):
    n = len(shape)
    return pl.BlockSpec(shape, lambda *_: (0,) * n)


def _in_proj_kernel(x_ref, gmix_ref, win_ref, wa2_ref, ba_ref, gqa_ref, wqb_ref, gqrow_ref,
                    gkva_ref, wkv_ref, gkrow_ref, gkpe_ref, bd_ref, c_ref, s1_ref, s2_ref,
                    gq_o, gk_o, gv_o, gr_o, la_o, mq_o, ckv_o, kp_o, *kv_o):
    h = _rms(x_ref[...], gmix_ref[...]).astype(BF16)

    z2 = _dot(h, win_ref[:, _C_QA:_C_END])
    qa = z2[:, 0:_C_AK - _C_QA]
    zak = z2[:, _C_AK - _C_QA:_C_KVA - _C_QA]
    kva = z2[:, _C_KVA - _C_QA:]

    gate = _dot(zak.astype(BF16), wa2_ref[...]) + ba_ref[...]
    la_o[...] = (jnp.minimum(gate, 0.0) - jnp.log1p(jnp.exp(-jnp.abs(gate)))) * (1.0 / GLA_TAU)

    cos, s_up, s_dn = c_ref[...], s1_ref[...], s2_ref[...]

    def rope(t):
        return t * cos + pltpu.roll(t, LANES - 16, 1) * s_up + pltpu.roll(t, 16, 1) * s_dn

    lane = lax.broadcasted_iota(jnp.int32, (1, LANES), 1)
    zk = jnp.where((lane >= _KPE_LANE) & (lane < _KPE_LANE + MLA_ROPE), zak, 0.0)
    kpn = zk * lax.rsqrt(jnp.sum(zk * zk, axis=-1, keepdims=True) * (1.0 / MLA_ROPE) + EPS) * gkpe_ref[...]
    kpt = rope(kpn)
    kp_o[...] = pltpu.roll(kpt, LANES - _KPE_LANE, 1)[:, 0:MLA_ROPE]

    ckv = _rms(kva, gkva_ref[...])
    ckv_o[...] = ckv
    cb = ckv.astype(BF16)
    bd = bd_ref[...]

    def seg_norm(t, grow):
        ms = _dot((t * t).astype(BF16), bd)
        return t * lax.rsqrt(ms + EPS) * grow

    qf = _dot(_rms(qa, gqa_ref[...]).astype(BF16), wqb_ref[...])
    for c in range(4):
        sl = slice(2 * LANES * c, 2 * LANES * (c + 1))
        qn = seg_norm(qf[:, sl], gqrow_ref[:, sl])
        for j in range(2):
            o = 2 * LANES * c + LANES * j
            mq_o[:, o:o + LANES] = rope(qn[:, LANES * j:LANES * (j + 1)]).astype(BF16)

    if kv_o:
        mk_o, mvt_o = kv_o
        kvf = _dot(cb, wkv_ref[...])
        kf, mv = kvf[:, 0:MLA_H * LANES], kvf[:, MLA_H * LANES:]
        for c in range(mvt_o.shape[0]):
            mvt_o[c] = mv[ATT_TILE * c:ATT_TILE * (c + 1), :].T.astype(BF16)
        for c in range(4):
            sl = slice(2 * LANES * c, 2 * LANES * (c + 1))
            kn = seg_norm(kf[:, sl], gkrow_ref[:, sl])
            for j in range(2):
                o = 2 * LANES * c + LANES * j
                mk_o[:, o:o + LANES] = (kn[:, LANES * j:LANES * (j + 1)] + kpt).astype(BF16)

    z1 = _dot(h, win_ref[:, _C_Q:_C_QA])
    gq_o[...] = z1[:, _C_Q:_C_K].astype(BF16)
    gk_o[...] = z1[:, _C_K:_C_V].astype(BF16)
    gv_o[...] = z1[:, _C_V:_C_R].astype(BF16)
    gr_o[...] = z1[:, _C_R:_C_QA].astype(BF16)


def _in_proj(x, w, tabs, tm, with_kv):
    n = x.shape[0]
    nt = tabs[0].shape[0] // tm
    row = lambda width: pl.BlockSpec((tm, width), lambda i: (i, 0))
    tab = pl.BlockSpec((tm, LANES), lambda i: (i % nt, 0))
    consts = [w['g_mix'], w['w_in'], w['w_a2'], w['b_a'], w['g_qa'], w['w_qb'], w['gq_row'],
              w['g_kva'], w['w_kv'], w['gk_row'], w['g_kpe'], w['bd']]
    out_w = [(256, BF16), (256, BF16), (512, BF16), (512, BF16), (256, F32),
             (1024, BF16), (256, F32), (MLA_ROPE, F32)] + ([(1024, BF16)] if with_kv else [])
    out_specs = [row(wd) for wd, _ in out_w]
    out_shape = [jax.ShapeDtypeStruct((n, wd), dt) for wd, dt in out_w]
    if with_kv:
        out_specs.append(pl.BlockSpec((tm // ATT_TILE, MLA_H * MLA_DV, ATT_TILE), lambda i: (i, 0, 0)))
        out_shape.append(jax.ShapeDtypeStruct((n // ATT_TILE, MLA_H * MLA_DV, ATT_TILE), BF16))
    return pl.pallas_call(
        _in_proj_kernel,
        grid=(n // tm,),
        in_specs=[row(D_MODEL)] + [_full(c.shape) for c in consts] + [tab, tab, tab],
        out_specs=out_specs,
        out_shape=out_shape,
        compiler_params=_cparams(("parallel",)),
        name="in_proj",
    )(x, *consts, *tabs)


def _head_masks():
    lane = lax.broadcasted_iota(jnp.int32, (1, GLA_H * GLA_K), 1)
    return [(lane // GLA_K == h).astype(F32) for h in range(GLA_H)]


def _gla_fast_step(q, k, v, b, st, keep, c_len, cps):
    masks = _head_masks()
    ts = c_len * cps
    eb = jnp.exp(b)
    qt = q * eb
    kt = k * jnp.exp(-b)
    lq = jnp.concatenate([qt * m for m in masks], axis=0).astype(BF16)
    a = _dot_nt(lq, kt.astype(BF16))
    vb = v.astype(BF16)
    o_intra = [_dot((a[ts * h:ts * (h + 1)] * keep).astype(BF16), vb[:, GLA_V * h:GLA_V * (h + 1)])
               for h in range(GLA_H)]
    o_inter = []
    for c in range(cps):
        cs = slice(c_len * c, c_len * (c + 1))
        lqc = jnp.concatenate([lq[ts * h + c_len * c:ts * h + c_len * (c + 1)] for h in range(GLA_H)], axis=0)
        o_inter.append(_dot_nt(lqc, st.astype(BF16)))
        ebl = eb[c_len * (c + 1) - 1:c_len * (c + 1), :]
        kl = kt[cs] * ebl
        vs = jnp.concatenate([v[cs, GLA_V * h:GLA_V * (h + 1)] for h in range(GLA_H)], axis=0)
        ks = jnp.concatenate([kl * m for m in masks], axis=0).astype(BF16)
        st = st * ebl + _dot(vs.T.astype(BF16), ks)
    outs = [o_intra[h] + jnp.concatenate([o_inter[c][c_len * h:c_len * (h + 1)] for c in range(cps)], axis=0)
            for h in range(GLA_H)]
    return outs, st


def _gla_chunk(q, k, v, b, st, bexp, c_len, sub):
    masks = _head_masks()
    bl = b[c_len - 1:c_len, :]

    qh = q * jnp.exp(b)
    lq = jnp.concatenate([qh * m for m in masks], axis=0).astype(BF16)
    o_inter = _dot_nt(lq, st.astype(BF16))

    nsub = c_len // sub
    a_rows = [[] for _ in range(GLA_H)]
    col = lax.broadcasted_iota(jnp.int32, (1, c_len), 1)
    for i in range(1, nsub):
        ref = b[sub * i - 1:sub * i, :]
        qi = q[sub * i:sub * (i + 1)] * jnp.exp(b[sub * i:sub * (i + 1)] - ref)
        ki = k * jnp.exp(jnp.minimum(ref - b, 0.0))
        li = jnp.concatenate([qi * m for m in masks], axis=0).astype(BF16)
        ai = jnp.where(col < sub * i, _dot_nt(li, ki.astype(BF16)), 0.0)
        for h in range(GLA_H):
            a_rows[h].append(ai[sub * h:sub * (h + 1)])

    row = lax.broadcasted_iota(jnp.int32, (sub, 1), 0)
    o_diag = []
    for i in range(nsub):
        sl = slice(sub * i, sub * (i + 1))
        bb, qb, kb, vb = b[sl], q[sl], k[sl], v[sl]
        ps = []
        for s in range(sub):
            e = jnp.exp(jnp.where(row >= s, bb - bb[s:s + 1, :], NEG))
            ps.append(qb * e * kb[s:s + 1, :])
        r = _dot(jnp.concatenate(ps, axis=0).astype(BF16), bexp)
        od = r[0:sub] * vb[0:1, :]
        for s in range(1, sub):
            od = od + r[sub * s:sub * (s + 1)] * vb[s:s + 1, :]
        o_diag.append(od)
    o_diag = jnp.concatenate(o_diag, axis=0) if nsub > 1 else o_diag[0]

    outs = []
    for h in range(GLA_H):
        oh = o_inter[c_len * h:c_len * (h + 1)] + o_diag[:, GLA_V * h:GLA_V * (h + 1)]
        if nsub > 1:
            ah = jnp.concatenate([jnp.zeros((sub, c_len), F32)] + a_rows[h], axis=0)
            oh = oh + _dot(ah.astype(BF16), v[:, GLA_V * h:GLA_V * (h + 1)].astype(BF16))
        outs.append(oh)

    kl = k * jnp.exp(bl - b)
    vs = jnp.concatenate([v[:, GLA_V * h:GLA_V * (h + 1)] for h in range(GLA_H)], axis=0)
    ks = jnp.concatenate([kl * m for m in masks], axis=0).astype(BF16)
    st_new = st * jnp.exp(bl) + _dot(vs.T.astype(BF16), ks)
    return outs, st_new


def _gla_kernel(*refs, c_len, sub, cps, fast):
    if fast:
        (q_ref, k_ref, v_ref, r_ref, la_ref, s0_ref, g_ref, bexp_ref, tri_ref, keep_ref,
         og_o, sf_o, st_ref, b_ref) = refs
    else:
        q_ref, k_ref, v_ref, r_ref, la_ref, s0_ref, g_ref, bexp_ref, og_o, sf_o, st_ref, b_ref = refs
    t = pl.program_id(1)
    ts = c_len * cps

    @pl.when(t == 0)
    def _():
        st_ref[...] = s0_ref[...].T

    g = g_ref[...]

    def emit(outs, rs):
        rr = r_ref[rs, :].astype(F32)
        for h in range(GLA_H):
            hs = slice(GLA_V * h, GLA_V * (h + 1))
            rh = rr[:, hs]
            og_o[rs, hs] = (_rms(outs[h], g) * (rh / (1.0 + jnp.exp(-rh)))).astype(BF16)

    la = la_ref[...]
    if fast:
        la_hi = la.astype(BF16)
        la_lo = (la - la_hi.astype(F32)).astype(BF16)
        b = _dot(tri_ref[...], la_hi) + _dot(tri_ref[...], la_lo)
    else:
        rows = lax.broadcasted_iota(jnp.int32, (c_len, 1), 0)
        b = jnp.zeros_like(la)
        for j in range(c_len):
            b = b + jnp.where(rows >= j, la[j:j + 1, :], 0.0)
    b_ref[...] = b

    def robust():
        bexp = bexp_ref[...]

        def body(c, carry):
            rs = pl.ds(pl.multiple_of(c * c_len, c_len), c_len)
            outs, st_new = _gla_chunk(q_ref[rs, :].astype(F32), k_ref[rs, :].astype(F32),
                                      v_ref[rs, :].astype(F32), b_ref[rs, :], st_ref[...], bexp, c_len, sub)
            st_ref[...] = st_new
            emit(outs, rs)
            return carry

        lax.fori_loop(0, cps, body, 0)

    if fast:
        mild = jnp.min(b) >= -GLA_FAST_MAX

        @pl.when(mild)
        def _():
            outs, st_new = _gla_fast_step(q_ref[...].astype(F32), k_ref[...].astype(F32), v_ref[...].astype(F32),
                                          b_ref[...], st_ref[...], keep_ref[...], c_len, cps)
            st_ref[...] = st_new
            emit(outs, slice(None))

        pl.when(jnp.logical_not(mild))(robust)
    else:
        robust()

    @pl.when(t == pl.num_programs(1) - 1)
    def _():
        sf_o[...] = st_ref[...].T


def _gla(gq, gk, gv, gr, la, s0, w, batch, seq, c_len, sub, cps):
    ts = c_len * cps
    nt = seq // ts
    fast = c_len == GLA_CHUNK
    assert fast or cps == 1
    row = lambda width: pl.BlockSpec((None, ts, width), lambda b, t: (b * nt + t, 0, 0))
    g3 = lambda a: a.reshape(batch * nt, ts, a.shape[-1])
    st_spec = pl.BlockSpec((None, GLA_H * GLA_K, GLA_V), lambda b, t: (b, 0, 0))
    consts = [w['g_gla_o'], w['bexp']]
    if fast:
        ti = jnp.arange(ts)
        keep = (ti[:, None] // c_len == ti[None, :] // c_len) & (ti[None, :] <= ti[:, None])
        consts += [keep.astype(BF16), keep.astype(F32)]
    og, s_fin = pl.pallas_call(
        functools.partial(_gla_kernel, c_len=c_len, sub=sub, cps=cps, fast=fast),
        grid=(batch, nt),
        in_specs=[row(256), row(256), row(512), row(512), row(256), st_spec] + [_full(c.shape) for c in consts],
        out_specs=[row(512), st_spec],
        out_shape=[jax.ShapeDtypeStruct((batch * nt, ts, GLA_H * GLA_V), BF16),
                   jax.ShapeDtypeStruct((batch, GLA_H * GLA_K, GLA_V), F32)],
        scratch_shapes=[pltpu.VMEM((GLA_V, GLA_H * GLA_K), F32), pltpu.VMEM((ts, GLA_H * GLA_K), F32)],
        compiler_params=_cparams(("parallel", "arbitrary")),
        name="gla",
    )(g3(gq), g3(gk), g3(gv), g3(gr), g3(la), s0, *consts)
    return og.reshape(batch * seq, GLA_H * GLA_V), s_fin


def _mla_prompt_kernel(q_ref, k_ref, vt_ref, o_ref, m_ref, l_ref, acc_ref, *, tq):
    i = pl.program_id(1)
    m_ref[...] = jnp.full(m_ref.shape, -jnp.inf, F32)
    l_ref[...] = jnp.zeros(l_ref.shape, F32)
    acc_ref[...] = jnp.zeros(acc_ref.shape, F32)
    keep = (lax.broadcasted_iota(jnp.int32, (tq, tq), 0) <= lax.broadcasted_iota(jnp.int32, (tq, tq), 1))

    def tiles(js, last_masked):
        n = len(js)
        sts = [[_dot_nt(k_ref[pl.ds(pl.multiple_of(j * tq, tq), tq), LANES * h:LANES * (h + 1)],
                        q_ref[:, LANES * h:LANES * (h + 1)]) for j in js] for h in range(MLA_H)]
        ps, alphas = [], []
        for h in range(MLA_H):
            if last_masked:
                sts[h][-1] = jnp.where(keep, sts[h][-1], -jnp.inf)
            m_old = m_ref[h]
            m_new = m_old
            for st in sts[h]:
                m_new = jnp.maximum(m_new, jnp.max(st, axis=0, keepdims=True))
            alpha = jnp.exp2(m_old - m_new)
            ph = [jnp.exp2(st - m_new) for st in sts[h]]
            m_ref[h] = m_new
            l_new = alpha * l_ref[h]
            for p in ph:
                l_new = l_new + jnp.sum(p, axis=0, keepdims=True)
            l_ref[h] = l_new
            ps.append([p.astype(BF16) for p in ph])
            alphas.append(alpha)
        for h in range(MLA_H):
            acc = alphas[h] * acc_ref[h]
            for t in range(n):
                acc = acc + _dot(vt_ref[js[t], LANES * (h // 2):LANES * (h // 2 + 1), :], ps[h][t])
            acc_ref[h] = acc

    def body(jj, carry):
        tiles([ATT_GROUP * jj + t for t in range(ATT_GROUP)], False)
        return carry

    lax.fori_loop(0, i // ATT_GROUP, body, 0)
    for rem in range(ATT_GROUP):
        pl.when(i % ATT_GROUP == rem)(functools.partial(tiles, [i - rem + t for t in range(rem + 1)], True))

    row = lax.broadcasted_iota(jnp.int32, (LANES, 1), 0)
    for pair in range(MLA_H // 2):
        h0, h1 = 2 * pair, 2 * pair + 1
        a = acc_ref[h0] * (1.0 / l_ref[h0])
        b = acc_ref[h1] * (1.0 / l_ref[h1])
        o_ref[:, LANES * pair:LANES * (pair + 1)] = jnp.where(row < MLA_DV, a, b).T.astype(BF16)


def _mla_prompt(mq, mk, mvt, batch, seq, tq):
    nq = seq // tq
    return pl.pallas_call(
        functools.partial(_mla_prompt_kernel, tq=tq),
        grid=(batch, nq),
        in_specs=[pl.BlockSpec((tq, MLA_H * LANES), lambda b, i: (b * nq + i, 0)),
                  pl.BlockSpec((seq, MLA_H * LANES), lambda b, i: (b, 0)),
                  pl.BlockSpec((nq, MLA_H * MLA_DV, tq), lambda b, i: (b, 0, 0))],
        out_specs=pl.BlockSpec((tq, MLA_H * MLA_DV), lambda b, i: (b * nq + i, 0)),
        out_shape=jax.ShapeDtypeStruct((batch * seq, MLA_H * MLA_DV), BF16),
        scratch_shapes=[pltpu.VMEM((MLA_H, 1, tq), F32), pltpu.VMEM((MLA_H, 1, tq), F32),
                        pltpu.VMEM((MLA_H, LANES, tq), F32)],
        compiler_params=_cparams(("parallel", "arbitrary")),
        name="mla_prompt",
    )(mq, mk, mvt)


def _mla_sample_kernel(pt_ref, q_ref, wkq_ref, wkt_ref, wv_ref, cn_ref, kn_ref, ckv_hbm, kpe_hbm,
                       o_ref, ckv_buf, kpe_buf, sem, *, n_pages, t_new, kpm):
    b = pl.program_id(0)
    nb = pl.num_programs(0)
    half = b % 2
    nrow = MLA_H * t_new

    def page_copies(seq, hf, p):
        page = pt_ref[seq * n_pages + p]
        return (pltpu.make_async_copy(ckv_hbm.at[page], ckv_buf.at[hf, p], sem.at[0, hf]),
                pltpu.make_async_copy(kpe_hbm.at[page], kpe_buf.at[hf, p], sem.at[1, hf]))

    def start_pages(seq, hf, pages):
        for p in pages:
            for cp in page_copies(seq, hf, p):
                cp.start()

    def wait_pages(seq, hf, pages):
        for p in pages:
            for cp in page_copies(seq, hf, p):
                cp.wait()

    @pl.when(b == 0)
    def _():
        start_pages(0, 0, range(n_pages))

    nxt = (b + 1) % nb

    wait_pages(b, half, range(n_pages))

    qa, qr = [], []
    for h in range(MLA_H):
        qh = q_ref[:, LANES * h:LANES * (h + 1)]
        qa.append(_dot(qh, wkq_ref[h]))
        qr.append(pltpu.roll(qh.astype(F32), LANES - _KPE_LANE, 1)[:, 0:MLA_ROPE])
    lq = jnp.concatenate([jnp.concatenate(qa, axis=0).astype(BF16), wkt_ref[...]], axis=0)
    qp = jnp.concatenate(qr, axis=0).astype(BF16)

    def scores(r, s_rope):
        rinv = []
        for h in range(MLA_H):
            kr = r[nrow + MLA_NOPE * h:nrow + MLA_NOPE * (h + 1)]
            ms = jnp.sum(kr * kr, axis=0, keepdims=True) * (1.0 / MLA_NOPE)
            rinv.append(jnp.broadcast_to(lax.rsqrt(ms + EPS), (t_new, r.shape[1])))
        return r[0:nrow] * jnp.concatenate(rinv, axis=0) + s_rope

    ngrp = n_pages // kpm
    ppb = kpm * SAMPLE_GROUPS
    xs, ss = [], []
    for g0 in range(0, ngrp, SAMPLE_GROUPS):
        gs = range(g0, min(g0 + SAMPLE_GROUPS, ngrp))
        xg =[jnp.concatenate([ckv_buf[half, g * kpm + j] for j in range(kpm)], axis=0).astype(BF16) for g in gs]
        rg = [_dot_nt(lq, x) for x in xg]
        pg = [_dot(qp, jnp.concatenate([kpe_buf[half, g * kpm + j] for j in range(kpm)], axis=1).astype(BF16))
              for g in gs]
        xs.append(jnp.concatenate(xg, axis=0))
        ss.append(jnp.concatenate([scores(r, s_rope) for r, s_rope in zip(rg, pg)], axis=1))
        start_pages(nxt, 1 - half, range(g0 * kpm, min(g0 * kpm + ppb, n_pages)))

    xn = cn_ref[...].astype(BF16)
    sn = scores(_dot_nt(lq, xn), _dot_nt(qp, kn_ref[...].astype(BF16)))
    tq = lax.broadcasted_iota(jnp.int32, (nrow, t_new), 0) % t_new
    ts = lax.broadcasted_iota(jnp.int32, (nrow, t_new), 1)
    xs.append(xn)
    ss.append(jnp.where(ts <= tq, sn, -jnp.inf))

    m = jnp.max(ss[0], axis=-1, keepdims=True)
    for s in ss[1:]:
        m = jnp.maximum(m, jnp.max(s, axis=-1, keepdims=True))
    ps = [jnp.exp2(s - m) for s in ss]
    l = jnp.sum(ps[0], axis=-1, keepdims=True)
    for p in ps[1:]:
        l = l + jnp.sum(p, axis=-1, keepdims=True)
    inv_l = 1.0 / l
    pvs = [_dot(p.astype(BF16), x) for p, x in zip(ps, xs)]
    oa = None
    for c in range(SAMPLE_PV_CHAINS):
        part = pvs[c::SAMPLE_PV_CHAINS]
        acc = part[0]
        for pv in part[1:]:
            acc = acc + pv
        oa = acc * inv_l if oa is None else oa + acc * inv_l

    r = _dot(oa.astype(BF16), wv_ref[...])
    lane = lax.broadcasted_iota(jnp.int32, (1, MLA_H * MLA_DV), 1)
    out = jnp.zeros((t_new, MLA_H * MLA_DV), F32)
    for h in range(MLA_H):
        out = out + jnp.where(lane // MLA_DV == h, r[t_new * h:t_new * (h + 1)], 0.0)
    o_ref[...] = out.astype(BF16)

    @pl.when(b == nb - 1)
    def _():
        wait_pages(nxt, 1 - half, range(n_pages))


def _mla_sample(mq, ckv_new, kp_new, pool_ckv, pool_kpe_t, page_table, w, batch, t_new, kpm):
    n_pages = page_table.shape[1]
    pt = page_table.reshape(-1)
    tok = lambda width: pl.BlockSpec((None, t_new, width), lambda b, pt_ref: (b, 0, 0))
    g3 = lambda a: a.reshape(batch, t_new, a.shape[-1])
    const = lambda a: pl.BlockSpec(a.shape, lambda b, pt_ref: (0,) * a.ndim)
    hbm = pl.BlockSpec(memory_space=pl.ANY)
    grid_spec = pltpu.PrefetchScalarGridSpec(
        num_scalar_prefetch=1,
        grid=(batch,),
        in_specs=[tok(MLA_H * LANES), const(w['w_kq']), const(w['w_kt']), const(w['w_v']),
                  tok(MLA_KVR), tok(MLA_ROPE), hbm, hbm],
        out_specs=tok(MLA_H * MLA_DV),
        scratch_shapes=[pltpu.VMEM((2, n_pages, PAGE, MLA_KVR), F32),
                        pltpu.VMEM((2, n_pages, MLA_ROPE, PAGE), F32),
                        pltpu.SemaphoreType.DMA((2, 2))],
    )
    om = pl.pallas_call(
        functools.partial(_mla_sample_kernel, n_pages=n_pages, t_new=t_new, kpm=kpm),
        grid_spec=grid_spec,
        out_shape=jax.ShapeDtypeStruct((batch, t_new, MLA_H * MLA_DV), BF16),
        compiler_params=_cparams(("arbitrary",)),
        name="mla_sample",
    )(pt, g3(mq), w['w_kq'], w['w_kt'], w['w_v'], g3(ckv_new), g3(kp_new), pool_ckv, pool_kpe_t)
    return om.reshape(batch * t_new, MLA_H * MLA_DV)


def _mix_out_kernel(x_ref, og_ref, om_ref, wo_ref, gx_ref, wxq_ref, gxq_ref, x1_o, xq_o):
    x1 = x_ref[...] + _dot(og_ref[...], wo_ref[0:512, :]) + _dot(om_ref[...], wo_ref[512:1024, :])
    x1_o[...] = x1
    qf = _dot(_rms(x1, gx_ref[...]).astype(BF16), wxq_ref[...])
    g = gxq_ref[...]
    for h in range(X_H):
        hs = slice(X_D * h, X_D * (h + 1))
        xq_o[:, hs] = _rms(qf[:, hs], g).astype(BF16)


def _mix_out(x, og, om, w, tm):
    n = x.shape[0]
    row = lambda width: pl.BlockSpec((tm, width), lambda i: (i, 0))
    consts = [w['w_out'], w['g_x'], w['w_xq'], w['g_xq']]
    return pl.pallas_call(
        _mix_out_kernel,
        grid=(n // tm,),
        in_specs=[row(D_MODEL), row(512), row(512)] + [_full(c.shape) for c in consts],
        out_specs=[row(D_MODEL), row(X_H * X_D)],
        out_shape=[jax.ShapeDtypeStruct((n, D_MODEL), F32), jax.ShapeDtypeStruct((n, X_H * X_D), BF16)],
        compiler_params=_cparams(("parallel",)),
        name="mix_out",
    )(x, og, om, *consts)


def _cross_kernel(xq_ref, mk_ref, mv_ref, o_ref, *, n_mem):
    gb = xq_ref.shape[0]

    def head(ref, g, h):
        if len(ref.shape) == 4:
            return ref[g, h]
        return ref[pl.ds(g * n_mem * X_H + h, n_mem, stride=X_H), :].astype(BF16)

    units = [(g, h) for g in range(gb) for h in range(X_H)]
    ss = [_dot_nt(xq_ref[g, :, X_D * h:X_D * (h + 1)], head(mk_ref, g, h)) for g, h in units]
    ps = [jnp.exp2(s - jnp.max(s, axis=-1, keepdims=True)) for s in ss]
    for (g, h), p in zip(units, ps):
        o = _dot(p.astype(BF16), head(mv_ref, g, h))
        o_ref[g, :, X_D * h:X_D * (h + 1)] = (o * (1.0 / jnp.sum(p, axis=-1, keepdims=True))).astype(BF16)


def _cross(xq, mk, mv, batch, seq, n_mem, tq, gb):
    nq = seq // tq
    assert gb == 1 or nq == 1
    row = pl.BlockSpec((gb, tq, X_H * X_D), lambda b, i: (b * nq + i, 0, 0))
    if mk.ndim == 4:
        mem = pl.BlockSpec((gb,) + mk.shape[1:], lambda b, i: (b, 0, 0, 0))
    else:
        mem = pl.BlockSpec((gb * n_mem * X_H, X_D), lambda b, i: (b, 0))
    o = pl.pallas_call(
        functools.partial(_cross_kernel, n_mem=n_mem),
        grid=(batch // gb, nq),
        in_specs=[row, mem, mem],
        out_specs=row,
        out_shape=jax.ShapeDtypeStruct((batch * nq, tq, X_H * X_D), BF16),
        compiler_params=_cparams(("parallel", "arbitrary")),
        name="cross",
    )(xq.reshape(batch * nq, tq, X_H * X_D), mk, mv)
    return o.reshape(batch * seq, X_H * X_D)


def _ffn_kernel(x_ref, o_ref, wxo_ref, g_ref, w1_ref, w2_ref, y_o, *, fc):
    x = x_ref[...] + _dot(o_ref[...], wxo_ref[...])
    h = _rms(x, g_ref[...]).astype(BF16)
    acc = x
    for c in range(D_FF // fc):
        u = jnp.maximum(_dot(h, w1_ref[:, fc * c:fc * (c + 1)]), 0.0)
        acc = acc + _dot((u * u).astype(BF16), w2_ref[fc * c:fc * (c + 1), :])
    y_o[...] = acc


def _ffn(x, o, w, tm, fc):
    n = x.shape[0]
    row = lambda width: pl.BlockSpec((tm, width), lambda i: (i, 0))
    consts = [w['w_xo'], w['g_ff'], w['w_ff1'], w['w_ff2']]
    return pl.pallas_call(
        functools.partial(_ffn_kernel, fc=fc),
        grid=(n // tm,),
        in_specs=[row(D_MODEL), row(X_H * X_D)] + [_full(c.shape) for c in consts],
        out_specs=row(D_MODEL),
        out_shape=jax.ShapeDtypeStruct((n, D_MODEL), F32),
        compiler_params=_cparams(("parallel",)),
        name="ffn",
    )(x, o, *consts)


def _mem_kv_kernel(m_ref, gm_ref, wk_ref, wv_ref, gk_ref, k_o, v_o, kh_o, vh_o, *, n_mem):
    tm = m_ref.shape[0]
    m = _rms(m_ref[...], gm_ref[...]).astype(BF16)
    kf = _dot(m, wk_ref[...])
    vf = _dot(m, wv_ref[...])
    g = gk_ref[...]
    for h in range(X_H):
        hs = slice(X_D * h, X_D * (h + 1))
        kn = _rms(kf[:, hs], g)
        k_o[pl.ds(h, tm, stride=X_H), :] = kn
        v_o[pl.ds(h, tm, stride=X_H), :] = vf[:, hs]
        for b in range(kh_o.shape[0]):
            rs = slice(n_mem * b, n_mem * (b + 1))
            kh_o[b, h] = kn[rs].astype(BF16)
            vh_o[b, h] = vf[rs, hs].astype(BF16)


def _mem_kv(mem, w, n_mem, tm):
    n = mem.shape[0]
    bt = tm // n_mem
    row = lambda width: pl.BlockSpec((tm, width), lambda i: (i, 0))
    head = pl.BlockSpec((bt, X_H, n_mem, X_D), lambda i: (i, 0, 0, 0))
    consts = [w['g_mem'], w['w_xk'], w['w_xv'], w['g_xk']]
    return pl.pallas_call(
        functools.partial(_mem_kv_kernel, n_mem=n_mem),
        grid=(n // tm,),
        in_specs=[row(D_MODEL)] + [_full(c.shape) for c in consts],
        out_specs=[pl.BlockSpec((tm * X_H, X_D), lambda i: (i, 0))] * 2 + [head, head],
        out_shape=[jax.ShapeDtypeStruct((n * X_H, X_D), F32)] * 2
                  + [jax.ShapeDtypeStruct((n // n_mem, X_H, n_mem, X_D), BF16)] * 2,
        compiler_params=_cparams(("parallel",)),
        name="mem_kv",
    )(mem, *consts)


def _prep_weights(g_mix, w_in, w_gla_a2, b_gla_a, g_gla_o, g_mla_qa, w_mla_qb, g_mla_kva, w_mla_kvb,
                  g_q_nope, g_k_nope, g_q_rope, g_k_rope, w_out, g_x, g_mem, w_xq, w_xk, w_xv, g_xq, g_xk,
                  w_xo, g_ff, w_ff1, w_ff2):
    rowv = lambda g: g.reshape(1, -1).astype(F32)
    zc = lambda n: jnp.zeros((D_MODEL, n), F32)
    sizes = (256, 256, 512, 512, GLA_RANK, MLA_QR, MLA_KVR, MLA_ROPE)
    offs = [0]
    for s in sizes:
        offs.append(offs[-1] + s)
    q, k, v, r, a, qa, kva, kpe = [w_in[:, offs[i]:offs[i + 1]] for i in range(8)]
    w_in_p = jnp.concatenate(
        [q * (GLA_K ** -0.5), k, v, r, qa,
         a, zc(_KPE_LANE - GLA_RANK), kpe, zc(LANES - _KPE_LANE - MLA_ROPE), kva], axis=1)
    w_a2 = jnp.concatenate([w_gla_a2, jnp.zeros((LANES - GLA_RANK, GLA_H * GLA_K), F32)], axis=0)

    qb = w_mla_qb.reshape(MLA_QR, MLA_H, MLA_NOPE + MLA_ROPE)
    qb = jnp.pad(qb, ((0, 0), (0, 0), (0, LANES - MLA_NOPE - MLA_ROPE))).reshape(MLA_QR, MLA_H * LANES)
    kvb = w_mla_kvb.reshape(MLA_KVR, MLA_H, MLA_NOPE + MLA_DV)
    wk = kvb[:, :, :MLA_NOPE]
    wk_p = jnp.pad(wk, ((0, 0), (0, 0), (0, LANES - MLA_NOPE))).reshape(MLA_KVR, MLA_H * LANES)
    wv = kvb[:, :, MLA_NOPE:].reshape(MLA_KVR, MLA_H * MLA_DV)
    z32 = jnp.zeros((LANES - MLA_NOPE - MLA_ROPE,), F32)
    gq_row = jnp.tile(jnp.concatenate([g_q_nope, g_q_rope, z32]), MLA_H) * (MLA_SCALE * LOG2E)
    gk_row = jnp.tile(jnp.concatenate([g_k_nope, jnp.zeros((LANES - MLA_NOPE,), F32)]), MLA_H)
    g_kpe = jnp.concatenate([jnp.zeros((_KPE_LANE,), F32), g_k_rope, z32])

    li = jnp.arange(2 * LANES)
    seg = jnp.where(li % LANES < MLA_NOPE, 0, jnp.where(li % LANES < MLA_NOPE + MLA_ROPE, 1, 2))
    same = (li[:, None] // LANES == li[None, :] // LANES) & (seg[:, None] == seg[None, :]) & (seg[:, None] < 2)
    bd = jnp.where(same, jnp.where(seg[:, None] == 0, 1.0 / MLA_NOPE, 1.0 / MLA_ROPE), 0.0)

    wkq = jnp.transpose(wk, (1, 2, 0)) * g_k_nope[None, :, None]
    wkq = jnp.pad(wkq, ((0, 0), (0, LANES - MLA_NOPE), (0, 0)))
    wkt = jnp.transpose(wk, (1, 2, 0)).reshape(MLA_H * MLA_NOPE, MLA_KVR)

    hk = jnp.arange(GLA_H * GLA_K) // GLA_K
    hv = jnp.arange(GLA_H * GLA_V) // GLA_V
    bexp = (hk[:, None] == hv[None, :])
    return {
        'g_mix': rowv(g_mix), 'w_in': w_in_p.astype(BF16), 'w_a2': w_a2.astype(BF16), 'b_a': rowv(b_gla_a),
        'g_qa': rowv(g_mla_qa), 'w_qb': qb.astype(BF16), 'gq_row': rowv(gq_row), 'g_kva': rowv(g_mla_kva),
        'w_kv': jnp.concatenate([wk_p, wv], axis=1).astype(BF16), 'gk_row': rowv(gk_row),
        'w_v': wv.astype(BF16), 'g_kpe': rowv(g_kpe),
        'bd': bd.astype(BF16), 'w_kq': wkq.astype(BF16), 'w_kt': wkt.astype(BF16),
        'g_gla_o': rowv(g_gla_o), 'bexp': bexp.astype(BF16),
        'w_out': w_out.astype(BF16), 'g_x': rowv(g_x), 'w_xq': w_xq.astype(BF16),
        'g_xq': rowv(g_xq) * (X_SCALE * LOG2E),
        'g_mem': rowv(g_mem), 'w_xk': w_xk.astype(BF16), 'w_xv': w_xv.astype(BF16), 'g_xk': rowv(g_xk),
        'w_xo': w_xo.astype(BF16), 'g_ff': rowv(g_ff), 'w_ff1': w_ff1.astype(BF16), 'w_ff2': w_ff2.astype(BF16),
    }


def _rope_tables(pos):
    half = MLA_ROPE // 2
    inv = ROPE_THETA ** (-jnp.arange(half, dtype=F32) / half)
    ang = pos.astype(F32)[:, None] * inv[None, :]
    cos, sin = jnp.cos(ang), jnp.sin(ang)
    n = pos.shape[0]
    one = jnp.ones((n, _KPE_LANE), F32)
    z = lambda w_: jnp.zeros((n, w_), F32)
    tail = LANES - _KPE_LANE - MLA_ROPE
    c = jnp.concatenate([one, cos, cos, jnp.ones((n, tail), F32)], axis=1)
    s_up = jnp.concatenate([z(_KPE_LANE), -sin, z(half), z(tail)], axis=1)
    s_dn = jnp.concatenate([z(_KPE_LANE), z(half), sin, z(tail)], axis=1)
    return c, s_up, s_dn


def _tile_rows(n, cap):
    t = min(n, cap)
    while n % t:
        t //= 2
    return t


def _tail(x1, xq, mk, mv, w, batch, seq, n_mem):
    tq = _tile_rows(seq, 512)
    gb = _tile_rows(batch, max(1, CROSS_ROWS // seq)) if tq == seq else 1
    o = _cross(xq, mk, mv, batch, seq, n_mem, tq, gb)
    return _ffn(x1, o, w, _tile_rows(x1.shape[0], 512), 1024)


def _prompt_layer(x, mem, w):
    batch, seq, _ = x.shape
    n = batch * seq
    xf = x.reshape(n, D_MODEL)
    tm = _tile_rows(seq, 512)
    tabs = _rope_tables(jnp.arange(seq))
    gq, gk, gv, gr, la, mq, ckv, kp, mk_, mvt = _in_proj(xf, w, tabs, tm, True)
    c_len = GLA_CHUNK
    cps = _tile_rows(seq // c_len, GLA_CHUNKS_PER_STEP)
    s0 = jnp.zeros((batch, GLA_H * GLA_K, GLA_V), F32)
    og, s_fin = _gla(gq, gk, gv, gr, la, s0, w, batch, seq, c_len, min(GLA_SUB, c_len), cps)
    om = _mla_prompt(mq, mk_, mvt, batch, seq, ATT_TILE)
    x1, xq = _mix_out(xf, og, om, w, tm)
    n_mem = mem.shape[1]
    memf = mem.reshape(-1, D_MODEL)
    xk, xv, xkh, xvh = _mem_kv(memf, w, n_mem, max(n_mem, _tile_rows(memf.shape[0], 512)))
    y = _tail(x1, xq, xkh, xvh, w, batch, seq, n_mem)
    return y, ckv, kp, xk, xv, s_fin


def _sample_layer(x, pool_ckv, pool_kpe, page_table, mem_k, mem_v, s_prev, w):
    batch, seq, _ = x.shape
    n = batch * seq
    xf = x.reshape(n, D_MODEL)
    tm = _tile_rows(n, 512)
    past = page_table.shape[1] * PAGE
    pos = past + (jnp.arange(tm) % seq)
    gq, gk, gv, gr, la, mq, ckv, kp = _in_proj(xf, w, _rope_tables(pos), tm, False)
    s0 = s_prev.reshape(batch, GLA_H * GLA_K, GLA_V)
    og, s_fin = _gla(gq, gk, gv, gr, la, s0, w, batch, seq, seq, seq, 1)
    om = _mla_sample(mq, ckv, kp, pool_ckv, jnp.swapaxes(pool_kpe, 1, 2), page_table, w, batch, seq,
                     _tile_rows(page_table.shape[1], SAMPLE_PAGES_PER_DOT))
    x1, xq = _mix_out(xf, og, om, w, tm)
    y = _tail(x1, xq, mem_k.reshape(-1, X_D), mem_v.reshape(-1, X_D), w, batch, seq, mem_k.shape[1])
    return y, ckv, kp, s_fin


def kernel(x_prompt, x_sample, mem_prompt, cache_ckv, cache_kpe, cache_mem_k, cache_mem_v, state_gla, page_table, g_mix, w_in, w_gla_a2, b_gla_a, g_gla_o, g_mla_qa, w_mla_qb, g_mla_kva, w_mla_kvb, g_q_nope, g_k_nope, g_q_rope, g_k_rope, w_out, g_x, g_mem, w_xq, w_xk, w_xv, g_xq, g_xk, w_xo, g_ff, w_ff1, w_ff2):
    depth = w_in.shape[0]
    assert depth == 1, "one layer: prompt-group caches of layer l would feed layer l+1 otherwise unchanged"
    params = (g_mix, w_in, w_gla_a2, b_gla_a, g_gla_o, g_mla_qa, w_mla_qb, g_mla_kva, w_mla_kvb,
              g_q_nope, g_k_nope, g_q_rope, g_k_rope, w_out, g_x, g_mem, w_xq, w_xk, w_xv, g_xq, g_xk,
              w_xo, g_ff, w_ff1, w_ff2)
    w = _prep_weights(*[p[0] for p in params])
    bp, tp, _ = x_prompt.shape
    bs, tsq, _ = x_sample.shape
    yp, ckv_p, kp_p, xk, xv, gla_p = _prompt_layer(x_prompt, mem_prompt, w)
    ys, ckv_s, kp_s, gla_s = _sample_layer(x_sample, cache_ckv.reshape(cache_ckv.shape[1:]), cache_kpe.reshape(cache_kpe.shape[1:]), page_table,
                                            cache_mem_k[0], cache_mem_v[0], state_gla[0], w)
    n_mem = mem_prompt.shape[1]
    return (yp.reshape(bp, tp, D_MODEL), ys.reshape(bs, tsq, D_MODEL),
            ckv_p.reshape(1, bp, tp, MLA_KVR), kp_p.reshape(1, bp, tp, MLA_ROPE),
            xk.reshape(1, bp, n_mem, X_H, X_D), xv.reshape(1, bp, n_mem, X_H, X_D),
            gla_p.reshape(1, bp, GLA_H, GLA_K, GLA_V),
            ckv_s.reshape(1, bs, tsq, MLA_KVR), kp_s.reshape(1, bs, tsq, MLA_ROPE),
            gla_s.reshape(1, bs, GLA_H, GLA_K, GLA_V))
```

```python
import functools

import jax
import jax.numpy as jnp
from jax import lax
from jax.experimental import pallas as pl
from jax.experimental.pallas import tpu as pltpu

F32 = jnp.float32
BF16 = jnp.bfloat16

EPS = 1e-6
D_MODEL = 1024
GLA_H, GLA_K, GLA_V, GLA_RANK, GLA_TAU, GLA_CHUNK = 4, 64, 128, 16, 16.0, 64
GLA_SUB = 16
GLA_CHUNKS_PER_STEP = 4
GLA_SEQS_PER_STEP = 4
GLA_SHORT_SEQS_PER_STEP = 8
GLA_FAST_MAX = 40.0
MLA_H, MLA_DV, MLA_NOPE, MLA_ROPE, MLA_QR, MLA_KVR = 8, 64, 64, 32, 384, 256
MLA_SCALE = (MLA_NOPE + MLA_ROPE) ** -0.5
ROPE_THETA = 10000.0
X_H, X_D = 4, 128
X_SCALE = X_D ** -0.5
D_FF = 4096
PAGE = 128
LANES = 128
NEG = -1e30
LOG2E = 1.4426950408889634
ATT_TILE = 256
ATT_GROUP = 2
CROSS_ROWS = 32
SAMPLE_PAGES_PER_DOT = 8
SAMPLE_PV_CHAINS = 2
SAMPLE_GROUPS = 4

_C_Q, _C_K, _C_V, _C_R = 0, 256, 512, 1024
_C_QA, _C_AK, _C_KVA, _C_END = 1536, 1920, 2048, 2304
_KPE_LANE = 64

VMEM_LIMIT = 56 * 1024 * 1024


def _cparams(sem):
    return pltpu.CompilerParams(dimension_semantics=sem, vmem_limit_bytes=VMEM_LIMIT)


def _dot(a, b):
    return jnp.dot(a, b, preferred_element_type=F32)


def _dot_nt(a, b):
    return lax.dot_general(a, b, (((1,), (1,)), ((), ())), preferred_element_type=F32)


def _rms(x, g):
    return x * lax.rsqrt(jnp.mean(x * x, axis=-1, keepdims=True) + EPS) * g


def _full(shape):
    n = len(shape)
    return pl.BlockSpec(shape, lambda *_: (0,) * n)


def _in_proj_kernel(x_ref, gmix_ref, win_ref, wa2_ref, ba_ref, gqa_ref, wqb_ref, gqrow_ref,
                    gkva_ref, wkv_ref, gkrow_ref, gkpe_ref, bd_ref, c_ref, s1_ref, s2_ref,
                    gq_o, gk_o, gv_o, gr_o, la_o, mq_o, ckv_o, kp_o, *kv_o):
    h = _rms(x_ref[...], gmix_ref[...]).astype(BF16)

    z2 = _dot(h, win_ref[:, _C_QA:_C_END])
    qa = z2[:, 0:_C_AK - _C_QA]
    zak = z2[:, _C_AK - _C_QA:_C_KVA - _C_QA]
    kva = z2[:, _C_KVA - _C_QA:]

    gate = _dot(zak.astype(BF16), wa2_ref[...]) + ba_ref[...]
    la_o[...] = (jnp.minimum(gate, 0.0) - jnp.log1p(jnp.exp(-jnp.abs(gate)))) * (1.0 / GLA_TAU)

    cos, s_up, s_dn = c_ref[...], s1_ref[...], s2_ref[...]

    def rope(t):
        return t * cos + pltpu.roll(t, LANES - 16, 1) * s_up + pltpu.roll(t, 16, 1) * s_dn

    lane = lax.broadcasted_iota(jnp.int32, (1, LANES), 1)
    zk = jnp.where((lane >= _KPE_LANE) & (lane < _KPE_LANE + MLA_ROPE), zak, 0.0)
    kpn = zk * lax.rsqrt(jnp.sum(zk * zk, axis=-1, keepdims=True) * (1.0 / MLA_ROPE) + EPS) * gkpe_ref[...]
    kpt = rope(kpn)
    kp_o[...] = pltpu.roll(kpt, LANES - _KPE_LANE, 1)[:, 0:MLA_ROPE]

    ckv = _rms(kva, gkva_ref[...])
    ckv_o[...] = ckv
    cb = ckv.astype(BF16)
    bd = bd_ref[...]

    def seg_norm(t, grow):
        ms = _dot((t * t).astype(BF16), bd)
        return t * lax.rsqrt(ms + EPS) * grow

    qf = _dot(_rms(qa, gqa_ref[...]).astype(BF16), wqb_ref[...])
    for c in range(4):
        sl = slice(2 * LANES * c, 2 * LANES * (c + 1))
        qn = seg_norm(qf[:, sl], gqrow_ref[:, sl])
        for j in range(2):
            o = 2 * LANES * c + LANES * j
            mq_o[:, o:o + LANES] = rope(qn[:, LANES * j:LANES * (j + 1)]).astype(BF16)

    if kv_o:
        mk_o, mvt_o = kv_o
        kvf = _dot(cb, wkv_ref[...])
        kf, mv = kvf[:, 0:MLA_H * LANES], kvf[:, MLA_H * LANES:]
        for c in range(mvt_o.shape[0]):
            mvt_o[c] = mv[ATT_TILE * c:ATT_TILE * (c + 1), :].T.astype(BF16)
        for c in range(4):
            sl = slice(2 * LANES * c, 2 * LANES * (c + 1))
            kn = seg_norm(kf[:, sl], gkrow_ref[:, sl])
            for j in range(2):
                o = 2 * LANES * c + LANES * j
                mk_o[:, o:o + LANES] = (kn[:, LANES * j:LANES * (j + 1)] + kpt).astype(BF16)

    z1 = _dot(h, win_ref[:, _C_Q:_C_QA])
    gq_o[...] = z1[:, _C_Q:_C_K].astype(BF16)
    gk_o[...] = z1[:, _C_K:_C_V].astype(BF16)
    gv_o[...] = z1[:, _C_V:_C_R].astype(BF16)
    gr_o[...] = z1[:, _C_R:_C_QA].astype(BF16)


def _in_proj(x, w, tabs, tm, with_kv):
    n = x.shape[0]
    nt = tabs[0].shape[0] // tm
    row = lambda width: pl.BlockSpec((tm, width), lambda i: (i, 0))
    tab = pl.BlockSpec((tm, LANES), lambda i: (i % nt, 0))
    consts = [w['g_mix'], w['w_in'], w['w_a2'], w['b_a'], w['g_qa'], w['w_qb'], w['gq_row'],
              w['g_kva'], w['w_kv'], w['gk_row'], w['g_kpe'], w['bd']]
    out_w = [(256, BF16), (256, BF16), (512, BF16), (512, BF16), (256, F32),
             (1024, BF16), (256, F32), (MLA_ROPE, F32)] + ([(1024, BF16)] if with_kv else [])
    out_specs = [row(wd) for wd, _ in out_w]
    out_shape = [jax.ShapeDtypeStruct((n, wd), dt) for wd, dt in out_w]
    if with_kv:
        out_specs.append(pl.BlockSpec((tm // ATT_TILE, MLA_H * MLA_DV, ATT_TILE), lambda i: (i, 0, 0)))
        out_shape.append(jax.ShapeDtypeStruct((n // ATT_TILE, MLA_H * MLA_DV, ATT_TILE), BF16))
    return pl.pallas_call(
        _in_proj_kernel,
        grid=(n // tm,),
        in_specs=[row(D_MODEL)] + [_full(c.shape) for c in consts] + [tab, tab, tab],
        out_specs=out_specs,
        out_shape=out_shape,
        compiler_params=_cparams(("parallel",)),
        name="in_proj",
    )(x, *consts, *tabs)


def _head_masks():
    lane = lax.broadcasted_iota(jnp.int32, (1, GLA_H * GLA_K), 1)
    return [(lane // GLA_K == h).astype(F32) for h in range(GLA_H)]


def _gla_fast_step(qs, ks, vs, bs, sts, keep, c_len, cps):
    masks = _head_masks()
    ts = c_len * cps
    n = len(qs)
    ebs = [jnp.exp(b) for b in bs]
    kts = [k * jnp.exp(-b) for k, b in zip(ks, bs)]
    lqs = [jnp.concatenate([q * eb * m for m in masks], axis=0).astype(BF16)
           for q, eb in zip(qs, ebs)]
    atts = [_dot_nt(lq, kt.astype(BF16)) for lq, kt in zip(lqs, kts)]
    vbs = [v.astype(BF16) for v in vs]
    o_intra = [[_dot((atts[g][ts * h:ts * (h + 1)] * keep).astype(BF16), vbs[g][:, GLA_V * h:GLA_V * (h + 1)])
                for h in range(GLA_H)] for g in range(n)]
    o_inter = [[] for _ in range(n)]
    sts = list(sts)
    for c in range(cps):
        cs = slice(c_len * c, c_len * (c + 1))
        for g in range(n):
            lqc = jnp.concatenate([lqs[g][ts * h + c_len * c:ts * h + c_len * (c + 1)] for h in range(GLA_H)],
                                  axis=0)
            o_inter[g].append(_dot_nt(lqc, sts[g].astype(BF16)))
            ebl = ebs[g][c_len * (c + 1) - 1:c_len * (c + 1), :]
            kl = kts[g][cs] * ebl
            vst = jnp.concatenate([vs[g][cs, GLA_V * h:GLA_V * (h + 1)] for h in range(GLA_H)], axis=0)
            kst = jnp.concatenate([kl * m for m in masks], axis=0).astype(BF16)
            sts[g] = sts[g] * ebl + _dot(vst.T.astype(BF16), kst)
    outs = [[o_intra[g][h] + jnp.concatenate([o_inter[g][c][c_len * h:c_len * (h + 1)] for c in range(cps)],
                                             axis=0) for h in range(GLA_H)] for g in range(n)]
    return outs, sts


def _gla_chunk(q, k, v, b, st, bexp, c_len, sub):
    masks = _head_masks()
    bl = b[c_len - 1:c_len, :]

    qh = q * jnp.exp(b)
    lq = jnp.concatenate([qh * m for m in masks], axis=0).astype(BF16)
    o_inter = _dot_nt(lq, st.astype(BF16))

    nsub = c_len // sub
    a_rows = [[] for _ in range(GLA_H)]
    col = lax.broadcasted_iota(jnp.int32, (1, c_len), 1)
    for i in range(1, nsub):
        ref = b[sub * i - 1:sub * i, :]
        qi = q[sub * i:sub * (i + 1)] * jnp.exp(b[sub * i:sub * (i + 1)] - ref)
        ki = k * jnp.exp(jnp.minimum(ref - b, 0.0))
        li = jnp.concatenate([qi * m for m in masks], axis=0).astype(BF16)
        ai = jnp.where(col < sub * i, _dot_nt(li, ki.astype(BF16)), 0.0)
        for h in range(GLA_H):
            a_rows[h].append(ai[sub * h:sub * (h + 1)])

    row = lax.broadcasted_iota(jnp.int32, (sub, 1), 0)
    o_diag = []
    for i in range(nsub):
        sl = slice(sub * i, sub * (i + 1))
        bb, qb, kb, vb = b[sl], q[sl], k[sl], v[sl]
        ps = []
        for s in range(sub):
            e = jnp.exp(jnp.where(row >= s, bb - bb[s:s + 1, :], NEG))
            ps.append(qb * e * kb[s:s + 1, :])
        r = _dot(jnp.concatenate(ps, axis=0).astype(BF16), bexp)
        od = r[0:sub] * vb[0:1, :]
        for s in range(1, sub):
            od = od + r[sub * s:sub * (s + 1)] * vb[s:s + 1, :]
        o_diag.append(od)
    o_diag = jnp.concatenate(o_diag, axis=0) if nsub > 1 else o_diag[0]

    outs = []
    for h in range(GLA_H):
        oh = o_inter[c_len * h:c_len * (h + 1)] + o_diag[:, GLA_V * h:GLA_V * (h + 1)]
        if nsub > 1:
            ah = jnp.concatenate([jnp.zeros((sub, c_len), F32)] + a_rows[h], axis=0)
            oh = oh + _dot(ah.astype(BF16), v[:, GLA_V * h:GLA_V * (h + 1)].astype(BF16))
        outs.append(oh)

    kl = k * jnp.exp(bl - b)
    vs = jnp.concatenate([v[:, GLA_V * h:GLA_V * (h + 1)] for h in range(GLA_H)], axis=0)
    ks = jnp.concatenate([kl * m for m in masks], axis=0).astype(BF16)
    st_new = st * jnp.exp(bl) + _dot(vs.T.astype(BF16), ks)
    return outs, st_new


def _gla_kernel(*refs, c_len, sub, cps, fast):
    if fast:
        (q_ref, k_ref, v_ref, r_ref, la_ref, s0_ref, g_ref, bexp_ref, tri_ref, keep_ref,
         og_o, sf_o, st_ref, b_ref) = refs
    else:
        q_ref, k_ref, v_ref, r_ref, la_ref, s0_ref, g_ref, bexp_ref, og_o, sf_o, st_ref, b_ref = refs
    t = pl.program_id(1)
    gb = q_ref.shape[0]

    @pl.when(t == 0)
    def _():
        for i in range(gb):
            st_ref[i] = s0_ref[i].T

    g = g_ref[...]

    def emit(i, outs, rs):
        rr = r_ref[i, rs, :].astype(F32)
        for h in range(GLA_H):
            hs = slice(GLA_V * h, GLA_V * (h + 1))
            rh = rr[:, hs]
            og_o[i, rs, hs] = (_rms(outs[h], g) * (rh / (1.0 + jnp.exp(-rh)))).astype(BF16)

    bs = []
    for i in range(gb):
        la = la_ref[i]
        if fast:
            la_hi = la.astype(BF16)
            la_lo = (la - la_hi.astype(F32)).astype(BF16)
            b = _dot(tri_ref[...], la_hi) + _dot(tri_ref[...], la_lo)
        else:
            rows = lax.broadcasted_iota(jnp.int32, (c_len, 1), 0)
            b = jnp.zeros_like(la)
            for j in range(c_len):
                b = b + jnp.where(rows >= j, la[j:j + 1, :], 0.0)
        b_ref[i] = b
        bs.append(b)

    def robust():
        bexp = bexp_ref[...]
        for i in range(gb):
            def body(c, carry, i=i):
                rs = pl.ds(pl.multiple_of(c * c_len, c_len), c_len)
                outs, st_new = _gla_chunk(q_ref[i, rs, :].astype(F32), k_ref[i, rs, :].astype(F32),
                                          v_ref[i, rs, :].astype(F32), b_ref[i, rs, :], st_ref[i], bexp, c_len, sub)
                st_ref[i] = st_new
                emit(i, outs, rs)
                return carry

            lax.fori_loop(0, cps, body, 0)

    if fast:
        low = jnp.min(bs[0])
        for b in bs[1:]:
            low = jnp.minimum(low, jnp.min(b))
        mild = low >= -GLA_FAST_MAX

        @pl.when(mild)
        def _():
            seqs = range(gb)
            outs, sts = _gla_fast_step([q_ref[i].astype(F32) for i in seqs], [k_ref[i].astype(F32) for i in seqs],
                                       [v_ref[i].astype(F32) for i in seqs], [b_ref[i] for i in seqs],
                                       [st_ref[i] for i in seqs], keep_ref[...], c_len, cps)
            for i in seqs:
                st_ref[i] = sts[i]
                emit(i, outs[i], slice(None))

        pl.when(jnp.logical_not(mild))(robust)
    else:
        robust()

    @pl.when(t == pl.num_programs(1) - 1)
    def _():
        for i in range(gb):
            sf_o[i] = st_ref[i].T


def _gla(gq, gk, gv, gr, la, s0, w, batch, seq, c_len, sub, cps, gb):
    ts = c_len * cps
    nt = seq // ts
    fast = c_len == GLA_CHUNK
    assert fast or cps == 1
    row = lambda width: pl.BlockSpec((gb, None, ts, width), lambda b, t: (b, t, 0, 0))
    g4 = lambda a: a.reshape(batch, nt, ts, a.shape[-1])
    st_spec = pl.BlockSpec((gb, GLA_H * GLA_K, GLA_V), lambda b, t: (b, 0, 0))
    consts = [w['g_gla_o'], w['bexp']]
    if fast:
        ti = jnp.arange(ts)
        keep = (ti[:, None] // c_len == ti[None, :] // c_len) & (ti[None, :] <= ti[:, None])
        consts += [keep.astype(BF16), keep.astype(F32)]
    og, s_fin = pl.pallas_call(
        functools.partial(_gla_kernel, c_len=c_len, sub=sub, cps=cps, fast=fast),
        grid=(batch // gb, nt),
        in_specs=[row(256), row(256), row(512), row(512), row(256), st_spec] + [_full(c.shape) for c in consts],
        out_specs=[row(512), st_spec],
        out_shape=[jax.ShapeDtypeStruct((batch, nt, ts, GLA_H * GLA_V), BF16),
                   jax.ShapeDtypeStruct((batch, GLA_H * GLA_K, GLA_V), F32)],
        scratch_shapes=[pltpu.VMEM((gb, GLA_V, GLA_H * GLA_K), F32), pltpu.VMEM((gb, ts, GLA_H * GLA_K), F32)],
        compiler_params=_cparams(("parallel", "arbitrary")),
        name="gla",
    )(g4(gq), g4(gk), g4(gv), g4(gr), g4(la), s0, *consts)
    return og.reshape(batch * seq, GLA_H * GLA_V), s_fin


def _mla_prompt_kernel(q_ref, k_ref, vt_ref, o_ref, m_ref, l_ref, acc_ref, *, tq):
    i = pl.program_id(1)
    m_ref[...] = jnp.full(m_ref.shape, -jnp.inf, F32)
    l_ref[...] = jnp.zeros(l_ref.shape, F32)
    acc_ref[...] = jnp.zeros(acc_ref.shape, F32)
    keep = (lax.broadcasted_iota(jnp.int32, (tq, tq), 0) <= lax.broadcasted_iota(jnp.int32, (tq, tq), 1))

    def tiles(js, last_masked):
        n = len(js)
        sts = [[_dot_nt(k_ref[pl.ds(pl.multiple_of(j * tq, tq), tq), LANES * h:LANES * (h + 1)],
                        q_ref[:, LANES * h:LANES * (h + 1)]) for j in js] for h in range(MLA_H)]
        ps, alphas = [], []
        for h in range(MLA_H):
            if last_masked:
                sts[h][-1] = jnp.where(keep, sts[h][-1], -jnp.inf)
            m_old = m_ref[h]
            m_new = m_old
            for st in sts[h]:
                m_new = jnp.maximum(m_new, jnp.max(st, axis=0, keepdims=True))
            alpha = jnp.exp2(m_old - m_new)
            ph = [jnp.exp2(st - m_new) for st in sts[h]]
            m_ref[h] = m_new
            l_new = alpha * l_ref[h]
            for p in ph:
                l_new = l_new + jnp.sum(p, axis=0, keepdims=True)
            l_ref[h] = l_new
            ps.append([p.astype(BF16) for p in ph])
            alphas.append(alpha)
        for h in range(MLA_H):
            acc = alphas[h] * acc_ref[h]
            for t in range(n):
                acc = acc + _dot(vt_ref[js[t], LANES * (h // 2):LANES * (h // 2 + 1), :], ps[h][t])
            acc_ref[h] = acc

    def body(jj, carry):
        tiles([ATT_GROUP * jj + t for t in range(ATT_GROUP)], False)
        return carry

    lax.fori_loop(0, i // ATT_GROUP, body, 0)
    for rem in range(ATT_GROUP):
        pl.when(i % ATT_GROUP == rem)(functools.partial(tiles, [i - rem + t for t in range(rem + 1)], True))

    row = lax.broadcasted_iota(jnp.int32, (LANES, 1), 0)
    for pair in range(MLA_H // 2):
        h0, h1 = 2 * pair, 2 * pair + 1
        a = acc_ref[h0] * (1.0 / l_ref[h0])
        b = acc_ref[h1] * (1.0 / l_ref[h1])
        o_ref[:, LANES * pair:LANES * (pair + 1)] = jnp.where(row < MLA_DV, a, b).T.astype(BF16)


def _mla_prompt(mq, mk, mvt, batch, seq, tq):
    nq = seq // tq
    return pl.pallas_call(
        functools.partial(_mla_prompt_kernel, tq=tq),
        grid=(batch, nq),
        in_specs=[pl.BlockSpec((tq, MLA_H * LANES), lambda b, i: (b * nq + i, 0)),
                  pl.BlockSpec((seq, MLA_H * LANES), lambda b, i: (b, 0)),
                  pl.BlockSpec((nq, MLA_H * MLA_DV, tq), lambda b, i: (b, 0, 0))],
        out_specs=pl.BlockSpec((tq, MLA_H * MLA_DV), lambda b, i: (b * nq + i, 0)),
        out_shape=jax.ShapeDtypeStruct((batch * seq, MLA_H * MLA_DV), BF16),
        scratch_shapes=[pltpu.VMEM((MLA_H, 1, tq), F32), pltpu.VMEM((MLA_H, 1, tq), F32),
                        pltpu.VMEM((MLA_H, LANES, tq), F32)],
        compiler_params=_cparams(("parallel", "arbitrary")),
        name="mla_prompt",
    )(mq, mk, mvt)


def _mla_sample_kernel(pt_ref, q_ref, wkq_ref, wkt_ref, wv_ref, cn_ref, kn_ref, ckv_hbm, kpe_hbm,
                       o_ref, ckv_buf, kpe_buf, sem, *, n_pages, t_new, kpm):
    b = pl.program_id(0)
    nb = pl.num_programs(0)
    half = b % 2
    nrow = MLA_H * t_new

    def page_copies(seq, hf, p):
        page = pt_ref[seq * n_pages + p]
        return (pltpu.make_async_copy(ckv_hbm.at[page], ckv_buf.at[hf, p], sem.at[0, hf]),
                pltpu.make_async_copy(kpe_hbm.at[page], kpe_buf.at[hf, p], sem.at[1, hf]))

    def start_pages(seq, hf, pages):
        for p in pages:
            for cp in page_copies(seq, hf, p):
                cp.start()

    def wait_pages(seq, hf, pages):
        for p in pages:
            for cp in page_copies(seq, hf, p):
                cp.wait()

    @pl.when(b == 0)
    def _():
        start_pages(0, 0, range(n_pages))

    nxt = (b + 1) % nb

    wait_pages(b, half, range(n_pages))

    qa, qr = [], []
    for h in range(MLA_H):
        qh = q_ref[:, LANES * h:LANES * (h + 1)]
        qa.append(_dot(qh, wkq_ref[h]))
        qr.append(pltpu.roll(qh.astype(F32), LANES - _KPE_LANE, 1)[:, 0:MLA_ROPE])
    lq = jnp.concatenate([jnp.concatenate(qa, axis=0).astype(BF16), wkt_ref[...]], axis=0)
    qp = jnp.concatenate(qr, axis=0).astype(BF16)

    def scores(r, s_rope):
        rinv = []
        for h in range(MLA_H):
            kr = r[nrow + MLA_NOPE * h:nrow + MLA_NOPE * (h + 1)]
            ms = jnp.sum(kr * kr, axis=0, keepdims=True) * (1.0 / MLA_NOPE)
            rinv.append(jnp.broadcast_to(lax.rsqrt(ms + EPS), (t_new, r.shape[1])))
        return r[0:nrow] * jnp.concatenate(rinv, axis=0) + s_rope

    ngrp = n_pages // kpm
    ppb = kpm * SAMPLE_GROUPS
    xs, ss = [], []
    for g0 in range(0, ngrp, SAMPLE_GROUPS):
        gs = range(g0, min(g0 + SAMPLE_GROUPS, ngrp))
        xg = [jnp.concatenate([ckv_buf[half, g * kpm + j] for j in range(kpm)], axis=0).astype(BF16) for g in gs]
        rg = [_dot_nt(lq, x) for x in xg]
        pg = [_dot(qp, jnp.concatenate([kpe_buf[half, g * kpm + j] for j in range(kpm)], axis=1).astype(BF16))
              for g in gs]
        xs.append(jnp.concatenate(xg, axis=0))
        ss.append(jnp.concatenate([scores(r, s_rope) for r, s_rope in zip(rg, pg)], axis=1))
        start_pages(nxt, 1 - half, range(g0 * kpm, min(g0 * kpm + ppb, n_pages)))

    xn = cn_ref[...].astype(BF16)
    sn = scores(_dot_nt(lq, xn), _dot_nt(qp, kn_ref[...].astype(BF16)))
    tq = lax.broadcasted_iota(jnp.int32, (nrow, t_new), 0) % t_new
    ts = lax.broadcasted_iota(jnp.int32, (nrow, t_new), 1)
    xs.append(xn)
    ss.append(jnp.where(ts <= tq, sn, -jnp.inf))

    m = jnp.max(ss[0], axis=-1, keepdims=True)
    for s in ss[1:]:
        m = jnp.maximum(m, jnp.max(s, axis=-1, keepdims=True))
    ps = [jnp.exp2(s - m) for s in ss]
    l = jnp.sum(ps[0], axis=-1, keepdims=True)
    for p in ps[1:]:
        l = l + jnp.sum(p, axis=-1, keepdims=True)
    inv_l = 1.0 / l
    pvs = [_dot(p.astype(BF16), x) for p, x in zip(ps, xs)]
    oa = None
    for c in range(SAMPLE_PV_CHAINS):
        part = pvs[c::SAMPLE_PV_CHAINS]
        acc = part[0]
        for pv in part[1:]:
            acc = acc + pv
        oa = acc * inv_l if oa is None else oa + acc * inv_l

    r = _dot(oa.astype(BF16), wv_ref[...])
    lane = lax.broadcasted_iota(jnp.int32, (1, MLA_H * MLA_DV), 1)
    out = jnp.zeros((t_new, MLA_H * MLA_DV), F32)
    for h in range(MLA_H):
        out = out + jnp.where(lane // MLA_DV == h, r[t_new * h:t_new * (h + 1)], 0.0)
    o_ref[...] = out.astype(BF16)

    @pl.when(b == nb - 1)
    def _():
        wait_pages(nxt, 1 - half, range(n_pages))


def _mla_sample(mq, ckv_new, kp_new, pool_ckv, pool_kpe_t, page_table, w, batch, t_new, kpm):
    n_pages = page_table.shape[1]
    pt = page_table.reshape(-1)
    tok = lambda width: pl.BlockSpec((None, t_new, width), lambda b, pt_ref: (b, 0, 0))
    g3 = lambda a: a.reshape(batch, t_new, a.shape[-1])
    const = lambda a: pl.BlockSpec(a.shape, lambda b, pt_ref: (0,) * a.ndim)
    hbm = pl.BlockSpec(memory_space=pl.ANY)
    grid_spec = pltpu.PrefetchScalarGridSpec(
        num_scalar_prefetch=1,
        grid=(batch,),
        in_specs=[tok(MLA_H * LANES), const(w['w_kq']), const(w['w_kt']), const(w['w_v']),
                  tok(MLA_KVR), tok(MLA_ROPE), hbm, hbm],
        out_specs=tok(MLA_H * MLA_DV),
        scratch_shapes=[pltpu.VMEM((2, n_pages, PAGE, MLA_KVR), F32),
                        pltpu.VMEM((2, n_pages, MLA_ROPE, PAGE), F32),
                        pltpu.SemaphoreType.DMA((2, 2))],
    )
    om = pl.pallas_call(
        functools.partial(_mla_sample_kernel, n_pages=n_pages, t_new=t_new, kpm=kpm),
        grid_spec=grid_spec,
        out_shape=jax.ShapeDtypeStruct((batch, t_new, MLA_H * MLA_DV), BF16),
        compiler_params=_cparams(("arbitrary",)),
        name="mla_sample",
    )(pt, g3(mq), w['w_kq'], w['w_kt'], w['w_v'], g3(ckv_new), g3(kp_new), pool_ckv, pool_kpe_t)
    return om.reshape(batch * t_new, MLA_H * MLA_DV)


def _mix_out_kernel(x_ref, og_ref, om_ref, wo_ref, gx_ref, wxq_ref, gxq_ref, x1_o, xq_o):
    x1 = x_ref[...] + _dot(og_ref[...], wo_ref[0:512, :]) + _dot(om_ref[...], wo_ref[512:1024, :])
    x1_o[...] = x1
    qf = _dot(_rms(x1, gx_ref[...]).astype(BF16), wxq_ref[...])
    g = gxq_ref[...]
    for h in range(X_H):
        hs = slice(X_D * h, X_D * (h + 1))
        xq_o[:, hs] = _rms(qf[:, hs], g).astype(BF16)


def _mix_out(x, og, om, w, tm):
    n = x.shape[0]
    row = lambda width: pl.BlockSpec((tm, width), lambda i: (i, 0))
    consts = [w['w_out'], w['g_x'], w['w_xq'], w['g_xq']]
    return pl.pallas_call(
        _mix_out_kernel,
        grid=(n // tm,),
        in_specs=[row(D_MODEL), row(512), row(512)] + [_full(c.shape) for c in consts],
        out_specs=[row(D_MODEL), row(X_H * X_D)],
        out_shape=[jax.ShapeDtypeStruct((n, D_MODEL), F32), jax.ShapeDtypeStruct((n, X_H * X_D), BF16)],
        compiler_params=_cparams(("parallel",)),
        name="mix_out",
    )(x, og, om, *consts)


def _cross_kernel(xq_ref, mk_ref, mv_ref, o_ref, *, n_mem):
    gb = xq_ref.shape[0]

    def head(ref, g, h):
        if len(ref.shape) == 4:
            return ref[g, h]
        return ref[pl.ds(g * n_mem * X_H + h, n_mem, stride=X_H), :].astype(BF16)

    units = [(g, h) for g in range(gb) for h in range(X_H)]
    ss = [_dot_nt(xq_ref[g, :, X_D * h:X_D * (h + 1)], head(mk_ref, g, h)) for g, h in units]
    ps = [jnp.exp2(s - jnp.max(s, axis=-1, keepdims=True)) for s in ss]
    for (g, h), p in zip(units, ps):
        o = _dot(p.astype(BF16), head(mv_ref, g, h))
        o_ref[g, :, X_D * h:X_D * (h + 1)] = (o * (1.0 / jnp.sum(p, axis=-1, keepdims=True))).astype(BF16)


def _cross(xq, mk, mv, batch, seq, n_mem, tq, gb):
    nq = seq // tq
    assert gb == 1 or nq == 1
    row = pl.BlockSpec((gb, tq, X_H * X_D), lambda b, i: (b * nq + i, 0, 0))
    if mk.ndim == 4:
        mem = pl.BlockSpec((gb,) + mk.shape[1:], lambda b, i: (b, 0, 0, 0))
    else:
        mem = pl.BlockSpec((gb * n_mem * X_H, X_D), lambda b, i: (b, 0))
    o = pl.pallas_call(
        functools.partial(_cross_kernel, n_mem=n_mem),
        grid=(batch // gb, nq),
        in_specs=[row, mem, mem],
        out_specs=row,
        out_shape=jax.ShapeDtypeStruct((batch * nq, tq, X_H * X_D), BF16),
        compiler_params=_cparams(("parallel", "arbitrary")),
        name="cross",
    )(xq.reshape(batch * nq, tq, X_H * X_D), mk, mv)
    return o.reshape(batch * seq, X_H * X_D)


def _ffn_kernel(x_ref, o_ref, wxo_ref, g_ref, w1_ref, w2_ref, y_o, *, fc):
    x = x_ref[...] + _dot(o_ref[...], wxo_ref[...])
    h = _rms(x, g_ref[...]).astype(BF16)
    acc = x
    for c in range(D_FF // fc):
        u = jnp.maximum(_dot(h, w1_ref[:, fc * c:fc * (c + 1)]), 0.0)
        acc = acc + _dot((u * u).astype(BF16), w2_ref[fc * c:fc * (c + 1), :])
    y_o[...] = acc


def _ffn(x, o, w, tm, fc):
    n = x.shape[0]
    row = lambda width: pl.BlockSpec((tm, width), lambda i: (i, 0))
    consts = [w['w_xo'], w['g_ff'], w['w_ff1'], w['w_ff2']]
    return pl.pallas_call(
        functools.partial(_ffn_kernel, fc=fc),
        grid=(n // tm,),
        in_specs=[row(D_MODEL), row(X_H * X_D)] + [_full(c.shape) for c in consts],
        out_specs=row(D_MODEL),
        out_shape=jax.ShapeDtypeStruct((n, D_MODEL), F32),
        compiler_params=_cparams(("parallel",)),
        name="ffn",
    )(x, o, *consts)


def _mem_kv_kernel(m_ref, gm_ref, wk_ref, wv_ref, gk_ref, k_o, v_o, kh_o, vh_o, *, n_mem):
    tm = m_ref.shape[0]
    m = _rms(m_ref[...], gm_ref[...]).astype(BF16)
    kf = _dot(m, wk_ref[...])
    vf = _dot(m, wv_ref[...])
    g = gk_ref[...]
    for h in range(X_H):
        hs = slice(X_D * h, X_D * (h + 1))
        kn = _rms(kf[:, hs], g)
        k_o[pl.ds(h, tm, stride=X_H), :] = kn
        v_o[pl.ds(h, tm, stride=X_H), :] = vf[:, hs]
        for b in range(kh_o.shape[0]):
            rs = slice(n_mem * b, n_mem * (b + 1))
            kh_o[b, h] = kn[rs].astype(BF16)
            vh_o[b, h] = vf[rs, hs].astype(BF16)


def _mem_kv(mem, w, n_mem, tm):
    n = mem.shape[0]
    bt = tm // n_mem
    row = lambda width: pl.BlockSpec((tm, width), lambda i: (i, 0))
    head = pl.BlockSpec((bt, X_H, n_mem, X_D), lambda i: (i, 0, 0, 0))
    consts = [w['g_mem'], w['w_xk'], w['w_xv'], w['g_xk']]
    return pl.pallas_call(
        functools.partial(_mem_kv_kernel, n_mem=n_mem),
        grid=(n // tm,),
        in_specs=[row(D_MODEL)] + [_full(c.shape) for c in consts],
        out_specs=[pl.BlockSpec((tm * X_H, X_D), lambda i: (i, 0))] * 2 + [head, head],
        out_shape=[jax.ShapeDtypeStruct((n * X_H, X_D), F32)] * 2
                  + [jax.ShapeDtypeStruct((n // n_mem, X_H, n_mem, X_D), BF16)] * 2,
        compiler_params=_cparams(("parallel",)),
        name="mem_kv",
    )(mem, *consts)


def _prep_weights(g_mix, w_in, w_gla_a2, b_gla_a, g_gla_o, g_mla_qa, w_mla_qb, g_mla_kva, w_mla_kvb,
                  g_q_nope, g_k_nope, g_q_rope, g_k_rope, w_out, g_x, g_mem, w_xq, w_xk, w_xv, g_xq, g_xk,
                  w_xo, g_ff, w_ff1, w_ff2):
    rowv = lambda g: g.reshape(1, -1).astype(F32)
    zc = lambda n: jnp.zeros((D_MODEL, n), F32)
    sizes = (256, 256, 512, 512, GLA_RANK, MLA_QR, MLA_KVR, MLA_ROPE)
    offs = [0]
    for s in sizes:
        offs.append(offs[-1] + s)
    q, k, v, r, a, qa, kva, kpe = [w_in[:, offs[i]:offs[i + 1]] for i in range(8)]
    w_in_p = jnp.concatenate(
        [q * (GLA_K ** -0.5), k, v, r, qa,
         a, zc(_KPE_LANE - GLA_RANK), kpe, zc(LANES - _KPE_LANE - MLA_ROPE), kva], axis=1)
    w_a2 = jnp.concatenate([w_gla_a2, jnp.zeros((LANES - GLA_RANK, GLA_H * GLA_K), F32)], axis=0)

    qb = w_mla_qb.reshape(MLA_QR, MLA_H, MLA_NOPE + MLA_ROPE)
    qb = jnp.pad(qb, ((0, 0), (0, 0), (0, LANES - MLA_NOPE - MLA_ROPE))).reshape(MLA_QR, MLA_H * LANES)
    kvb = w_mla_kvb.reshape(MLA_KVR, MLA_H, MLA_NOPE + MLA_DV)
    wk = kvb[:, :, :MLA_NOPE]
    wk_p = jnp.pad(wk, ((0, 0), (0, 0), (0, LANES - MLA_NOPE))).reshape(MLA_KVR, MLA_H * LANES)
    wv = kvb[:, :, MLA_NOPE:].reshape(MLA_KVR, MLA_H * MLA_DV)
    z32 = jnp.zeros((LANES - MLA_NOPE - MLA_ROPE,), F32)
    gq_row = jnp.tile(jnp.concatenate([g_q_nope, g_q_rope, z32]), MLA_H) * (MLA_SCALE * LOG2E)
    gk_row = jnp.tile(jnp.concatenate([g_k_nope, jnp.zeros((LANES - MLA_NOPE,), F32)]), MLA_H)
    g_kpe = jnp.concatenate([jnp.zeros((_KPE_LANE,), F32), g_k_rope, z32])

    li = jnp.arange(2 * LANES)
    seg = jnp.where(li % LANES < MLA_NOPE, 0, jnp.where(li % LANES < MLA_NOPE + MLA_ROPE, 1, 2))
    same = (li[:, None] // LANES == li[None, :] // LANES) & (seg[:, None] == seg[None, :]) & (seg[:, None] < 2)
    bd = jnp.where(same, jnp.where(seg[:, None] == 0, 1.0 / MLA_NOPE, 1.0 / MLA_ROPE), 0.0)

    wkq = jnp.transpose(wk, (1, 2, 0)) * g_k_nope[None, :, None]
    wkq = jnp.pad(wkq, ((0, 0), (0, LANES - MLA_NOPE), (0, 0)))
    wkt = jnp.transpose(wk, (1, 2, 0)).reshape(MLA_H * MLA_NOPE, MLA_KVR)

    hk = jnp.arange(GLA_H * GLA_K) // GLA_K
    hv = jnp.arange(GLA_H * GLA_V) // GLA_V
    bexp = (hk[:, None] == hv[None, :])
    return {
        'g_mix': rowv(g_mix), 'w_in': w_in_p.astype(BF16), 'w_a2': w_a2.astype(BF16), 'b_a': rowv(b_gla_a),
        'g_qa': rowv(g_mla_qa), 'w_qb': qb.astype(BF16), 'gq_row': rowv(gq_row), 'g_kva': rowv(g_mla_kva),
        'w_kv': jnp.concatenate([wk_p, wv], axis=1).astype(BF16), 'gk_row': rowv(gk_row),
        'w_v': wv.astype(BF16), 'g_kpe': rowv(g_kpe),
        'bd': bd.astype(BF16), 'w_kq': wkq.astype(BF16), 'w_kt': wkt.astype(BF16),
        'g_gla_o': rowv(g_gla_o), 'bexp': bexp.astype(BF16),
        'w_out': w_out.astype(BF16), 'g_x': rowv(g_x), 'w_xq': w_xq.astype(BF16),
        'g_xq': rowv(g_xq) * (X_SCALE * LOG2E),
        'g_mem': rowv(g_mem), 'w_xk': w_xk.astype(BF16), 'w_xv': w_xv.astype(BF16), 'g_xk': rowv(g_xk),
        'w_xo': w_xo.astype(BF16), 'g_ff': rowv(g_ff), 'w_ff1': w_ff1.astype(BF16), 'w_ff2': w_ff2.astype(BF16),
    }


def _rope_tables(pos):
    half = MLA_ROPE // 2
    inv = ROPE_THETA ** (-jnp.arange(half, dtype=F32) / half)
    ang = pos.astype(F32)[:, None] * inv[None, :]
    cos, sin = jnp.cos(ang), jnp.sin(ang)
    n = pos.shape[0]
    one = jnp.ones((n, _KPE_LANE), F32)
    z = lambda w_: jnp.zeros((n, w_), F32)
    tail = LANES - _KPE_LANE - MLA_ROPE
    c = jnp.concatenate([one, cos, cos, jnp.ones((n, tail), F32)], axis=1)
    s_up = jnp.concatenate([z(_KPE_LANE), -sin, z(half), z(tail)], axis=1)
    s_dn = jnp.concatenate([z(_KPE_LANE), z(half), sin, z(tail)], axis=1)
    return c, s_up, s_dn


def _tile_rows(n, cap):
    t = min(n, cap)
    while n % t:
        t //= 2
    return t


def _tail(x1, xq, mk, mv, w, batch, seq, n_mem):
    tq = _tile_rows(seq, 512)
    gb = _tile_rows(batch, max(1, CROSS_ROWS // seq)) if tq == seq else 1
    o = _cross(xq, mk, mv, batch, seq, n_mem, tq, gb)
    return _ffn(x1, o, w, _tile_rows(x1.shape[0], 512), 1024)


def _prompt_layer(x, mem, w):
    batch, seq, _ = x.shape
    n = batch * seq
    xf = x.reshape(n, D_MODEL)
    tm = _tile_rows(seq, 512)
    tabs = _rope_tables(jnp.arange(seq))
    gq, gk, gv, gr, la, mq, ckv, kp, mk_, mvt = _in_proj(xf, w, tabs, tm, True)
    c_len = GLA_CHUNK
    cps = _tile_rows(seq // c_len, GLA_CHUNKS_PER_STEP)
    s0 = jnp.zeros((batch, GLA_H * GLA_K, GLA_V), F32)
    og, s_fin = _gla(gq, gk, gv, gr, la, s0, w, batch, seq, c_len, min(GLA_SUB, c_len), cps,
                     _tile_rows(batch, GLA_SEQS_PER_STEP))
    om = _mla_prompt(mq, mk_, mvt, batch, seq, ATT_TILE)
    x1, xq = _mix_out(xf, og, om, w, tm)
    n_mem = mem.shape[1]
    memf = mem.reshape(-1, D_MODEL)
    xk, xv, xkh, xvh = _mem_kv(memf, w, n_mem, max(n_mem, _tile_rows(memf.shape[0], 512)))
    y = _tail(x1, xq, xkh, xvh, w, batch, seq, n_mem)
    return y, ckv, kp, xk, xv, s_fin


def _sample_layer(x, pool_ckv, pool_kpe, page_table, mem_k, mem_v, s_prev, w):
    batch, seq, _ = x.shape
    n = batch * seq
    xf = x.reshape(n, D_MODEL)
    tm = _tile_rows(n, 512)
    past = page_table.shape[1] * PAGE
    pos = past + (jnp.arange(tm) % seq)
    gq, gk, gv, gr, la, mq, ckv, kp = _in_proj(xf, w, _rope_tables(pos), tm, False)
    s0 = s_prev.reshape(batch, GLA_H * GLA_K, GLA_V)
    og, s_fin = _gla(gq, gk, gv, gr, la, s0, w, batch, seq, seq, seq, 1, _tile_rows(batch, GLA_SHORT_SEQS_PER_STEP))
    om = _mla_sample(mq, ckv, kp, pool_ckv, jnp.swapaxes(pool_kpe, 1, 2), page_table, w, batch, seq,
                     _tile_rows(page_table.shape[1], SAMPLE_PAGES_PER_DOT))
    x1, xq = _mix_out(xf, og, om, w, tm)
    y = _tail(x1, xq, mem_k.reshape(-1, X_D), mem_v.reshape(-1, X_D), w, batch, seq, mem_k.shape[1])
    return y, ckv, kp, s_fin


def kernel(x_prompt, x_sample, mem_prompt, cache_ckv, cache_kpe, cache_mem_k, cache_mem_v, state_gla, page_table, g_mix, w_in, w_gla_a2, b_gla_a, g_gla_o, g_mla_qa, w_mla_qb, g_mla_kva, w_mla_kvb, g_q_nope, g_k_nope, g_q_rope, g_k_rope, w_out, g_x, g_mem, w_xq, w_xk, w_xv, g_xq, g_xk, w_xo, g_ff, w_ff1, w_ff2):
    depth = w_in.shape[0]
    assert depth == 1, "one layer: prompt-group caches of layer l would feed layer l+1 otherwise unchanged"
    params = (g_mix, w_in, w_gla_a2, b_gla_a, g_gla_o, g_mla_qa, w_mla_qb, g_mla_kva, w_mla_kvb,
              g_q_nope, g_k_nope, g_q_rope, g_k_rope, w_out, g_x, g_mem, w_xq, w_xk, w_xv, g_xq, g_xk,
              w_xo, g_ff, w_ff1, w_ff2)
    w = _prep_weights(*[p[0] for p in params])
    bp, tp, _ = x_prompt.shape
    bs, tsq, _ = x_sample.shape
    yp, ckv_p, kp_p, xk, xv, gla_p = _prompt_layer(x_prompt, mem_prompt, w)
    ys, ckv_s, kp_s, gla_s = _sample_layer(x_sample, cache_ckv.reshape(cache_ckv.shape[1:]), cache_kpe.reshape(cache_kpe.shape[1:]), page_table,
                                            cache_mem_k[0], cache_mem_v[0], state_gla[0], w)
    n_mem = mem_prompt.shape[1]
    return (yp.reshape(bp, tp, D_MODEL), ys.reshape(bs, tsq, D_MODEL),
            ckv_p.reshape(1, bp, tp, MLA_KVR), kp_p.reshape(1, bp, tp, MLA_ROPE),
            xk.reshape(1, bp, n_mem, X_H, X_D), xv.reshape(1, bp, n_mem, X_H, X_D),
            gla_p.reshape(1, bp, GLA_H, GLA_K, GLA_V),
            ckv_s.reshape(1, bs, tsq, MLA_KVR), kp_s.reshape(1, bs, tsq, MLA_ROPE),
            gla_s.reshape(1, bs, GLA_H, GLA_K, GLA_V))
```

```python
import functools

import jax
import jax.numpy as jnp
from jax import lax
from jax.experimental import pallas as pl
from jax.experimental.pallas import tpu as pltpu

F32 = jnp.float32
BF16 = jnp.bfloat16

EPS = 1e-6
D_MODEL = 1024
GLA_H, GLA_K, GLA_V, GLA_RANK, GLA_TAU, GLA_CHUNK = 4, 64, 128, 16, 16.0, 64
GLA_SUB = 16
GLA_CHUNKS_PER_STEP = 4
GLA_SEQS_PER_STEP = 4
GLA_SHORT_SEQS_PER_STEP = 8
GLA_FAST_MAX = 40.0
MLA_H, MLA_DV, MLA_NOPE, MLA_ROPE, MLA_QR, MLA_KVR = 8, 64, 64, 32, 384, 256
MLA_SCALE = (MLA_NOPE + MLA_ROPE) ** -0.5
ROPE_THETA = 10000.0
X_H, X_D = 4, 128
X_SCALE = X_D ** -0.5
D_FF = 4096
PAGE = 128
LANES = 128
NEG = -1e30
LOG2E = 1.4426950408889634
ATT_TILE = 256
ATT_GROUP = 2
CROSS_ROWS = 32
SAMPLE_PAGES_PER_DOT = 4
SAMPLE_GROUPS = 4

_C_Q, _C_K, _C_V, _C_R = 0, 256, 512, 1024
_C_QA, _C_AK, _C_KVA, _C_END = 1536, 1920, 2048, 2304
_KPE_LANE = 64

VMEM_LIMIT = 56 * 1024 * 1024


def _cparams(sem):
    return pltpu.CompilerParams(dimension_semantics=sem, vmem_limit_bytes=VMEM_LIMIT)


def _dot(a, b):
    return jnp.dot(a, b, preferred_element_type=F32)


def _dot_nt(a, b):
    return lax.dot_general(a, b, (((1,), (1,)), ((), ())), preferred_element_type=F32)


def _rms(x, g):
    return x * lax.rsqrt(jnp.mean(x * x, axis=-1, keepdims=True) + EPS) * g


def _full(shape):
    n = len(shape)
    return pl.BlockSpec(shape, lambda *_: (0,) * n)


def _in_proj_kernel(x_ref, gmix_ref, win_ref, wa2_ref, ba_ref, gqa_ref, wqb_ref, gqrow_ref,
                    gkva_ref, wkv_ref, gkrow_ref, gkpe_ref, bd_ref, c_ref, s1_ref, s2_ref,
                    gq_o, gk_o, gv_o, gr_o, la_o, mq_o, ckv_o, kp_o, *kv_o):
    h = _rms(x_ref[...], gmix_ref[...]).astype(BF16)

    z2 = _dot(h, win_ref[:, _C_QA:_C_END])
    qa = z2[:, 0:_C_AK - _C_QA]
    zak = z2[:, _C_AK - _C_QA:_C_KVA - _C_QA]
    kva = z2[:, _C_KVA - _C_QA:]

    gate = _dot(zak.astype(BF16), wa2_ref[...]) + ba_ref[...]
    la_o[...] = (jnp.minimum(gate, 0.0) - jnp.log1p(jnp.exp(-jnp.abs(gate)))) * (1.0 / GLA_TAU)

    cos, s_up, s_dn = c_ref[...], s1_ref[...], s2_ref[...]

    def rope(t):
        return t * cos + pltpu.roll(t, LANES - 16, 1) * s_up + pltpu.roll(t, 16, 1) * s_dn

    lane = lax.broadcasted_iota(jnp.int32, (1, LANES), 1)
    zk = jnp.where((lane >= _KPE_LANE) & (lane < _KPE_LANE + MLA_ROPE), zak, 0.0)
    kpn = zk * lax.rsqrt(jnp.sum(zk * zk, axis=-1, keepdims=True) * (1.0 / MLA_ROPE) + EPS) * gkpe_ref[...]
    kpt = rope(kpn)
    kp_o[...] = pltpu.roll(kpt, LANES - _KPE_LANE, 1)[:, 0:MLA_ROPE]

    ckv = _rms(kva, gkva_ref[...])
    ckv_o[...] = ckv
    cb = ckv.astype(BF16)
    bd = bd_ref[...]

    def seg_norm(t, grow):
        ms = _dot((t * t).astype(BF16), bd)
        return t * lax.rsqrt(ms + EPS) * grow

    qf = _dot(_rms(qa, gqa_ref[...]).astype(BF16), wqb_ref[...])
    for c in range(4):
        sl = slice(2 * LANES * c, 2 * LANES * (c + 1))
        qn = seg_norm(qf[:, sl], gqrow_ref[:, sl])
        for j in range(2):
            o = 2 * LANES * c + LANES * j
            mq_o[:, o:o + LANES] = rope(qn[:, LANES * j:LANES * (j + 1)]).astype(BF16)

    if kv_o:
        mk_o, mvt_o = kv_o
        kvf = _dot(cb, wkv_ref[...])
        kf, mv = kvf[:, 0:MLA_H * LANES], kvf[:, MLA_H * LANES:]
        for c in range(mvt_o.shape[0]):
            mvt_o[c] = mv[ATT_TILE * c:ATT_TILE * (c + 1), :].T.astype(BF16)
        for c in range(4):
            sl = slice(2 * LANES * c, 2 * LANES * (c + 1))
            kn = seg_norm(kf[:, sl], gkrow_ref[:, sl])
            for j in range(2):
                o = 2 * LANES * c + LANES * j
                mk_o[:, o:o + LANES] = (kn[:, LANES * j:LANES * (j + 1)] + kpt).astype(BF16)

    z1 = _dot(h, win_ref[:, _C_Q:_C_QA])
    gq_o[...] = z1[:, _C_Q:_C_K].astype(BF16)
    gk_o[...] = z1[:, _C_K:_C_V].astype(BF16)
    gv_o[...] = z1[:, _C_V:_C_R].astype(BF16)
    gr_o[...] = z1[:, _C_R:_C_QA].astype(BF16)


def _in_proj(x, w, tabs, tm, with_kv):
    n = x.shape[0]
    nt = tabs[0].shape[0] // tm
    row = lambda width: pl.BlockSpec((tm, width), lambda i: (i, 0))
    tab = pl.BlockSpec((tm, LANES), lambda i: (i % nt, 0))
    consts = [w['g_mix'], w['w_in'], w['w_a2'], w['b_a'], w['g_qa'], w['w_qb'], w['gq_row'],
              w['g_kva'], w['w_kv'], w['gk_row'], w['g_kpe'], w['bd']]
    out_w = [(256, BF16), (256, BF16), (512, BF16), (512, BF16), (256, F32),
             (1024, BF16), (256, F32), (MLA_ROPE, F32)] + ([(1024, BF16)] if with_kv else [])
    out_specs = [row(wd) for wd, _ in out_w]
    out_shape = [jax.ShapeDtypeStruct((n, wd), dt) for wd, dt in out_w]
    if with_kv:
        out_specs.append(pl.BlockSpec((tm // ATT_TILE, MLA_H * MLA_DV, ATT_TILE), lambda i: (i, 0, 0)))
        out_shape.append(jax.ShapeDtypeStruct((n // ATT_TILE, MLA_H * MLA_DV, ATT_TILE), BF16))
    return pl.pallas_call(
        _in_proj_kernel,
        grid=(n // tm,),
        in_specs=[row(D_MODEL)] + [_full(c.shape) for c in consts] + [tab, tab, tab],
        out_specs=out_specs,
        out_shape=out_shape,
        compiler_params=_cparams(("parallel",)),
        name="in_proj",
    )(x, *consts, *tabs)


def _head_masks():
    lane = lax.broadcasted_iota(jnp.int32, (1, GLA_H * GLA_K), 1)
    return [(lane // GLA_K == h).astype(F32) for h in range(GLA_H)]


def _gla_fast_step(qs, ks, vs, bs, sts, keep, c_len, cps):
    masks = _head_masks()
    ts = c_len * cps
    n = len(qs)
    ebs = [jnp.exp(b) for b in bs]
    kts = [k * jnp.exp(-b) for k, b in zip(ks, bs)]
    lqs = [jnp.concatenate([q * eb * m for m in masks], axis=0).astype(BF16)
           for q, eb in zip(qs, ebs)]
    atts = [_dot_nt(lq, kt.astype(BF16)) for lq, kt in zip(lqs, kts)]
    vbs = [v.astype(BF16) for v in vs]
    o_intra = [[_dot((atts[g][ts * h:ts * (h + 1)] * keep).astype(BF16), vbs[g][:, GLA_V * h:GLA_V * (h + 1)])
                for h in range(GLA_H)] for g in range(n)]
    o_inter = [[] for _ in range(n)]
    sts = list(sts)
    for c in range(cps):
        cs = slice(c_len * c, c_len * (c + 1))
        for g in range(n):
            lqc = jnp.concatenate([lqs[g][ts * h + c_len * c:ts * h + c_len * (c + 1)] for h in range(GLA_H)],
                                  axis=0)
            o_inter[g].append(_dot_nt(lqc, sts[g].astype(BF16)))
            ebl = ebs[g][c_len * (c + 1) - 1:c_len * (c + 1), :]
            kl = kts[g][cs] * ebl
            vst = jnp.concatenate([vs[g][cs, GLA_V * h:GLA_V * (h + 1)] for h in range(GLA_H)], axis=0)
            kst = jnp.concatenate([kl * m for m in masks], axis=0).astype(BF16)
            sts[g] = sts[g] * ebl + _dot(vst.T.astype(BF16), kst)
    outs = [[o_intra[g][h] + jnp.concatenate([o_inter[g][c][c_len * h:c_len * (h + 1)] for c in range(cps)],
                                             axis=0) for h in range(GLA_H)] for g in range(n)]
    return outs, sts


def _gla_chunk(q, k, v, b, st, bexp, c_len, sub):
    masks = _head_masks()
    bl = b[c_len - 1:c_len, :]

    qh = q * jnp.exp(b)
    lq = jnp.concatenate([qh * m for m in masks], axis=0).astype(BF16)
    o_inter = _dot_nt(lq, st.astype(BF16))

    nsub = c_len // sub
    a_rows = [[] for _ in range(GLA_H)]
    col = lax.broadcasted_iota(jnp.int32, (1, c_len), 1)
    for i in range(1, nsub):
        ref = b[sub * i - 1:sub * i, :]
        qi = q[sub * i:sub * (i + 1)] * jnp.exp(b[sub * i:sub * (i + 1)] - ref)
        ki = k * jnp.exp(jnp.minimum(ref - b, 0.0))
        li = jnp.concatenate([qi * m for m in masks], axis=0).astype(BF16)
        ai = jnp.where(col < sub * i, _dot_nt(li, ki.astype(BF16)), 0.0)
        for h in range(GLA_H):
            a_rows[h].append(ai[sub * h:sub * (h + 1)])

    row = lax.broadcasted_iota(jnp.int32, (sub, 1), 0)
    o_diag = []
    for i in range(nsub):
        sl = slice(sub * i, sub * (i + 1))
        bb, qb, kb, vb = b[sl], q[sl], k[sl], v[sl]
        ps = []
        for s in range(sub):
            e = jnp.exp(jnp.where(row >= s, bb - bb[s:s + 1, :], NEG))
            ps.append(qb * e * kb[s:s + 1, :])
        r = _dot(jnp.concatenate(ps, axis=0).astype(BF16), bexp)
        od = r[0:sub] * vb[0:1, :]
        for s in range(1, sub):
            od = od + r[sub * s:sub * (s + 1)] * vb[s:s + 1, :]
        o_diag.append(od)
    o_diag = jnp.concatenate(o_diag, axis=0) if nsub > 1 else o_diag[0]

    outs = []
    for h in range(GLA_H):
        oh = o_inter[c_len * h:c_len * (h + 1)] + o_diag[:, GLA_V * h:GLA_V * (h + 1)]
        if nsub > 1:
            ah = jnp.concatenate([jnp.zeros((sub, c_len), F32)] + a_rows[h], axis=0)
            oh = oh + _dot(ah.astype(BF16), v[:, GLA_V * h:GLA_V * (h + 1)].astype(BF16))
        outs.append(oh)

    kl = k * jnp.exp(bl - b)
    vs = jnp.concatenate([v[:, GLA_V * h:GLA_V * (h + 1)] for h in range(GLA_H)], axis=0)
    ks = jnp.concatenate([kl * m for m in masks], axis=0).astype(BF16)
    st_new = st * jnp.exp(bl) + _dot(vs.T.astype(BF16), ks)
    return outs, st_new


def _gla_kernel(*refs, c_len, sub, cps, fast):
    if fast:
        (q_ref, k_ref, v_ref, r_ref, la_ref, s0_ref, g_ref, bexp_ref, tri_ref, keep_ref,
         og_o, sf_o, st_ref, b_ref) = refs
    else:
        q_ref, k_ref, v_ref, r_ref, la_ref, s0_ref, g_ref, bexp_ref, og_o, sf_o, st_ref, b_ref = refs
    t = pl.program_id(1)
    gb = q_ref.shape[0]

    @pl.when(t == 0)
    def _():
        for i in range(gb):
            st_ref[i] = s0_ref[i].T

    g = g_ref[...]

    def emit(i, outs, rs):
        rr = r_ref[i, rs, :].astype(F32)
        for h in range(GLA_H):
            hs = slice(GLA_V * h, GLA_V * (h + 1))
            rh = rr[:, hs]
            og_o[i, rs, hs] = (_rms(outs[h], g) * (rh / (1.0 + jnp.exp(-rh)))).astype(BF16)

    bs = []
    for i in range(gb):
        la = la_ref[i]
        if fast:
            la_hi = la.astype(BF16)
            la_lo = (la - la_hi.astype(F32)).astype(BF16)
            b = _dot(tri_ref[...], la_hi) + _dot(tri_ref[...], la_lo)
        else:
            rows = lax.broadcasted_iota(jnp.int32, (c_len, 1), 0)
            b = jnp.zeros_like(la)
            for j in range(c_len):
                b = b + jnp.where(rows >= j, la[j:j + 1, :], 0.0)
        b_ref[i] = b
        bs.append(b)

    def robust():
        bexp = bexp_ref[...]
        for i in range(gb):
            def body(c, carry, i=i):
                rs = pl.ds(pl.multiple_of(c * c_len, c_len), c_len)
                outs, st_new = _gla_chunk(q_ref[i, rs, :].astype(F32), k_ref[i, rs, :].astype(F32),
                                          v_ref[i, rs, :].astype(F32), b_ref[i, rs, :], st_ref[i], bexp, c_len, sub)
                st_ref[i] = st_new
                emit(i, outs, rs)
                return carry

            lax.fori_loop(0, cps, body, 0)

    if fast:
        low = jnp.min(bs[0])
        for b in bs[1:]:
            low = jnp.minimum(low, jnp.min(b))
        mild = low >= -GLA_FAST_MAX

        @pl.when(mild)
        def _():
            seqs = range(gb)
            outs, sts = _gla_fast_step([q_ref[i].astype(F32) for i in seqs], [k_ref[i].astype(F32) for i in seqs],
                                       [v_ref[i].astype(F32) for i in seqs], [b_ref[i] for i in seqs],
                                       [st_ref[i] for i in seqs], keep_ref[...], c_len, cps)
            for i in seqs:
                st_ref[i] = sts[i]
                emit(i, outs[i], slice(None))

        pl.when(jnp.logical_not(mild))(robust)
    else:
        robust()

    @pl.when(t == pl.num_programs(1) - 1)
    def _():
        for i in range(gb):
            sf_o[i] = st_ref[i].T


def _gla(gq, gk, gv, gr, la, s0, w, batch, seq, c_len, sub, cps, gb):
    ts = c_len * cps
    nt = seq // ts
    fast = c_len == GLA_CHUNK
    assert fast or cps == 1
    row = lambda width: pl.BlockSpec((gb, None, ts, width), lambda b, t: (b, t, 0, 0))
    g4 = lambda a: a.reshape(batch, nt, ts, a.shape[-1])
    st_spec = pl.BlockSpec((gb, GLA_H * GLA_K, GLA_V), lambda b, t: (b, 0, 0))
    consts = [w['g_gla_o'], w['bexp']]
    if fast:
        ti = jnp.arange(ts)
        keep = (ti[:, None] // c_len == ti[None, :] // c_len) & (ti[None, :] <= ti[:, None])
        consts += [keep.astype(BF16), keep.astype(F32)]
    og, s_fin = pl.pallas_call(
        functools.partial(_gla_kernel, c_len=c_len, sub=sub, cps=cps, fast=fast),
        grid=(batch // gb, nt),
        in_specs=[row(256), row(256), row(512), row(512), row(256), st_spec] + [_full(c.shape) for c in consts],
        out_specs=[row(512), st_spec],
        out_shape=[jax.ShapeDtypeStruct((batch, nt, ts, GLA_H * GLA_V), BF16),
                   jax.ShapeDtypeStruct((batch, GLA_H * GLA_K, GLA_V), F32)],
        scratch_shapes=[pltpu.VMEM((gb, GLA_V, GLA_H * GLA_K), F32), pltpu.VMEM((gb, ts, GLA_H * GLA_K), F32)],
        compiler_params=_cparams(("parallel", "arbitrary")),
        name="gla",
    )(g4(gq), g4(gk), g4(gv), g4(gr), g4(la), s0, *consts)
    return og.reshape(batch * seq, GLA_H * GLA_V), s_fin


def _mla_prompt_kernel(q_ref, k_ref, vt_ref, o_ref, m_ref, l_ref, acc_ref, *, tq):
    i = pl.program_id(1)
    m_ref[...] = jnp.full(m_ref.shape, -jnp.inf, F32)
    l_ref[...] = jnp.zeros(l_ref.shape, F32)
    acc_ref[...] = jnp.zeros(acc_ref.shape, F32)
    keep = (lax.broadcasted_iota(jnp.int32, (tq, tq), 0) <= lax.broadcasted_iota(jnp.int32, (tq, tq), 1))

    def tiles(js, last_masked):
        n = len(js)
        sts = [[_dot_nt(k_ref[pl.ds(pl.multiple_of(j * tq, tq), tq), LANES * h:LANES * (h + 1)],
                        q_ref[:, LANES * h:LANES * (h + 1)]) for j in js] for h in range(MLA_H)]
        ps, alphas = [], []
        for h in range(MLA_H):
            if last_masked:
                sts[h][-1] = jnp.where(keep, sts[h][-1], -jnp.inf)
            m_old = m_ref[h]
            m_new = m_old
            for st in sts[h]:
                m_new = jnp.maximum(m_new, jnp.max(st, axis=0, keepdims=True))
            alpha = jnp.exp2(m_old - m_new)
            ph = [jnp.exp2(st - m_new) for st in sts[h]]
            m_ref[h] = m_new
            l_new = alpha * l_ref[h]
            for p in ph:
                l_new = l_new + jnp.sum(p, axis=0, keepdims=True)
            l_ref[h] = l_new
            ps.append([p.astype(BF16) for p in ph])
            alphas.append(alpha)
        for h in range(MLA_H):
            acc = alphas[h] * acc_ref[h]
            for t in range(n):
                acc = acc + _dot(vt_ref[js[t], LANES * (h // 2):LANES * (h // 2 + 1), :], ps[h][t])
            acc_ref[h] = acc

    def body(jj, carry):
        tiles([ATT_GROUP * jj + t for t in range(ATT_GROUP)], False)
        return carry

    lax.fori_loop(0, i // ATT_GROUP, body, 0)
    for rem in range(ATT_GROUP):
        pl.when(i % ATT_GROUP == rem)(functools.partial(tiles, [i - rem + t for t in range(rem + 1)], True))

    row = lax.broadcasted_iota(jnp.int32, (LANES, 1), 0)
    for pair in range(MLA_H // 2):
        h0, h1 = 2 * pair, 2 * pair + 1
        a = acc_ref[h0] * (1.0 / l_ref[h0])
        b = acc_ref[h1] * (1.0 / l_ref[h1])
        o_ref[:, LANES * pair:LANES * (pair + 1)] = jnp.where(row < MLA_DV, a, b).T.astype(BF16)


def _mla_prompt(mq, mk, mvt, batch, seq, tq):
    nq = seq // tq
    return pl.pallas_call(
        functools.partial(_mla_prompt_kernel, tq=tq),
        grid=(batch, nq),
        in_specs=[pl.BlockSpec((tq, MLA_H * LANES), lambda b, i: (b * nq + i, 0)),
                  pl.BlockSpec((seq, MLA_H * LANES), lambda b, i: (b, 0)),
                  pl.BlockSpec((nq, MLA_H * MLA_DV, tq), lambda b, i: (b, 0, 0))],
        out_specs=pl.BlockSpec((tq, MLA_H * MLA_DV), lambda b, i: (b * nq + i, 0)),
        out_shape=jax.ShapeDtypeStruct((batch * seq, MLA_H * MLA_DV), BF16),
        scratch_shapes=[pltpu.VMEM((MLA_H, 1, tq), F32), pltpu.VMEM((MLA_H, 1, tq), F32),
                        pltpu.VMEM((MLA_H, LANES, tq), F32)],
        compiler_params=_cparams(("parallel", "arbitrary")),
        name="mla_prompt",
    )(mq, mk, mvt)


def _mla_sample_kernel(pt_ref, q_ref, wkq_ref, wkt_ref, wv_ref, cn_ref, kn_ref, ckv_hbm, kpe_hbm,
                       o_ref, ckv_buf, kpe_buf, sem, *, n_pages, t_new, kpm):
    b = pl.program_id(0)
    nb = pl.num_programs(0)
    half = b % 2
    nrow = MLA_H * t_new

    def page_copies(seq, hf, p):
        page = pt_ref[seq * n_pages + p]
        return (pltpu.make_async_copy(ckv_hbm.at[page], ckv_buf.at[hf, p], sem.at[0, hf]),
                pltpu.make_async_copy(kpe_hbm.at[page], kpe_buf.at[hf, p], sem.at[1, hf]))

    def start_pages(seq, hf, pages):
        for p in pages:
            for cp in page_copies(seq, hf, p):
                cp.start()

    def wait_pages(seq, hf, pages):
        for p in pages:
            for cp in page_copies(seq, hf, p):
                cp.wait()

    @pl.when(b == 0)
    def _():
        start_pages(0, 0, range(n_pages))

    nxt = (b + 1) % nb

    wait_pages(b, half, range(n_pages))

    qa, qr = [], []
    for h in range(MLA_H):
        qh = q_ref[:, LANES * h:LANES * (h + 1)]
        qa.append(_dot(qh, wkq_ref[h]))
        qr.append(pltpu.roll(qh.astype(F32), LANES - _KPE_LANE, 1)[:, 0:MLA_ROPE])
    lq = jnp.concatenate([jnp.concatenate(qa, axis=0).astype(BF16), wkt_ref[...]], axis=0)
    qp = jnp.concatenate(qr, axis=0).astype(BF16)

    def scores(r, s_rope):
        rinv = []
        for h in range(MLA_H):
            kr = r[nrow + MLA_NOPE * h:nrow + MLA_NOPE * (h + 1)]
            ms = jnp.sum(kr * kr, axis=0, keepdims=True) * (1.0 / MLA_NOPE)
            rinv.append(jnp.broadcast_to(lax.rsqrt(ms + EPS), (t_new, r.shape[1])))
        return r[0:nrow] * jnp.concatenate(rinv, axis=0) + s_rope

    ngrp = n_pages // kpm
    ppb = kpm * SAMPLE_GROUPS
    xs, ss = [], []
    for g0 in range(0, ngrp, SAMPLE_GROUPS):
        gs = range(g0, min(g0 + SAMPLE_GROUPS, ngrp))
        start_pages(nxt, 1 - half, range(g0 * kpm, min(g0 * kpm + ppb, n_pages)))
        xg = [jnp.concatenate([ckv_buf[half, g * kpm + j] for j in range(kpm)], axis=0).astype(BF16) for g in gs]
        rg = [_dot_nt(lq, x) for x in xg]
        pg = [_dot(qp, jnp.concatenate([kpe_buf[half, g * kpm + j] for j in range(kpm)], axis=1).astype(BF16))
              for g in gs]
        xs.append(jnp.concatenate(xg, axis=0))
        ss.append(jnp.concatenate([scores(r, s_rope) for r, s_rope in zip(rg, pg)], axis=1))

    xn = cn_ref[...].astype(BF16)
    sn = scores(_dot_nt(lq, xn), _dot_nt(qp, kn_ref[...].astype(BF16)))
    tq = lax.broadcasted_iota(jnp.int32, (nrow, t_new), 0) % t_new
    ts = lax.broadcasted_iota(jnp.int32, (nrow, t_new), 1)
    xs.append(xn)
    ss.append(jnp.where(ts <= tq, sn, -jnp.inf))

    ms = [jnp.max(s, axis=-1, keepdims=True) for s in ss]
    ps = [jnp.exp2(s - mi) for s, mi in zip(ss, ms)]
    ls = [jnp.sum(p, axis=-1, keepdims=True) for p in ps]
    pvs = [_dot(p.astype(BF16), x) for p, x in zip(ps, xs)]
    m = ms[0]
    for mi in ms[1:]:
        m = jnp.maximum(m, mi)
    ws = [jnp.exp2(mi - m) for mi in ms]
    l = ws[0] * ls[0]
    oa = ws[0] * pvs[0]
    for wi, li, pv in zip(ws[1:], ls[1:], pvs[1:]):
        l = l + wi * li
        oa = oa + wi * pv
    oa = oa / l

    r = _dot(oa.astype(BF16), wv_ref[...])
    lane = lax.broadcasted_iota(jnp.int32, (1, MLA_H * MLA_DV), 1)
    out = jnp.zeros((t_new, MLA_H * MLA_DV), F32)
    for h in range(MLA_H):
        out = out + jnp.where(lane // MLA_DV == h, r[t_new * h:t_new * (h + 1)], 0.0)
    o_ref[...] = out.astype(BF16)

    @pl.when(b == nb - 1)
    def _():
        wait_pages(nxt, 1 - half, range(n_pages))


def _mla_sample(mq, ckv_new, kp_new, pool_ckv, pool_kpe_t, page_table, w, batch, t_new, kpm):
    n_pages = page_table.shape[1]
    pt = page_table.reshape(-1)
    tok = lambda width: pl.BlockSpec((None, t_new, width), lambda b, pt_ref: (b, 0, 0))
    g3 = lambda a: a.reshape(batch, t_new, a.shape[-1])
    const = lambda a: pl.BlockSpec(a.shape, lambda b, pt_ref: (0,) * a.ndim)
    hbm = pl.BlockSpec(memory_space=pl.ANY)
    grid_spec = pltpu.PrefetchScalarGridSpec(
        num_scalar_prefetch=1,
        grid=(batch,),
        in_specs=[tok(MLA_H * LANES), const(w['w_kq']), const(w['w_kt']), const(w['w_v']),
                  tok(MLA_KVR), tok(MLA_ROPE), hbm, hbm],
        out_specs=tok(MLA_H * MLA_DV),
        scratch_shapes=[pltpu.VMEM((2, n_pages, PAGE, MLA_KVR), F32),
                        pltpu.VMEM((2, n_pages, MLA_ROPE, PAGE), F32),
                        pltpu.SemaphoreType.DMA((2, 2))],
    )
    om = pl.pallas_call(
        functools.partial(_mla_sample_kernel, n_pages=n_pages, t_new=t_new, kpm=kpm),
        grid_spec=grid_spec,
        out_shape=jax.ShapeDtypeStruct((batch, t_new, MLA_H * MLA_DV), BF16),
        compiler_params=_cparams(("arbitrary",)),
        name="mla_sample",
    )(pt, g3(mq), w['w_kq'], w['w_kt'], w['w_v'], g3(ckv_new), g3(kp_new), pool_ckv, pool_kpe_t)
    return om.reshape(batch * t_new, MLA_H * MLA_DV)


def _mix_out_kernel(x_ref, og_ref, om_ref, wo_ref, gx_ref, wxq_ref, gxq_ref, x1_o, xq_o):
    x1 = x_ref[...] + _dot(og_ref[...], wo_ref[0:512, :]) + _dot(om_ref[...], wo_ref[512:1024, :])
    x1_o[...] = x1
    qf = _dot(_rms(x1, gx_ref[...]).astype(BF16), wxq_ref[...])
    g = gxq_ref[...]
    for h in range(X_H):
        hs = slice(X_D * h, X_D * (h + 1))
        xq_o[:, hs] = _rms(qf[:, hs], g).astype(BF16)


def _mix_out(x, og, om, w, tm):
    n = x.shape[0]
    row = lambda width: pl.BlockSpec((tm, width), lambda i: (i, 0))
    consts = [w['w_out'], w['g_x'], w['w_xq'], w['g_xq']]
    return pl.pallas_call(
        _mix_out_kernel,
        grid=(n // tm,),
        in_specs=[row(D_MODEL), row(512), row(512)] + [_full(c.shape) for c in consts],
        out_specs=[row(D_MODEL), row(X_H * X_D)],
        out_shape=[jax.ShapeDtypeStruct((n, D_MODEL), F32), jax.ShapeDtypeStruct((n, X_H * X_D), BF16)],
        compiler_params=_cparams(("parallel",)),
        name="mix_out",
    )(x, og, om, *consts)


def _cross_kernel(xq_ref, mk_ref, mv_ref, o_ref, *, n_mem):
    gb = xq_ref.shape[0]

    def head(ref, g, h):
        if len(ref.shape) == 4:
            return ref[g, h]
        return ref[pl.ds(g * n_mem * X_H + h, n_mem, stride=X_H), :].astype(BF16)

    units = [(g, h) for g in range(gb) for h in range(X_H)]
    ss = [_dot_nt(xq_ref[g, :, X_D * h:X_D * (h + 1)], head(mk_ref, g, h)) for g, h in units]
    ps = [jnp.exp2(s - jnp.max(s, axis=-1, keepdims=True)) for s in ss]
    for (g, h), p in zip(units, ps):
        o = _dot(p.astype(BF16), head(mv_ref, g, h))
        o_ref[g, :, X_D * h:X_D * (h + 1)] = (o * (1.0 / jnp.sum(p, axis=-1, keepdims=True))).astype(BF16)


def _cross(xq, mk, mv, batch, seq, n_mem, tq, gb):
    nq = seq // tq
    assert gb == 1 or nq == 1
    row = pl.BlockSpec((gb, tq, X_H * X_D), lambda b, i: (b * nq + i, 0, 0))
    if mk.ndim == 4:
        mem = pl.BlockSpec((gb,) + mk.shape[1:], lambda b, i: (b, 0, 0, 0))
    else:
        mem = pl.BlockSpec((gb * n_mem * X_H, X_D), lambda b, i: (b, 0))
    o = pl.pallas_call(
        functools.partial(_cross_kernel, n_mem=n_mem),
        grid=(batch // gb, nq),
        in_specs=[row, mem, mem],
        out_specs=row,
        out_shape=jax.ShapeDtypeStruct((batch * nq, tq, X_H * X_D), BF16),
        compiler_params=_cparams(("parallel", "arbitrary")),
        name="cross",
    )(xq.reshape(batch * nq, tq, X_H * X_D), mk, mv)
    return o.reshape(batch * seq, X_H * X_D)


def _ffn_kernel(x_ref, o_ref, wxo_ref, g_ref, w1_ref, w2_ref, y_o, *, fc):
    x = x_ref[...] + _dot(o_ref[...], wxo_ref[...])
    h = _rms(x, g_ref[...]).astype(BF16)
    acc = x
    for c in range(D_FF // fc):
        u = jnp.maximum(_dot(h, w1_ref[:, fc * c:fc * (c + 1)]), 0.0)
        acc = acc + _dot((u * u).astype(BF16), w2_ref[fc * c:fc * (c + 1), :])
    y_o[...] = acc


def _ffn(x, o, w, tm, fc):
    n = x.shape[0]
    row = lambda width: pl.BlockSpec((tm, width), lambda i: (i, 0))
    consts = [w['w_xo'], w['g_ff'], w['w_ff1'], w['w_ff2']]
    return pl.pallas_call(
        functools.partial(_ffn_kernel, fc=fc),
        grid=(n // tm,),
        in_specs=[row(D_MODEL), row(X_H * X_D)] + [_full(c.shape) for c in consts],
        out_specs=row(D_MODEL),
        out_shape=jax.ShapeDtypeStruct((n, D_MODEL), F32),
        compiler_params=_cparams(("parallel",)),
        name="ffn",
    )(x, o, *consts)


def _mem_kv_kernel(m_ref, gm_ref, wk_ref, wv_ref, gk_ref, k_o, v_o, kh_o, vh_o, *, n_mem):
    tm = m_ref.shape[0]
    m = _rms(m_ref[...], gm_ref[...]).astype(BF16)
    kf = _dot(m, wk_ref[...])
    vf = _dot(m, wv_ref[...])
    g = gk_ref[...]
    for h in range(X_H):
        hs = slice(X_D * h, X_D * (h + 1))
        kn = _rms(kf[:, hs], g)
        k_o[pl.ds(h, tm, stride=X_H), :] = kn
        v_o[pl.ds(h, tm, stride=X_H), :] = vf[:, hs]
        for b in range(kh_o.shape[0]):
            rs = slice(n_mem * b, n_mem * (b + 1))
            kh_o[b, h] = kn[rs].astype(BF16)
            vh_o[b, h] = vf[rs, hs].astype(BF16)


def _mem_kv(mem, w, n_mem, tm):
    n = mem.shape[0]
    bt = tm // n_mem
    row = lambda width: pl.BlockSpec((tm, width), lambda i: (i, 0))
    head = pl.BlockSpec((bt, X_H, n_mem, X_D), lambda i: (i, 0, 0, 0))
    consts = [w['g_mem'], w['w_xk'], w['w_xv'], w['g_xk']]
    return pl.pallas_call(
        functools.partial(_mem_kv_kernel, n_mem=n_mem),
        grid=(n // tm,),
        in_specs=[row(D_MODEL)] + [_full(c.shape) for c in consts],
        out_specs=[pl.BlockSpec((tm * X_H, X_D), lambda i: (i, 0))] * 2 + [head, head],
        out_shape=[jax.ShapeDtypeStruct((n * X_H, X_D), F32)] * 2
                  + [jax.ShapeDtypeStruct((n // n_mem, X_H, n_mem, X_D), BF16)] * 2,
        compiler_params=_cparams(("parallel",)),
        name="mem_kv",
    )(mem, *consts)


def _prep_weights(g_mix, w_in, w_gla_a2, b_gla_a, g_gla_o, g_mla_qa, w_mla_qb, g_mla_kva, w_mla_kvb,
                  g_q_nope, g_k_nope, g_q_rope, g_k_rope, w_out, g_x, g_mem, w_xq, w_xk, w_xv, g_xq, g_xk,
                  w_xo, g_ff, w_ff1, w_ff2):
    rowv = lambda g: g.reshape(1, -1).astype(F32)
    zc = lambda n: jnp.zeros((D_MODEL, n), F32)
    sizes = (256, 256, 512, 512, GLA_RANK, MLA_QR, MLA_KVR, MLA_ROPE)
    offs = [0]
    for s in sizes:
        offs.append(offs[-1] + s)
    q, k, v, r, a, qa, kva, kpe = [w_in[:, offs[i]:offs[i + 1]] for i in range(8)]
    w_in_p = jnp.concatenate(
        [q * (GLA_K ** -0.5), k, v, r, qa,
         a, zc(_KPE_LANE - GLA_RANK), kpe, zc(LANES - _KPE_LANE - MLA_ROPE), kva], axis=1)
    w_a2 = jnp.concatenate([w_gla_a2, jnp.zeros((LANES - GLA_RANK, GLA_H * GLA_K), F32)], axis=0)

    qb = w_mla_qb.reshape(MLA_QR, MLA_H, MLA_NOPE + MLA_ROPE)
    qb = jnp.pad(qb, ((0, 0), (0, 0), (0, LANES - MLA_NOPE - MLA_ROPE))).reshape(MLA_QR, MLA_H * LANES)
    kvb = w_mla_kvb.reshape(MLA_KVR, MLA_H, MLA_NOPE + MLA_DV)
    wk = kvb[:, :, :MLA_NOPE]
    wk_p = jnp.pad(wk, ((0, 0), (0, 0), (0, LANES - MLA_NOPE))).reshape(MLA_KVR, MLA_H * LANES)
    wv = kvb[:, :, MLA_NOPE:].reshape(MLA_KVR, MLA_H * MLA_DV)
    z32 = jnp.zeros((LANES - MLA_NOPE - MLA_ROPE,), F32)
    gq_row = jnp.tile(jnp.concatenate([g_q_nope, g_q_rope, z32]), MLA_H) * (MLA_SCALE * LOG2E)
    gk_row = jnp.tile(jnp.concatenate([g_k_nope, jnp.zeros((LANES - MLA_NOPE,), F32)]), MLA_H)
    g_kpe = jnp.concatenate([jnp.zeros((_KPE_LANE,), F32), g_k_rope, z32])

    li = jnp.arange(2 * LANES)
    seg = jnp.where(li % LANES < MLA_NOPE, 0, jnp.where(li % LANES < MLA_NOPE + MLA_ROPE, 1, 2))
    same = (li[:, None] // LANES == li[None, :] // LANES) & (seg[:, None] == seg[None, :]) & (seg[:, None] < 2)
    bd = jnp.where(same, jnp.where(seg[:, None] == 0, 1.0 / MLA_NOPE, 1.0 / MLA_ROPE), 0.0)

    wkq = jnp.transpose(wk, (1, 2, 0)) * g_k_nope[None, :, None]
    wkq = jnp.pad(wkq, ((0, 0), (0, LANES - MLA_NOPE), (0, 0)))
    wkt = jnp.transpose(wk, (1, 2, 0)).reshape(MLA_H * MLA_NOPE, MLA_KVR)

    hk = jnp.arange(GLA_H * GLA_K) // GLA_K
    hv = jnp.arange(GLA_H * GLA_V) // GLA_V
    bexp = (hk[:, None] == hv[None, :])
    return {
        'g_mix': rowv(g_mix), 'w_in': w_in_p.astype(BF16), 'w_a2': w_a2.astype(BF16), 'b_a': rowv(b_gla_a),
        'g_qa': rowv(g_mla_qa), 'w_qb': qb.astype(BF16), 'gq_row': rowv(gq_row), 'g_kva': rowv(g_mla_kva),
        'w_kv': jnp.concatenate([wk_p, wv], axis=1).astype(BF16), 'gk_row': rowv(gk_row),
        'w_v': wv.astype(BF16), 'g_kpe': rowv(g_kpe),
        'bd': bd.astype(BF16), 'w_kq': wkq.astype(BF16), 'w_kt': wkt.astype(BF16),
        'g_gla_o': rowv(g_gla_o), 'bexp': bexp.astype(BF16),
        'w_out': w_out.astype(BF16), 'g_x': rowv(g_x), 'w_xq': w_xq.astype(BF16),
        'g_xq': rowv(g_xq) * (X_SCALE * LOG2E),
        'g_mem': rowv(g_mem), 'w_xk': w_xk.astype(BF16), 'w_xv': w_xv.astype(BF16), 'g_xk': rowv(g_xk),
        'w_xo': w_xo.astype(BF16), 'g_ff': rowv(g_ff), 'w_ff1': w_ff1.astype(BF16), 'w_ff2': w_ff2.astype(BF16),
    }


def _rope_tables(pos):
    half = MLA_ROPE // 2
    inv = ROPE_THETA ** (-jnp.arange(half, dtype=F32) / half)
    ang = pos.astype(F32)[:, None] * inv[None, :]
    cos, sin = jnp.cos(ang), jnp.sin(ang)
    n = pos.shape[0]
    one = jnp.ones((n, _KPE_LANE), F32)
    z = lambda w_: jnp.zeros((n, w_), F32)
    tail = LANES - _KPE_LANE - MLA_ROPE
    c = jnp.concatenate([one, cos, cos, jnp.ones((n, tail), F32)], axis=1)
    s_up = jnp.concatenate([z(_KPE_LANE), -sin, z(half), z(tail)], axis=1)
    s_dn = jnp.concatenate([z(_KPE_LANE), z(half), sin, z(tail)], axis=1)
    return c, s_up, s_dn


def _tile_rows(n, cap):
    t = min(n, cap)
    while n % t:
        t //= 2
    return t


def _tail(x1, xq, mk, mv, w, batch, seq, n_mem):
    tq = _tile_rows(seq, 512)
    gb = _tile_rows(batch, max(1, CROSS_ROWS // seq)) if tq == seq else 1
    o = _cross(xq, mk, mv, batch, seq, n_mem, tq, gb)
    return _ffn(x1, o, w, _tile_rows(x1.shape[0], 512), 1024)


def _prompt_layer(x, mem, w):
    batch, seq, _ = x.shape
    n = batch * seq
    xf = x.reshape(n, D_MODEL)
    tm = _tile_rows(seq, 512)
    tabs = _rope_tables(jnp.arange(seq))
    gq, gk, gv, gr, la, mq, ckv, kp, mk_, mvt = _in_proj(xf, w, tabs, tm, True)
    c_len = GLA_CHUNK
    cps = _tile_rows(seq // c_len, GLA_CHUNKS_PER_STEP)
    s0 = jnp.zeros((batch, GLA_H * GLA_K, GLA_V), F32)
    og, s_fin = _gla(gq, gk, gv, gr, la, s0, w, batch, seq, c_len, min(GLA_SUB, c_len), cps,
                     _tile_rows(batch, GLA_SEQS_PER_STEP))
    om = _mla_prompt(mq, mk_, mvt, batch, seq, ATT_TILE)
    x1, xq = _mix_out(xf, og, om, w, tm)
    n_mem = mem.shape[1]
    memf = mem.reshape(-1, D_MODEL)
    xk, xv, xkh, xvh = _mem_kv(memf, w, n_mem, max(n_mem, _tile_rows(memf.shape[0], 512)))
    y = _tail(x1, xq, xkh, xvh, w, batch, seq, n_mem)
    return y, ckv, kp, xk, xv, s_fin


def _sample_layer(x, pool_ckv, pool_kpe, page_table, mem_k, mem_v, s_prev, w):
    batch, seq, _ = x.shape
    n = batch * seq
    xf = x.reshape(n, D_MODEL)
    tm = _tile_rows(n, 512)
    past = page_table.shape[1] * PAGE
    pos = past + (jnp.arange(tm) % seq)
    gq, gk, gv, gr, la, mq, ckv, kp = _in_proj(xf, w, _rope_tables(pos), tm, False)
    s0 = s_prev.reshape(batch, GLA_H * GLA_K, GLA_V)
    og, s_fin = _gla(gq, gk, gv, gr, la, s0, w, batch, seq, seq, seq, 1, _tile_rows(batch, GLA_SHORT_SEQS_PER_STEP))
    om = _mla_sample(mq, ckv, kp, pool_ckv, jnp.swapaxes(pool_kpe, 1, 2), page_table, w, batch, seq,
                     _tile_rows(page_table.shape[1], SAMPLE_PAGES_PER_DOT))
    x1, xq = _mix_out(xf, og, om, w, tm)
    y = _tail(x1, xq, mem_k.reshape(-1, X_D), mem_v.reshape(-1, X_D), w, batch, seq, mem_k.shape[1])
    return y, ckv, kp, s_fin


def kernel(x_prompt, x_sample, mem_prompt, cache_ckv, cache_kpe, cache_mem_k, cache_mem_v, state_gla, page_table, g_mix, w_in, w_gla_a2, b_gla_a, g_gla_o, g_mla_qa, w_mla_qb, g_mla_kva, w_mla_kvb, g_q_nope, g_k_nope, g_q_rope, g_k_rope, w_out, g_x, g_mem, w_xq, w_xk, w_xv, g_xq, g_xk, w_xo, g_ff, w_ff1, w_ff2):
    depth = w_in.shape[0]
    assert depth == 1, "one layer: prompt-group caches of layer l would feed layer l+1 otherwise unchanged"
    params = (g_mix, w_in, w_gla_a2, b_gla_a, g_gla_o, g_mla_qa, w_mla_qb, g_mla_kva, w_mla_kvb,
              g_q_nope, g_k_nope, g_q_rope, g_k_rope, w_out, g_x, g_mem, w_xq, w_xk, w_xv, g_xq, g_xk,
              w_xo, g_ff, w_ff1, w_ff2)
    w = _prep_weights(*[p[0] for p in params])
    bp, tp, _ = x_prompt.shape
    bs, tsq, _ = x_sample.shape
    yp, ckv_p, kp_p, xk, xv, gla_p = _prompt_layer(x_prompt, mem_prompt, w)
    ys, ckv_s, kp_s, gla_s = _sample_layer(x_sample, cache_ckv.reshape(cache_ckv.shape[1:]), cache_kpe.reshape(cache_kpe.shape[1:]), page_table,
                                            cache_mem_k[0], cache_mem_v[0], state_gla[0], w)
    n_mem = mem_prompt.shape[1]
    return (yp.reshape(bp, tp, D_MODEL), ys.reshape(bs, tsq, D_MODEL),
            ckv_p.reshape(1, bp, tp, MLA_KVR), kp_p.reshape(1, bp, tp, MLA_ROPE),
            xk.reshape(1, bp, n_mem, X_H, X_D), xv.reshape(1, bp, n_mem, X_H, X_D),
            gla_p.reshape(1, bp, GLA_H, GLA_K, GLA_V),
            ckv_s.reshape(1, bs, tsq, MLA_KVR), kp_s.reshape(1, bs, tsq, MLA_ROPE),
            gla_s.reshape(1, bs, GLA_H, GLA_K, GLA_V))
```

```python
import functools

import jax
import jax.numpy as jnp
from jax import lax
from jax.experimental import pallas as pl
from jax.experimental.pallas import tpu as pltpu

F32 = jnp.float32
BF16 = jnp.bfloat16

EPS = 1e-6
D_MODEL = 1024
GLA_H, GLA_K, GLA_V, GLA_RANK, GLA_TAU, GLA_CHUNK = 4, 64, 128, 16, 16.0, 64
GLA_SUB = 16
GLA_CHUNKS_PER_STEP = 4
GLA_SEQS_PER_STEP = 4
GLA_SHORT_SEQS_PER_STEP = 8
GLA_FAST_MAX = 40.0
MLA_H, MLA_DV, MLA_NOPE, MLA_ROPE, MLA_QR, MLA_KVR = 8, 64, 64, 32, 384, 256
MLA_SCALE = (MLA_NOPE + MLA_ROPE) ** -0.5
ROPE_THETA = 10000.0
X_H, X_D = 4, 128
X_SCALE = X_D ** -0.5
D_FF = 4096
PAGE = 128
LANES = 128
NEG = -1e30
LOG2E = 1.4426950408889634
ATT_TILE = 256
ATT_VROWS = 80
ATT_GROUP = 2
CROSS_ROWS = 32
SAMPLE_PAGES_PER_DOT = 4
SAMPLE_GROUPS = 4

_C_Q, _C_K, _C_V, _C_R = 0, 256, 512, 1024
_C_QA, _C_AK, _C_KVA, _C_END = 1536, 1920, 2048, 2304
_KPE_LANE = 64

VMEM_LIMIT = 56 * 1024 * 1024


def _cparams(sem):
    return pltpu.CompilerParams(dimension_semantics=sem, vmem_limit_bytes=VMEM_LIMIT)


def _dot(a, b):
    return jnp.dot(a, b, preferred_element_type=F32)


def _dot_nt(a, b):
    return lax.dot_general(a, b, (((1,), (1,)), ((), ())), preferred_element_type=F32)


def _rms(x, g):
    return x * lax.rsqrt(jnp.mean(x * x, axis=-1, keepdims=True) + EPS) * g


def _full(shape):
    n = len(shape)
    return pl.BlockSpec(shape, lambda *_: (0,) * n)


def _in_proj_kernel(x_ref, gmix_ref, win_ref, wa2_ref, ba_ref, gqa_ref, wqb_ref, gqrow_ref,
                    gkva_ref, wkv_ref, gkrow_ref, gkpe_ref, bd_ref, c_ref, s1_ref, s2_ref,
                    gq_o, gk_o, gv_o, gr_o, la_o, mq_o, ckv_o, kp_o, *kv_o):
    h = _rms(x_ref[...], gmix_ref[...]).astype(BF16)

    z2 = _dot(h, win_ref[:, _C_QA:_C_END])
    qa = z2[:, 0:_C_AK - _C_QA]
    zak = z2[:, _C_AK - _C_QA:_C_KVA - _C_QA]
    kva = z2[:, _C_KVA - _C_QA:]

    gate = _dot(zak.astype(BF16), wa2_ref[...]) + ba_ref[...]
    la_o[...] = (jnp.minimum(gate, 0.0) - jnp.log1p(jnp.exp(-jnp.abs(gate)))) * (1.0 / GLA_TAU)

    cos, s_up, s_dn = c_ref[...], s1_ref[...], s2_ref[...]

    def rope(t):
        return t * cos + pltpu.roll(t, LANES - 16, 1) * s_up + pltpu.roll(t, 16, 1) * s_dn

    lane = lax.broadcasted_iota(jnp.int32, (1, LANES), 1)
    zk = jnp.where((lane >= _KPE_LANE) & (lane < _KPE_LANE + MLA_ROPE), zak, 0.0)
    kpn = zk * lax.rsqrt(jnp.sum(zk * zk, axis=-1, keepdims=True) * (1.0 / MLA_ROPE) + EPS) * gkpe_ref[...]
    kpt = rope(kpn)
    kp_o[...] = pltpu.roll(kpt, LANES - _KPE_LANE, 1)[:, 0:MLA_ROPE]

    ckv = _rms(kva, gkva_ref[...])
    ckv_o[...] = ckv
    cb = ckv.astype(BF16)
    bd = bd_ref[...]

    def seg_norm(t, grow):
        ms = _dot((t * t).astype(BF16), bd)
        return t * lax.rsqrt(ms + EPS) * grow

    qf = _dot(_rms(qa, gqa_ref[...]).astype(BF16), wqb_ref[...])
    for c in range(4):
        sl = slice(2 * LANES * c, 2 * LANES * (c + 1))
        qn = seg_norm(qf[:, sl], gqrow_ref[:, sl])
        for j in range(2):
            o = 2 * LANES * c + LANES * j
            mq_o[:, o:o + LANES] = rope(qn[:, LANES * j:LANES * (j + 1)]).astype(BF16)

    if kv_o:
        mk_o, mvt_o = kv_o
        kvf = _dot(cb, wkv_ref[...])
        kf, mv = kvf[:, 0:MLA_H * LANES], kvf[:, MLA_H * LANES:]
        ones = jnp.ones((ATT_VROWS - MLA_DV, ATT_TILE), BF16)
        for c in range(mvt_o.shape[0]):
            mvt = mv[ATT_TILE * c:ATT_TILE * (c + 1), :].T.astype(BF16)
            for hh in range(MLA_H):
                mvt_o[c, ATT_VROWS * hh:ATT_VROWS * hh + MLA_DV, :] = mvt[MLA_DV * hh:MLA_DV * (hh + 1)]
                mvt_o[c, ATT_VROWS * hh + MLA_DV:ATT_VROWS * (hh + 1), :] = ones
        for c in range(4):
            sl = slice(2 * LANES * c, 2 * LANES * (c + 1))
            kn = seg_norm(kf[:, sl], gkrow_ref[:, sl])
            for j in range(2):
                o = 2 * LANES * c + LANES * j
                mk_o[:, o:o + LANES] = (kn[:, LANES * j:LANES * (j + 1)] + kpt).astype(BF16)

    z1 = _dot(h, win_ref[:, _C_Q:_C_QA])
    gq_o[...] = z1[:, _C_Q:_C_K].astype(BF16)
    gk_o[...] = z1[:, _C_K:_C_V].astype(BF16)
    gv_o[...] = z1[:, _C_V:_C_R].astype(BF16)
    gr_o[...] = z1[:, _C_R:_C_QA].astype(BF16)


def _in_proj(x, w, tabs, tm, with_kv):
    n = x.shape[0]
    nt = tabs[0].shape[0] // tm
    row = lambda width: pl.BlockSpec((tm, width), lambda i: (i, 0))
    tab = pl.BlockSpec((tm, LANES), lambda i: (i % nt, 0))
    consts = [w['g_mix'], w['w_in'], w['w_a2'], w['b_a'], w['g_qa'], w['w_qb'], w['gq_row'],
              w['g_kva'], w['w_kv'], w['gk_row'], w['g_kpe'], w['bd']]
    out_w = [(256, BF16), (256, BF16), (512, BF16), (512, BF16), (256, F32),
             (1024, BF16), (256, F32), (MLA_ROPE, F32)] + ([(1024, BF16)] if with_kv else [])
    out_specs = [row(wd) for wd, _ in out_w]
    out_shape = [jax.ShapeDtypeStruct((n, wd), dt) for wd, dt in out_w]
    if with_kv:
        out_specs.append(pl.BlockSpec((tm // ATT_TILE, MLA_H * ATT_VROWS, ATT_TILE), lambda i: (i, 0, 0)))
        out_shape.append(jax.ShapeDtypeStruct((n // ATT_TILE, MLA_H * ATT_VROWS, ATT_TILE), BF16))
    return pl.pallas_call(
        _in_proj_kernel,
        grid=(n // tm,),
        in_specs=[row(D_MODEL)] + [_full(c.shape) for c in consts] + [tab, tab, tab],
        out_specs=out_specs,
        out_shape=out_shape,
        compiler_params=_cparams(("parallel",)),
        name="in_proj",
    )(x, *consts, *tabs)


def _head_masks():
    lane = lax.broadcasted_iota(jnp.int32, (1, GLA_H * GLA_K), 1)
    return [(lane // GLA_K == h).astype(F32) for h in range(GLA_H)]


def _gla_fast_step(qs, ks, vs, bs, sts, keep, c_len, cps):
    masks = _head_masks()
    ts = c_len * cps
    n = len(qs)
    ebs = [jnp.exp(b) for b in bs]
    kts = [k * jnp.exp(-b) for k, b in zip(ks, bs)]
    lqs = [jnp.concatenate([q * eb * m for m in masks], axis=0).astype(BF16)
           for q, eb in zip(qs, ebs)]
    atts = [_dot_nt(lq, kt.astype(BF16)) for lq, kt in zip(lqs, kts)]
    vbs = [v.astype(BF16) for v in vs]
    o_intra = [[_dot((atts[g][ts * h:ts * (h + 1)] * keep).astype(BF16), vbs[g][:, GLA_V * h:GLA_V * (h + 1)])
                for h in range(GLA_H)] for g in range(n)]
    o_inter = [[] for _ in range(n)]
    sts = list(sts)
    for c in range(cps):
        cs = slice(c_len * c, c_len * (c + 1))
        for g in range(n):
            lqc = jnp.concatenate([lqs[g][ts * h + c_len * c:ts * h + c_len * (c + 1)] for h in range(GLA_H)],
                                  axis=0)
            o_inter[g].append(_dot_nt(lqc, sts[g].astype(BF16)))
            ebl = ebs[g][c_len * (c + 1) - 1:c_len * (c + 1), :]
            kl = kts[g][cs] * ebl
            vst = jnp.concatenate([vs[g][cs, GLA_V * h:GLA_V * (h + 1)] for h in range(GLA_H)], axis=0)
            kst = jnp.concatenate([kl * m for m in masks], axis=0).astype(BF16)
            sts[g] = sts[g] * ebl + _dot(vst.T.astype(BF16), kst)
    outs = [[o_intra[g][h] + jnp.concatenate([o_inter[g][c][c_len * h:c_len * (h + 1)] for c in range(cps)],
                                             axis=0) for h in range(GLA_H)] for g in range(n)]
    return outs, sts


def _gla_chunk(q, k, v, b, st, bexp, c_len, sub):
    masks = _head_masks()
    bl = b[c_len - 1:c_len, :]

    qh = q * jnp.exp(b)
    lq = jnp.concatenate([qh * m for m in masks], axis=0).astype(BF16)
    o_inter = _dot_nt(lq, st.astype(BF16))

    nsub = c_len // sub
    a_rows = [[] for _ in range(GLA_H)]
    col = lax.broadcasted_iota(jnp.int32, (1, c_len), 1)
    for i in range(1, nsub):
        ref = b[sub * i - 1:sub * i, :]
        qi = q[sub * i:sub * (i + 1)] * jnp.exp(b[sub * i:sub * (i + 1)] - ref)
        ki = k * jnp.exp(jnp.minimum(ref - b, 0.0))
        li = jnp.concatenate([qi * m for m in masks], axis=0).astype(BF16)
        ai = jnp.where(col < sub * i, _dot_nt(li, ki.astype(BF16)), 0.0)
        for h in range(GLA_H):
            a_rows[h].append(ai[sub * h:sub * (h + 1)])

    row = lax.broadcasted_iota(jnp.int32, (sub, 1), 0)
    o_diag = []
    for i in range(nsub):
        sl = slice(sub * i, sub * (i + 1))
        bb, qb, kb, vb = b[sl], q[sl], k[sl], v[sl]
        ps = []
        for s in range(sub):
            e = jnp.exp(jnp.where(row >= s, bb - bb[s:s + 1, :], NEG))
            ps.append(qb * e * kb[s:s + 1, :])
        r = _dot(jnp.concatenate(ps, axis=0).astype(BF16), bexp)
        od = r[0:sub] * vb[0:1, :]
        for s in range(1, sub):
            od = od + r[sub * s:sub * (s + 1)] * vb[s:s + 1, :]
        o_diag.append(od)
    o_diag = jnp.concatenate(o_diag, axis=0) if nsub > 1 else o_diag[0]

    outs = []
    for h in range(GLA_H):
        oh = o_inter[c_len * h:c_len * (h + 1)] + o_diag[:, GLA_V * h:GLA_V * (h + 1)]
        if nsub > 1:
            ah = jnp.concatenate([jnp.zeros((sub, c_len), F32)] + a_rows[h], axis=0)
            oh = oh + _dot(ah.astype(BF16), v[:, GLA_V * h:GLA_V * (h + 1)].astype(BF16))
        outs.append(oh)

    kl = k * jnp.exp(bl - b)
    vs = jnp.concatenate([v[:, GLA_V * h:GLA_V * (h + 1)] for h in range(GLA_H)], axis=0)
    ks = jnp.concatenate([kl * m for m in masks], axis=0).astype(BF16)
    st_new = st * jnp.exp(bl) + _dot(vs.T.astype(BF16), ks)
    return outs, st_new


def _gla_kernel(*refs, c_len, sub, cps, fast):
    if fast:
        (q_ref, k_ref, v_ref, r_ref, la_ref, s0_ref, g_ref, bexp_ref, tri_ref, keep_ref,
         og_o, sf_o, st_ref, b_ref) = refs
    else:
        q_ref, k_ref, v_ref, r_ref, la_ref, s0_ref, g_ref, bexp_ref, og_o, sf_o, st_ref, b_ref = refs
    t = pl.program_id(1)
    gb = q_ref.shape[0]

    @pl.when(t == 0)
    def _():
        for i in range(gb):
            st_ref[i] = s0_ref[i].T

    g = g_ref[...]

    def emit(i, outs, rs):
        rr = r_ref[i, rs, :].astype(F32)
        for h in range(GLA_H):
            hs = slice(GLA_V * h, GLA_V * (h + 1))
            rh = rr[:, hs]
            og_o[i, rs, hs] = (_rms(outs[h], g) * (rh / (1.0 + jnp.exp(-rh)))).astype(BF16)

    bs = []
    for i in range(gb):
        la = la_ref[i]
        if fast:
            la_hi = la.astype(BF16)
            la_lo = (la - la_hi.astype(F32)).astype(BF16)
            b = _dot(tri_ref[...], la_hi) + _dot(tri_ref[...], la_lo)
        else:
            rows = lax.broadcasted_iota(jnp.int32, (c_len, 1), 0)
            b = jnp.zeros_like(la)
            for j in range(c_len):
                b = b + jnp.where(rows >= j, la[j:j + 1, :], 0.0)
        b_ref[i] = b
        bs.append(b)

    def robust():
        bexp = bexp_ref[...]
        for i in range(gb):
            def body(c, carry, i=i):
                rs = pl.ds(pl.multiple_of(c * c_len, c_len), c_len)
                outs, st_new = _gla_chunk(q_ref[i, rs, :].astype(F32), k_ref[i, rs, :].astype(F32),
                                          v_ref[i, rs, :].astype(F32), b_ref[i, rs, :], st_ref[i], bexp, c_len, sub)
                st_ref[i] = st_new
                emit(i, outs, rs)
                return carry

            lax.fori_loop(0, cps, body, 0)

    if fast:
        low = jnp.min(bs[0])
        for b in bs[1:]:
            low = jnp.minimum(low, jnp.min(b))
        mild = low >= -GLA_FAST_MAX

        @pl.when(mild)
        def _():
            seqs = range(gb)
            outs, sts = _gla_fast_step([q_ref[i].astype(F32) for i in seqs], [k_ref[i].astype(F32) for i in seqs],
                                       [v_ref[i].astype(F32) for i in seqs], [b_ref[i] for i in seqs],
                                       [st_ref[i] for i in seqs], keep_ref[...], c_len, cps)
            for i in seqs:
                st_ref[i] = sts[i]
                emit(i, outs[i], slice(None))

        pl.when(jnp.logical_not(mild))(robust)
    else:
        robust()

    @pl.when(t == pl.num_programs(1) - 1)
    def _():
        for i in range(gb):
            sf_o[i] = st_ref[i].T


def _gla(gq, gk, gv, gr, la, s0, w, batch, seq, c_len, sub, cps, gb):
    ts = c_len * cps
    nt = seq // ts
    fast = c_len == GLA_CHUNK
    assert fast or cps == 1
    row = lambda width: pl.BlockSpec((gb, None, ts, width), lambda b, t: (b, t, 0, 0))
    g4 = lambda a: a.reshape(batch, nt, ts, a.shape[-1])
    st_spec = pl.BlockSpec((gb, GLA_H * GLA_K, GLA_V), lambda b, t: (b, 0, 0))
    consts = [w['g_gla_o'], w['bexp']]
    if fast:
        ti = jnp.arange(ts)
        keep = (ti[:, None] // c_len == ti[None, :] // c_len) & (ti[None, :] <= ti[:, None])
        consts += [keep.astype(BF16), keep.astype(F32)]
    og, s_fin = pl.pallas_call(
        functools.partial(_gla_kernel, c_len=c_len, sub=sub, cps=cps, fast=fast),
        grid=(batch // gb, nt),
        in_specs=[row(256), row(256), row(512), row(512), row(256), st_spec] + [_full(c.shape) for c in consts],
        out_specs=[row(512), st_spec],
        out_shape=[jax.ShapeDtypeStruct((batch, nt, ts, GLA_H * GLA_V), BF16),
                   jax.ShapeDtypeStruct((batch, GLA_H * GLA_K, GLA_V), F32)],
        scratch_shapes=[pltpu.VMEM((gb, GLA_V, GLA_H * GLA_K), F32), pltpu.VMEM((gb, ts, GLA_H * GLA_K), F32)],
        compiler_params=_cparams(("parallel", "arbitrary")),
        name="gla",
    )(g4(gq), g4(gk), g4(gv), g4(gr), g4(la), s0, *consts)
    return og.reshape(batch * seq, GLA_H * GLA_V), s_fin


def _mla_prompt_kernel(q_ref, k_ref, vt_ref, o_ref, m_ref, acc_ref, *, tq):
    i = pl.program_id(1)
    m_ref[...] = jnp.full(m_ref.shape, -jnp.inf, F32)
    acc_ref[...] = jnp.zeros(acc_ref.shape, F32)
    keep = (lax.broadcasted_iota(jnp.int32, (tq, tq), 0) <= lax.broadcasted_iota(jnp.int32, (tq, tq), 1))

    def tiles(js, last_masked):
        n = len(js)
        sts = [[_dot_nt(k_ref[pl.ds(pl.multiple_of(j * tq, tq), tq), LANES * h:LANES * (h + 1)],
                        q_ref[:, LANES * h:LANES * (h + 1)]) for j in js] for h in range(MLA_H)]
        ps, alphas = [], []
        for h in range(MLA_H):
            if last_masked:
                sts[h][-1] = jnp.where(keep, sts[h][-1], -jnp.inf)
            m_old = m_ref[h]
            m_new = m_old
            for st in sts[h]:
                m_new = jnp.maximum(m_new, jnp.max(st, axis=0, keepdims=True))
            alphas.append(jnp.exp2(m_old - m_new))
            ps.append([jnp.exp2(st - m_new).astype(BF16) for st in sts[h]])
            m_ref[h] = m_new
        for h in range(MLA_H):
            acc = alphas[h] * acc_ref[h]
            for t in range(n):
                acc = acc + _dot(vt_ref[js[t], ATT_VROWS * h:ATT_VROWS * (h + 1), :], ps[h][t])
            acc_ref[h] = acc

    def body(jj, carry):
        tiles([ATT_GROUP * jj + t for t in range(ATT_GROUP)], False)
        return carry

    lax.fori_loop(0, i // ATT_GROUP, body, 0)
    for rem in range(ATT_GROUP):
        pl.when(i % ATT_GROUP == rem)(functools.partial(tiles, [i - rem + t for t in range(rem + 1)], True))

    for pair in range(MLA_H // 2):
        h0, h1 = 2 * pair, 2 * pair + 1
        ab = jnp.concatenate([acc_ref[h, 0:MLA_DV, :] * (1.0 / acc_ref[h, MLA_DV:MLA_DV + 1, :]) for h in (h0, h1)],
                             axis=0)
        o_ref[:, LANES * pair:LANES * (pair + 1)] = ab.T.astype(BF16)


def _mla_prompt(mq, mk, mvt, batch, seq, tq):
    nq = seq // tq
    return pl.pallas_call(
        functools.partial(_mla_prompt_kernel, tq=tq),
        grid=(batch, nq),
        in_specs=[pl.BlockSpec((tq, MLA_H * LANES), lambda b, i: (b * nq + i, 0)),
                  pl.BlockSpec((seq, MLA_H * LANES), lambda b, i: (b, 0)),
                  pl.BlockSpec((nq, MLA_H * ATT_VROWS, tq), lambda b, i: (b, 0, 0))],
        out_specs=pl.BlockSpec((tq, MLA_H * MLA_DV), lambda b, i: (b * nq + i, 0)),
        out_shape=jax.ShapeDtypeStruct((batch * seq, MLA_H * MLA_DV), BF16),
        scratch_shapes=[pltpu.VMEM((MLA_H, 1, tq), F32), pltpu.VMEM((MLA_H, ATT_VROWS, tq), F32)],
        compiler_params=_cparams(("parallel", "arbitrary")),
        name="mla_prompt",
    )(mq, mk, mvt)


def _mla_sample_kernel(pt_ref, q_ref, wkq_ref, wkt_ref, wv_ref, cn_ref, kn_ref, ckv_hbm, kpe_hbm,
                       o_ref, ckv_buf, kpe_buf, sem, *, n_pages, t_new, kpm):
    b = pl.program_id(0)
    nb = pl.num_programs(0)
    half = b % 2
    nrow = MLA_H * t_new

    def page_copies(seq, hf, p):
        page = pt_ref[seq * n_pages + p]
        return (pltpu.make_async_copy(ckv_hbm.at[page], ckv_buf.at[hf, p], sem.at[0, hf]),
                pltpu.make_async_copy(kpe_hbm.at[page], kpe_buf.at[hf, p], sem.at[1, hf]))

    def start_pages(seq, hf, pages):
        for p in pages:
            for cp in page_copies(seq, hf, p):
                cp.start()

    def wait_pages(seq, hf, pages):
        for p in pages:
            for cp in page_copies(seq, hf, p):
                cp.wait()

    @pl.when(b == 0)
    def _():
        start_pages(0, 0, range(n_pages))

    nxt = (b + 1) % nb

    wait_pages(b, half, range(n_pages))

    qa, qr = [], []
    for h in range(MLA_H):
        qh = q_ref[:, LANES * h:LANES * (h + 1)]
        qa.append(_dot(qh, wkq_ref[h]))
        qr.append(pltpu.roll(qh.astype(F32), LANES - _KPE_LANE, 1)[:, 0:MLA_ROPE])
    lq = jnp.concatenate([jnp.concatenate(qa, axis=0).astype(BF16), wkt_ref[...]], axis=0)
    qp = jnp.concatenate(qr, axis=0).astype(BF16)

    def scores(r, s_rope):
        rinv = []
        for h in range(MLA_H):
            kr = r[nrow + MLA_NOPE * h:nrow + MLA_NOPE * (h + 1)]
            ms = jnp.sum(kr * kr, axis=0, keepdims=True) * (1.0 / MLA_NOPE)
            rinv.append(jnp.broadcast_to(lax.rsqrt(ms + EPS), (t_new, r.shape[1])))
        return r[0:nrow] * jnp.concatenate(rinv, axis=0) + s_rope

    ngrp = n_pages // kpm
    ppb = kpm * SAMPLE_GROUPS
    xs, ss = [], []
    for g0 in range(0, ngrp, SAMPLE_GROUPS):
        gs = range(g0, min(g0 + SAMPLE_GROUPS, ngrp))
        start_pages(nxt, 1 - half, range(g0 * kpm, min(g0 * kpm + ppb, n_pages)))
        xg = [jnp.concatenate([ckv_buf[half, g * kpm + j] for j in range(kpm)], axis=0).astype(BF16) for g in gs]
        rg = [_dot_nt(lq, x) for x in xg]
        pg = [_dot(qp, jnp.concatenate([kpe_buf[half, g * kpm + j] for j in range(kpm)], axis=1).astype(BF16))
              for g in gs]
        xs.append(jnp.concatenate(xg, axis=0))
        ss.append(jnp.concatenate([scores(r, s_rope) for r, s_rope in zip(rg, pg)], axis=1))

    xn = cn_ref[...].astype(BF16)
    sn = scores(_dot_nt(lq, xn), _dot_nt(qp, kn_ref[...].astype(BF16)))
    tq = lax.broadcasted_iota(jnp.int32, (nrow, t_new), 0) % t_new
    ts = lax.broadcasted_iota(jnp.int32, (nrow, t_new), 1)
    xs.append(xn)
    ss.append(jnp.where(ts <= tq, sn, -jnp.inf))

    ms = [jnp.max(s, axis=-1, keepdims=True) for s in ss]
    ps = [jnp.exp2(s - mi) for s, mi in zip(ss, ms)]
    ls = [jnp.sum(p, axis=-1, keepdims=True) for p in ps]
    pvs = [_dot(p.astype(BF16), x) for p, x in zip(ps, xs)]
    m = ms[0]
    for mi in ms[1:]:
        m = jnp.maximum(m, mi)
    ws = [jnp.exp2(mi - m) for mi in ms]
    l = ws[0] * ls[0]
    oa = ws[0] * pvs[0]
    for wi, li, pv in zip(ws[1:], ls[1:], pvs[1:]):
        l = l + wi * li
        oa = oa + wi * pv
    oa = oa / l

    r = _dot(oa.astype(BF16), wv_ref[...])
    lane = lax.broadcasted_iota(jnp.int32, (1, MLA_H * MLA_DV), 1)
    out = jnp.zeros((t_new, MLA_H * MLA_DV), F32)
    for h in range(MLA_H):
        out = out + jnp.where(lane // MLA_DV == h, r[t_new * h:t_new * (h + 1)], 0.0)
    o_ref[...] = out.astype(BF16)

    @pl.when(b == nb - 1)
    def _():
        wait_pages(nxt, 1 - half, range(n_pages))


def _mla_sample(mq, ckv_new, kp_new, pool_ckv, pool_kpe_t, page_table, w, batch, t_new, kpm):
    n_pages = page_table.shape[1]
    pt = page_table.reshape(-1)
    tok = lambda width: pl.BlockSpec((None, t_new, width), lambda b, pt_ref: (b, 0, 0))
    g3 = lambda a: a.reshape(batch, t_new, a.shape[-1])
    const = lambda a: pl.BlockSpec(a.shape, lambda b, pt_ref: (0,) * a.ndim)
    hbm = pl.BlockSpec(memory_space=pl.ANY)
    grid_spec = pltpu.PrefetchScalarGridSpec(
        num_scalar_prefetch=1,
        grid=(batch,),
        in_specs=[tok(MLA_H * LANES), const(w['w_kq']), const(w['w_kt']), const(w['w_v']),
                  tok(MLA_KVR), tok(MLA_ROPE), hbm, hbm],
        out_specs=tok(MLA_H * MLA_DV),
        scratch_shapes=[pltpu.VMEM((2, n_pages, PAGE, MLA_KVR), F32),
                        pltpu.VMEM((2, n_pages, MLA_ROPE, PAGE), F32),
                        pltpu.SemaphoreType.DMA((2, 2))],
    )
    om = pl.pallas_call(
        functools.partial(_mla_sample_kernel, n_pages=n_pages, t_new=t_new, kpm=kpm),
        grid_spec=grid_spec,
        out_shape=jax.ShapeDtypeStruct((batch, t_new, MLA_H * MLA_DV), BF16),
        compiler_params=_cparams(("arbitrary",)),
        name="mla_sample",
    )(pt, g3(mq), w['w_kq'], w['w_kt'], w['w_v'], g3(ckv_new), g3(kp_new), pool_ckv, pool_kpe_t)
    return om.reshape(batch * t_new, MLA_H * MLA_DV)


def _mix_out_kernel(x_ref, og_ref, om_ref, wo_ref, gx_ref, wxq_ref, gxq_ref, x1_o, xq_o):
    x1 = x_ref[...] + _dot(og_ref[...], wo_ref[0:512, :]) + _dot(om_ref[...], wo_ref[512:1024, :])
    x1_o[...] = x1
    qf = _dot(_rms(x1, gx_ref[...]).astype(BF16), wxq_ref[...])
    g = gxq_ref[...]
    for h in range(X_H):
        hs = slice(X_D * h, X_D * (h + 1))
        xq_o[:, hs] = _rms(qf[:, hs], g).astype(BF16)


def _mix_out(x, og, om, w, tm):
    n = x.shape[0]
    row = lambda width: pl.BlockSpec((tm, width), lambda i: (i, 0))
    consts = [w['w_out'], w['g_x'], w['w_xq'], w['g_xq']]
    return pl.pallas_call(
        _mix_out_kernel,
        grid=(n // tm,),
        in_specs=[row(D_MODEL), row(512), row(512)] + [_full(c.shape) for c in consts],
        out_specs=[row(D_MODEL), row(X_H * X_D)],
        out_shape=[jax.ShapeDtypeStruct((n, D_MODEL), F32), jax.ShapeDtypeStruct((n, X_H * X_D), BF16)],
        compiler_params=_cparams(("parallel",)),
        name="mix_out",
    )(x, og, om, *consts)


def _cross_kernel(xq_ref, mk_ref, mv_ref, o_ref, *, n_mem):
    gb = xq_ref.shape[0]

    def head(ref, g, h):
        if len(ref.shape) == 4:
            return ref[g, h]
        return ref[pl.ds(g * n_mem * X_H + h, n_mem, stride=X_H), :].astype(BF16)

    units = [(g, h) for g in range(gb) for h in range(X_H)]
    ss = [_dot_nt(xq_ref[g, :, X_D * h:X_D * (h + 1)], head(mk_ref, g, h)) for g, h in units]
    ps = [jnp.exp2(s - jnp.max(s, axis=-1, keepdims=True)) for s in ss]
    for (g, h), p in zip(units, ps):
        o = _dot(p.astype(BF16), head(mv_ref, g, h))
        o_ref[g, :, X_D * h:X_D * (h + 1)] = (o * (1.0 / jnp.sum(p, axis=-1, keepdims=True))).astype(BF16)


def _cross(xq, mk, mv, batch, seq, n_mem, tq, gb):
    nq = seq // tq
    assert gb == 1 or nq == 1
    row = pl.BlockSpec((gb, tq, X_H * X_D), lambda b, i: (b * nq + i, 0, 0))
    if mk.ndim == 4:
        mem = pl.BlockSpec((gb,) + mk.shape[1:], lambda b, i: (b, 0, 0, 0))
    else:
        mem = pl.BlockSpec((gb * n_mem * X_H, X_D), lambda b, i: (b, 0))
    o = pl.pallas_call(
        functools.partial(_cross_kernel, n_mem=n_mem),
        grid=(batch // gb, nq),
        in_specs=[row, mem, mem],
        out_specs=row,
        out_shape=jax.ShapeDtypeStruct((batch * nq, tq, X_H * X_D), BF16),
        compiler_params=_cparams(("parallel", "arbitrary")),
        name="cross",
    )(xq.reshape(batch * nq, tq, X_H * X_D), mk, mv)
    return o.reshape(batch * seq, X_H * X_D)


def _ffn_kernel(x_ref, o_ref, wxo_ref, g_ref, w1_ref, w2_ref, y_o, *, fc):
    x = x_ref[...] + _dot(o_ref[...], wxo_ref[...])
    h = _rms(x, g_ref[...]).astype(BF16)
    acc = x
    for c in range(D_FF // fc):
        u = jnp.maximum(_dot(h, w1_ref[:, fc * c:fc * (c + 1)]), 0.0)
        acc = acc + _dot((u * u).astype(BF16), w2_ref[fc * c:fc * (c + 1), :])
    y_o[...] = acc


def _ffn(x, o, w, tm, fc):
    n = x.shape[0]
    row = lambda width: pl.BlockSpec((tm, width), lambda i: (i, 0))
    consts = [w['w_xo'], w['g_ff'], w['w_ff1'], w['w_ff2']]
    return pl.pallas_call(
        functools.partial(_ffn_kernel, fc=fc),
        grid=(n // tm,),
        in_specs=[row(D_MODEL), row(X_H * X_D)] + [_full(c.shape) for c in consts],
        out_specs=row(D_MODEL),
        out_shape=jax.ShapeDtypeStruct((n, D_MODEL), F32),
        compiler_params=_cparams(("parallel",)),
        name="ffn",
    )(x, o, *consts)


def _mem_kv_kernel(m_ref, gm_ref, wk_ref, wv_ref, gk_ref, k_o, v_o, kh_o, vh_o, *, n_mem):
    tm = m_ref.shape[0]
    m = _rms(m_ref[...], gm_ref[...]).astype(BF16)
    kf = _dot(m, wk_ref[...])
    vf = _dot(m, wv_ref[...])
    g = gk_ref[...]
    for h in range(X_H):
        hs = slice(X_D * h, X_D * (h + 1))
        kn = _rms(kf[:, hs], g)
        k_o[pl.ds(h, tm, stride=X_H), :] = kn
        v_o[pl.ds(h, tm, stride=X_H), :] = vf[:, hs]
        for b in range(kh_o.shape[0]):
            rs = slice(n_mem * b, n_mem * (b + 1))
            kh_o[b, h] = kn[rs].astype(BF16)
            vh_o[b, h] = vf[rs, hs].astype(BF16)


def _mem_kv(mem, w, n_mem, tm):
    n = mem.shape[0]
    bt = tm // n_mem
    row = lambda width: pl.BlockSpec((tm, width), lambda i: (i, 0))
    head = pl.BlockSpec((bt, X_H, n_mem, X_D), lambda i: (i, 0, 0, 0))
    consts = [w['g_mem'], w['w_xk'], w['w_xv'], w['g_xk']]
    return pl.pallas_call(
        functools.partial(_mem_kv_kernel, n_mem=n_mem),
        grid=(n // tm,),
        in_specs=[row(D_MODEL)] + [_full(c.shape) for c in consts],
        out_specs=[pl.BlockSpec((tm * X_H, X_D), lambda i: (i, 0))] * 2 + [head, head],
        out_shape=[jax.ShapeDtypeStruct((n * X_H, X_D), F32)] * 2
                  + [jax.ShapeDtypeStruct((n // n_mem, X_H, n_mem, X_D), BF16)] * 2,
        compiler_params=_cparams(("parallel",)),
        name="mem_kv",
    )(mem, *consts)


def _prep_weights(g_mix, w_in, w_gla_a2, b_gla_a, g_gla_o, g_mla_qa, w_mla_qb, g_mla_kva, w_mla_kvb,
                  g_q_nope, g_k_nope, g_q_rope, g_k_rope, w_out, g_x, g_mem, w_xq, w_xk, w_xv, g_xq, g_xk,
                  w_xo, g_ff, w_ff1, w_ff2):
    rowv = lambda g: g.reshape(1, -1).astype(F32)
    zc = lambda n: jnp.zeros((D_MODEL, n), F32)
    sizes = (256, 256, 512, 512, GLA_RANK, MLA_QR, MLA_KVR, MLA_ROPE)
    offs = [0]
    for s in sizes:
        offs.append(offs[-1] + s)
    q, k, v, r, a, qa, kva, kpe = [w_in[:, offs[i]:offs[i + 1]] for i in range(8)]
    w_in_p = jnp.concatenate(
        [q * (GLA_K ** -0.5), k, v, r, qa,
         a, zc(_KPE_LANE - GLA_RANK), kpe, zc(LANES - _KPE_LANE - MLA_ROPE), kva], axis=1)
    w_a2 = jnp.concatenate([w_gla_a2, jnp.zeros((LANES - GLA_RANK, GLA_H * GLA_K), F32)], axis=0)

    qb = w_mla_qb.reshape(MLA_QR, MLA_H, MLA_NOPE + MLA_ROPE)
    qb = jnp.pad(qb, ((0, 0), (0, 0), (0, LANES - MLA_NOPE - MLA_ROPE))).reshape(MLA_QR, MLA_H * LANES)
    kvb = w_mla_kvb.reshape(MLA_KVR, MLA_H, MLA_NOPE + MLA_DV)
    wk = kvb[:, :, :MLA_NOPE]
    wk_p = jnp.pad(wk, ((0, 0), (0, 0), (0, LANES - MLA_NOPE))).reshape(MLA_KVR, MLA_H * LANES)
    wv = kvb[:, :, MLA_NOPE:].reshape(MLA_KVR, MLA_H * MLA_DV)
    z32 = jnp.zeros((LANES - MLA_NOPE - MLA_ROPE,), F32)
    gq_row = jnp.tile(jnp.concatenate([g_q_nope, g_q_rope, z32]), MLA_H) * (MLA_SCALE * LOG2E)
    gk_row = jnp.tile(jnp.concatenate([g_k_nope, jnp.zeros((LANES - MLA_NOPE,), F32)]), MLA_H)
    g_kpe = jnp.concatenate([jnp.zeros((_KPE_LANE,), F32), g_k_rope, z32])

    li = jnp.arange(2 * LANES)
    seg = jnp.where(li % LANES < MLA_NOPE, 0, jnp.where(li % LANES < MLA_NOPE + MLA_ROPE, 1, 2))
    same = (li[:, None] // LANES == li[None, :] // LANES) & (seg[:, None] == seg[None, :]) & (seg[:, None] < 2)
    bd = jnp.where(same, jnp.where(seg[:, None] == 0, 1.0 / MLA_NOPE, 1.0 / MLA_ROPE), 0.0)

    wkq = jnp.transpose(wk, (1, 2, 0)) * g_k_nope[None, :, None]
    wkq = jnp.pad(wkq, ((0, 0), (0, LANES - MLA_NOPE), (0, 0)))
    wkt = jnp.transpose(wk, (1, 2, 0)).reshape(MLA_H * MLA_NOPE, MLA_KVR)

    hk = jnp.arange(GLA_H * GLA_K) // GLA_K
    hv = jnp.arange(GLA_H * GLA_V) // GLA_V
    bexp = (hk[:, None] == hv[None, :])
    return {
        'g_mix': rowv(g_mix), 'w_in': w_in_p.astype(BF16), 'w_a2': w_a2.astype(BF16), 'b_a': rowv(b_gla_a),
        'g_qa': rowv(g_mla_qa), 'w_qb': qb.astype(BF16), 'gq_row': rowv(gq_row), 'g_kva': rowv(g_mla_kva),
        'w_kv': jnp.concatenate([wk_p, wv], axis=1).astype(BF16), 'gk_row': rowv(gk_row),
        'w_v': wv.astype(BF16), 'g_kpe': rowv(g_kpe),
        'bd': bd.astype(BF16), 'w_kq': wkq.astype(BF16), 'w_kt': wkt.astype(BF16),
        'g_gla_o': rowv(g_gla_o), 'bexp': bexp.astype(BF16),
        'w_out': w_out.astype(BF16), 'g_x': rowv(g_x), 'w_xq': w_xq.astype(BF16),
        'g_xq': rowv(g_xq) * (X_SCALE * LOG2E),
        'g_mem': rowv(g_mem), 'w_xk': w_xk.astype(BF16), 'w_xv': w_xv.astype(BF16), 'g_xk': rowv(g_xk),
        'w_xo': w_xo.astype(BF16), 'g_ff': rowv(g_ff), 'w_ff1': w_ff1.astype(BF16), 'w_ff2': w_ff2.astype(BF16),
    }


def _rope_tables(pos):
    half = MLA_ROPE // 2
    inv = ROPE_THETA ** (-jnp.arange(half, dtype=F32) / half)
    ang = pos.astype(F32)[:, None] * inv[None, :]
    cos, sin = jnp.cos(ang), jnp.sin(ang)
    n = pos.shape[0]
    one = jnp.ones((n, _KPE_LANE), F32)
    z = lambda w_: jnp.zeros((n, w_), F32)
    tail = LANES - _KPE_LANE - MLA_ROPE
    c = jnp.concatenate([one, cos, cos, jnp.ones((n, tail), F32)], axis=1)
    s_up = jnp.concatenate([z(_KPE_LANE), -sin, z(half), z(tail)], axis=1)
    s_dn = jnp.concatenate([z(_KPE_LANE), z(half), sin, z(tail)], axis=1)
    return c, s_up, s_dn


def _tile_rows(n, cap):
    t = min(n, cap)
    while n % t:
        t //= 2
    return t


def _tail(x1, xq, mk, mv, w, batch, seq, n_mem):
    tq = _tile_rows(seq, 512)
    gb = _tile_rows(batch, max(1, CROSS_ROWS // seq)) if tq == seq else 1
    o = _cross(xq, mk, mv, batch, seq, n_mem, tq, gb)
    return _ffn(x1, o, w, _tile_rows(x1.shape[0], 512), 1024)


def _prompt_layer(x, mem, w):
    batch, seq, _ = x.shape
    n = batch * seq
    xf = x.reshape(n, D_MODEL)
    tm = _tile_rows(seq, 512)
    tabs = _rope_tables(jnp.arange(seq))
    gq, gk, gv, gr, la, mq, ckv, kp, mk_, mvt = _in_proj(xf, w, tabs, tm, True)
    c_len = GLA_CHUNK
    cps = _tile_rows(seq // c_len, GLA_CHUNKS_PER_STEP)
    s0 = jnp.zeros((batch, GLA_H * GLA_K, GLA_V), F32)
    og, s_fin = _gla(gq, gk, gv, gr, la, s0, w, batch, seq, c_len, min(GLA_SUB, c_len), cps,
                     _tile_rows(batch, GLA_SEQS_PER_STEP))
    om = _mla_prompt(mq, mk_, mvt, batch, seq, ATT_TILE)
    x1, xq = _mix_out(xf, og, om, w, tm)
    n_mem = mem.shape[1]
    memf = mem.reshape(-1, D_MODEL)
    xk, xv, xkh, xvh = _mem_kv(memf, w, n_mem, max(n_mem, _tile_rows(memf.shape[0], 512)))
    y = _tail(x1, xq, xkh, xvh, w, batch, seq, n_mem)
    return y, ckv, kp, xk, xv, s_fin


def _sample_layer(x, pool_ckv, pool_kpe, page_table, mem_k, mem_v, s_prev, w):
    batch, seq, _ = x.shape
    n = batch * seq
    xf = x.reshape(n, D_MODEL)
    tm = _tile_rows(n, 512)
    past = page_table.shape[1] * PAGE
    pos = past + (jnp.arange(tm) % seq)
    gq, gk, gv, gr, la, mq, ckv, kp = _in_proj(xf, w, _rope_tables(pos), tm, False)
    s0 = s_prev.reshape(batch, GLA_H * GLA_K, GLA_V)
    og, s_fin = _gla(gq, gk, gv, gr, la, s0, w, batch, seq, seq, seq, 1, _tile_rows(batch, GLA_SHORT_SEQS_PER_STEP))
    om = _mla_sample(mq, ckv, kp, pool_ckv, jnp.swapaxes(pool_kpe, 1, 2), page_table, w, batch, seq,
                     _tile_rows(page_table.shape[1], SAMPLE_PAGES_PER_DOT))
    x1, xq = _mix_out(xf, og, om, w, tm)
    y = _tail(x1, xq, mem_k.reshape(-1, X_D), mem_v.reshape(-1, X_D), w, batch, seq, mem_k.shape[1])
    return y, ckv, kp, s_fin


def kernel(x_prompt, x_sample, mem_prompt, cache_ckv, cache_kpe, cache_mem_k, cache_mem_v, state_gla, page_table, g_mix, w_in, w_gla_a2, b_gla_a, g_gla_o, g_mla_qa, w_mla_qb, g_mla_kva, w_mla_kvb, g_q_nope, g_k_nope, g_q_rope, g_k_rope, w_out, g_x, g_mem, w_xq, w_xk, w_xv, g_xq, g_xk, w_xo, g_ff, w_ff1, w_ff2):
    depth = w_in.shape[0]
    assert depth == 1, "one layer: prompt-group caches of layer l would feed layer l+1 otherwise unchanged"
    params = (g_mix, w_in, w_gla_a2, b_gla_a, g_gla_o, g_mla_qa, w_mla_qb, g_mla_kva, w_mla_kvb,
              g_q_nope, g_k_nope, g_q_rope, g_k_rope, w_out, g_x, g_mem, w_xq, w_xk, w_xv, g_xq, g_xk,
              w_xo, g_ff, w_ff1, w_ff2)
    w = _prep_weights(*[p[0] for p in params])
    bp, tp, _ = x_prompt.shape
    bs, tsq, _ = x_sample.shape
    yp, ckv_p, kp_p, xk, xv, gla_p = _prompt_layer(x_prompt, mem_prompt, w)
    ys, ckv_s, kp_s, gla_s = _sample_layer(x_sample, cache_ckv.reshape(cache_ckv.shape[1:]), cache_kpe.reshape(cache_kpe.shape[1:]), page_table,
                                            cache_mem_k[0], cache_mem_v[0], state_gla[0], w)
    n_mem = mem_prompt.shape[1]
    return (yp.reshape(bp, tp, D_MODEL), ys.reshape(bs, tsq, D_MODEL),
            ckv_p.reshape(1, bp, tp, MLA_KVR), kp_p.reshape(1, bp, tp, MLA_ROPE),
            xk.reshape(1, bp, n_mem, X_H, X_D), xv.reshape(1, bp, n_mem, X_H, X_D),
            gla_p.reshape(1, bp, GLA_H, GLA_K, GLA_V),
            ckv_s.reshape(1, bs, tsq, MLA_KVR), kp_s.reshape(1, bs, tsq, MLA_ROPE),
            gla_s.reshape(1, bs, GLA_H, GLA_K, GLA_V))
```

```python
import functools

import jax
import jax.numpy as jnp
from jax import lax
from jax.experimental import pallas as pl
from jax.experimental.pallas import tpu as pltpu

F32 = jnp.float32
BF16 = jnp.bfloat16

EPS = 1e-6
D_MODEL = 1024
GLA_H, GLA_K, GLA_V, GLA_RANK, GLA_TAU, GLA_CHUNK = 4, 64, 128, 16, 16.0, 64
GLA_SUB = 16
GLA_CHUNKS_PER_STEP = 4
GLA_SEQS_PER_STEP = 4
GLA_SHORT_SEQS_PER_STEP = 8
GLA_FAST_MAX = 40.0
MLA_H, MLA_DV, MLA_NOPE, MLA_ROPE, MLA_QR, MLA_KVR = 8, 64, 64, 32, 384, 256
MLA_SCALE = (MLA_NOPE + MLA_ROPE) ** -0.5
ROPE_THETA = 10000.0
X_H, X_D = 4, 128
X_SCALE = X_D ** -0.5
D_FF = 4096
PAGE = 128
LANES = 128
NEG = -1e30
LOG2E = 1.4426950408889634
ATT_TILE = 256
ATT_VROWS = 80
ATT_HEADS = 8
ATT_GROUP = 2
CROSS_ROWS = 32
SAMPLE_PAGES_PER_DOT = 4
SAMPLE_GROUPS = 4

_C_Q, _C_K, _C_V, _C_R = 0, 256, 512, 1024
_C_QA, _C_AK, _C_KVA, _C_END = 1536, 1920, 2048, 2304
_KPE_LANE = 64

VMEM_LIMIT = 56 * 1024 * 1024


def _cparams(sem):
    return pltpu.CompilerParams(dimension_semantics=sem, vmem_limit_bytes=VMEM_LIMIT)


def _dot(a, b):
    return jnp.dot(a, b, preferred_element_type=F32)


def _dot_nt(a, b):
    return lax.dot_general(a, b, (((1,), (1,)), ((), ())), preferred_element_type=F32)


def _rms(x, g):
    return x * lax.rsqrt(jnp.mean(x * x, axis=-1, keepdims=True) + EPS) * g


def _full(shape):
    n = len(shape)
    return pl.BlockSpec(shape, lambda *_: (0,) * n)


def _in_proj_kernel(x_ref, gmix_ref, win_ref, wa2_ref, ba_ref, gqa_ref, wqb_ref, gqrow_ref,
                    gkva_ref, wkv_ref, gkrow_ref, gkpe_ref, bd_ref, c_ref, s1_ref, s2_ref,
                    gq_o, gk_o, gv_o, gr_o, la_o, mq_o, ckv_o, kp_o, *kv_o):
    h = _rms(x_ref[...], gmix_ref[...]).astype(BF16)

    z2 = _dot(h, win_ref[:, _C_QA:_C_END])
    qa = z2[:, 0:_C_AK - _C_QA]
    zak = z2[:, _C_AK - _C_QA:_C_KVA - _C_QA]
    kva = z2[:, _C_KVA - _C_QA:]

    gate = _dot(zak.astype(BF16), wa2_ref[...]) + ba_ref[...]
    la_o[...] = (jnp.minimum(gate, 0.0) - jnp.log1p(jnp.exp(-jnp.abs(gate)))) * (1.0 / GLA_TAU)

    cos, s_up, s_dn = c_ref[...], s1_ref[...], s2_ref[...]

    def rope(t):
        return t * cos + pltpu.roll(t, LANES - 16, 1) * s_up + pltpu.roll(t, 16, 1) * s_dn

    lane = lax.broadcasted_iota(jnp.int32, (1, LANES), 1)
    zk = jnp.where((lane >= _KPE_LANE) & (lane < _KPE_LANE + MLA_ROPE), zak, 0.0)
    kpn = zk * lax.rsqrt(jnp.sum(zk * zk, axis=-1, keepdims=True) * (1.0 / MLA_ROPE) + EPS) * gkpe_ref[...]
    kpt = rope(kpn)
    kp_o[...] = pltpu.roll(kpt, LANES - _KPE_LANE, 1)[:, 0:MLA_ROPE]

    ckv = _rms(kva, gkva_ref[...])
    ckv_o[...] = ckv
    cb = ckv.astype(BF16)
    bd = bd_ref[...]

    def seg_norm(t, grow):
        ms = _dot((t * t).astype(BF16), bd)
        return t * lax.rsqrt(ms + EPS) * grow

    qf = _dot(_rms(qa, gqa_ref[...]).astype(BF16), wqb_ref[...])
    for c in range(4):
        sl = slice(2 * LANES * c, 2 * LANES * (c + 1))
        qn = seg_norm(qf[:, sl], gqrow_ref[:, sl])
        for j in range(2):
            o = 2 * LANES * c + LANES * j
            mq_o[:, o:o + LANES] = rope(qn[:, LANES * j:LANES * (j + 1)]).astype(BF16)

    if kv_o:
        mk_o, mvt_o = kv_o
        kvf = _dot(cb, wkv_ref[...])
        kf, mv = kvf[:, 0:MLA_H * LANES], kvf[:, MLA_H * LANES:]
        ones = jnp.ones((ATT_VROWS - MLA_DV, ATT_TILE), BF16)
        for c in range(mvt_o.shape[0]):
            mvt = mv[ATT_TILE * c:ATT_TILE * (c + 1), :].T.astype(BF16)
            for hh in range(MLA_H):
                mvt_o[c, ATT_VROWS * hh:ATT_VROWS * hh + MLA_DV, :] = mvt[MLA_DV * hh:MLA_DV * (hh + 1)]
                mvt_o[c, ATT_VROWS * hh + MLA_DV:ATT_VROWS * (hh + 1), :] = ones
        for c in range(4):
            sl = slice(2 * LANES * c, 2 * LANES * (c + 1))
            kn = seg_norm(kf[:, sl], gkrow_ref[:, sl])
            for j in range(2):
                o = 2 * LANES * c + LANES * j
                mk_o[:, o:o + LANES] = (kn[:, LANES * j:LANES * (j + 1)] + kpt).astype(BF16)

    z1 = _dot(h, win_ref[:, _C_Q:_C_QA])
    gq_o[...] = z1[:, _C_Q:_C_K].astype(BF16)
    gk_o[...] = z1[:, _C_K:_C_V].astype(BF16)
    gv_o[...] = z1[:, _C_V:_C_R].astype(BF16)
    gr_o[...] = z1[:, _C_R:_C_QA].astype(BF16)


def _in_proj(x, w, tabs, tm, with_kv):
    n = x.shape[0]
    nt = tabs[0].shape[0] // tm
    row = lambda width: pl.BlockSpec((tm, width), lambda i: (i, 0))
    tab = pl.BlockSpec((tm, LANES), lambda i: (i % nt, 0))
    consts = [w['g_mix'], w['w_in'], w['w_a2'], w['b_a'], w['g_qa'], w['w_qb'], w['gq_row'],
              w['g_kva'], w['w_kv'], w['gk_row'], w['g_kpe'], w['bd']]
    out_w = [(256, BF16), (256, BF16), (512, BF16), (512, BF16), (256, F32),
             (1024, BF16), (256, F32), (MLA_ROPE, F32)] + ([(1024, BF16)] if with_kv else [])
    out_specs = [row(wd) for wd, _ in out_w]
    out_shape = [jax.ShapeDtypeStruct((n, wd), dt) for wd, dt in out_w]
    if with_kv:
        out_specs.append(pl.BlockSpec((tm // ATT_TILE, MLA_H * ATT_VROWS, ATT_TILE), lambda i: (i, 0, 0)))
        out_shape.append(jax.ShapeDtypeStruct((n // ATT_TILE, MLA_H * ATT_VROWS, ATT_TILE), BF16))
    return pl.pallas_call(
        _in_proj_kernel,
        grid=(n // tm,),
        in_specs=[row(D_MODEL)] + [_full(c.shape) for c in consts] + [tab, tab, tab],
        out_specs=out_specs,
        out_shape=out_shape,
        compiler_params=_cparams(("parallel",)),
        name="in_proj",
    )(x, *consts, *tabs)


def _head_masks():
    lane = lax.broadcasted_iota(jnp.int32, (1, GLA_H * GLA_K), 1)
    return [(lane // GLA_K == h).astype(F32) for h in range(GLA_H)]


def _gla_fast_step(qs, ks, vs, bs, sts, keep, c_len, cps):
    masks = _head_masks()
    ts = c_len * cps
    n = len(qs)
    ebs = [jnp.exp(b) for b in bs]
    kts = [k * jnp.exp(-b) for k, b in zip(ks, bs)]
    lqs = [jnp.concatenate([q * eb * m for m in masks], axis=0).astype(BF16)
           for q, eb in zip(qs, ebs)]
    atts = [_dot_nt(lq, kt.astype(BF16)) for lq, kt in zip(lqs, kts)]
    vbs = [v.astype(BF16) for v in vs]
    o_intra = [[_dot((atts[g][ts * h:ts * (h + 1)] * keep).astype(BF16), vbs[g][:, GLA_V * h:GLA_V * (h + 1)])
                for h in range(GLA_H)] for g in range(n)]
    o_inter = [[] for _ in range(n)]
    sts = list(sts)
    for c in range(cps):
        cs = slice(c_len * c, c_len * (c + 1))
        for g in range(n):
            lqc = jnp.concatenate([lqs[g][ts * h + c_len * c:ts * h + c_len * (c + 1)] for h in range(GLA_H)],
                                  axis=0)
            o_inter[g].append(_dot_nt(lqc, sts[g].astype(BF16)))
            ebl = ebs[g][c_len * (c + 1) - 1:c_len * (c + 1), :]
            kl = kts[g][cs] * ebl
            vst = jnp.concatenate([vs[g][cs, GLA_V * h:GLA_V * (h + 1)] for h in range(GLA_H)], axis=0)
            kst = jnp.concatenate([kl * m for m in masks], axis=0).astype(BF16)
            sts[g] = sts[g] * ebl + _dot(vst.T.astype(BF16), kst)
    outs = [[o_intra[g][h] + jnp.concatenate([o_inter[g][c][c_len * h:c_len * (h + 1)] for c in range(cps)],
                                             axis=0) for h in range(GLA_H)] for g in range(n)]
    return outs, sts


def _gla_chunk(q, k, v, b, st, bexp, c_len, sub):
    masks = _head_masks()
    bl = b[c_len - 1:c_len, :]

    qh = q * jnp.exp(b)
    lq = jnp.concatenate([qh * m for m in masks], axis=0).astype(BF16)
    o_inter = _dot_nt(lq, st.astype(BF16))

    nsub = c_len // sub
    a_rows = [[] for _ in range(GLA_H)]
    col = lax.broadcasted_iota(jnp.int32, (1, c_len), 1)
    for i in range(1, nsub):
        ref = b[sub * i - 1:sub * i, :]
        qi = q[sub * i:sub * (i + 1)] * jnp.exp(b[sub * i:sub * (i + 1)] - ref)
        ki = k * jnp.exp(jnp.minimum(ref - b, 0.0))
        li = jnp.concatenate([qi * m for m in masks], axis=0).astype(BF16)
        ai = jnp.where(col < sub * i, _dot_nt(li, ki.astype(BF16)), 0.0)
        for h in range(GLA_H):
            a_rows[h].append(ai[sub * h:sub * (h + 1)])

    row = lax.broadcasted_iota(jnp.int32, (sub, 1), 0)
    o_diag = []
    for i in range(nsub):
        sl = slice(sub * i, sub * (i + 1))
        bb, qb, kb, vb = b[sl], q[sl], k[sl], v[sl]
        ps = []
        for s in range(sub):
            e = jnp.exp(jnp.where(row >= s, bb - bb[s:s + 1, :], NEG))
            ps.append(qb * e * kb[s:s + 1, :])
        r = _dot(jnp.concatenate(ps, axis=0).astype(BF16), bexp)
        od = r[0:sub] * vb[0:1, :]
        for s in range(1, sub):
            od = od + r[sub * s:sub * (s + 1)] * vb[s:s + 1, :]
        o_diag.append(od)
    o_diag = jnp.concatenate(o_diag, axis=0) if nsub > 1 else o_diag[0]

    outs = []
    for h in range(GLA_H):
        oh = o_inter[c_len * h:c_len * (h + 1)] + o_diag[:, GLA_V * h:GLA_V * (h + 1)]
        if nsub > 1:
            ah = jnp.concatenate([jnp.zeros((sub, c_len), F32)] + a_rows[h], axis=0)
            oh = oh + _dot(ah.astype(BF16), v[:, GLA_V * h:GLA_V * (h + 1)].astype(BF16))
        outs.append(oh)

    kl = k * jnp.exp(bl - b)
    vs = jnp.concatenate([v[:, GLA_V * h:GLA_V * (h + 1)] for h in range(GLA_H)], axis=0)
    ks = jnp.concatenate([kl * m for m in masks], axis=0).astype(BF16)
    st_new = st * jnp.exp(bl) + _dot(vs.T.astype(BF16), ks)
    return outs, st_new


def _gla_kernel(*refs, c_len, sub, cps, fast):
    if fast:
        (q_ref, k_ref, v_ref, r_ref, la_ref, s0_ref, g_ref, bexp_ref, tri_ref, keep_ref,
         og_o, sf_o, st_ref, b_ref) = refs
    else:
        q_ref, k_ref, v_ref, r_ref, la_ref, s0_ref, g_ref, bexp_ref, og_o, sf_o, st_ref, b_ref = refs
    t = pl.program_id(1)
    gb = q_ref.shape[0]

    @pl.when(t == 0)
    def _():
        for i in range(gb):
            st_ref[i] = s0_ref[i].T

    g = g_ref[...]

    def emit(i, outs, rs):
        rr = r_ref[i, rs, :].astype(F32)
        for h in range(GLA_H):
            hs = slice(GLA_V * h, GLA_V * (h + 1))
            rh = rr[:, hs]
            og_o[i, rs, hs] = (_rms(outs[h], g) * (rh / (1.0 + jnp.exp(-rh)))).astype(BF16)

    bs = []
    for i in range(gb):
        la = la_ref[i]
        if fast:
            la_hi = la.astype(BF16)
            la_lo = (la - la_hi.astype(F32)).astype(BF16)
            b = _dot(tri_ref[...], la_hi) + _dot(tri_ref[...], la_lo)
        else:
            rows = lax.broadcasted_iota(jnp.int32, (c_len, 1), 0)
            b = jnp.zeros_like(la)
            for j in range(c_len):
                b = b + jnp.where(rows >= j, la[j:j + 1, :], 0.0)
        b_ref[i] = b
        bs.append(b)

    def robust():
        bexp = bexp_ref[...]
        for i in range(gb):
            def body(c, carry, i=i):
                rs = pl.ds(pl.multiple_of(c * c_len, c_len), c_len)
                outs, st_new = _gla_chunk(q_ref[i, rs, :].astype(F32), k_ref[i, rs, :].astype(F32),
                                          v_ref[i, rs, :].astype(F32), b_ref[i, rs, :], st_ref[i], bexp, c_len, sub)
                st_ref[i] = st_new
                emit(i, outs, rs)
                return carry

            lax.fori_loop(0, cps, body, 0)

    if fast:
        low = jnp.min(bs[0])
        for b in bs[1:]:
            low = jnp.minimum(low, jnp.min(b))
        mild = low >= -GLA_FAST_MAX

        @pl.when(mild)
        def _():
            seqs = range(gb)
            outs, sts = _gla_fast_step([q_ref[i].astype(F32) for i in seqs], [k_ref[i].astype(F32) for i in seqs],
                                       [v_ref[i].astype(F32) for i in seqs], [b_ref[i] for i in seqs],
                                       [st_ref[i] for i in seqs], keep_ref[...], c_len, cps)
            for i in seqs:
                st_ref[i] = sts[i]
                emit(i, outs[i], slice(None))

        pl.when(jnp.logical_not(mild))(robust)
    else:
        robust()

    @pl.when(t == pl.num_programs(1) - 1)
    def _():
        for i in range(gb):
            sf_o[i] = st_ref[i].T


def _gla(gq, gk, gv, gr, la, s0, w, batch, seq, c_len, sub, cps, gb):
    ts = c_len * cps
    nt = seq // ts
    fast = c_len == GLA_CHUNK
    assert fast or cps == 1
    row = lambda width: pl.BlockSpec((gb, None, ts, width), lambda b, t: (b, t, 0, 0))
    g4 = lambda a: a.reshape(batch, nt, ts, a.shape[-1])
    st_spec = pl.BlockSpec((gb, GLA_H * GLA_K, GLA_V), lambda b, t: (b, 0, 0))
    consts = [w['g_gla_o'], w['bexp']]
    if fast:
        ti = jnp.arange(ts)
        keep = (ti[:, None] // c_len == ti[None, :] // c_len) & (ti[None, :] <= ti[:, None])
        consts += [keep.astype(BF16), keep.astype(F32)]
    og, s_fin = pl.pallas_call(
        functools.partial(_gla_kernel, c_len=c_len, sub=sub, cps=cps, fast=fast),
        grid=(batch // gb, nt),
        in_specs=[row(256), row(256), row(512), row(512), row(256), st_spec] + [_full(c.shape) for c in consts],
        out_specs=[row(512), st_spec],
        out_shape=[jax.ShapeDtypeStruct((batch, nt, ts, GLA_H * GLA_V), BF16),
                   jax.ShapeDtypeStruct((batch, GLA_H * GLA_K, GLA_V), F32)],
        scratch_shapes=[pltpu.VMEM((gb, GLA_V, GLA_H * GLA_K), F32), pltpu.VMEM((gb, ts, GLA_H * GLA_K), F32)],
        compiler_params=_cparams(("parallel", "arbitrary")),
        name="gla",
    )(g4(gq), g4(gk), g4(gv), g4(gr), g4(la), s0, *consts)
    return og.reshape(batch * seq, GLA_H * GLA_V), s_fin


def _mla_prompt_kernel(q_ref, k_ref, vt_ref, o_ref, m_ref, acc_ref, *, tq):
    i = pl.program_id(1)
    m_ref[...] = jnp.full(m_ref.shape, -jnp.inf, F32)
    acc_ref[...] = jnp.zeros(acc_ref.shape, F32)
    keep = (lax.broadcasted_iota(jnp.int32, (tq, tq), 0) <= lax.broadcasted_iota(jnp.int32, (tq, tq), 1))

    def tiles(js, last_masked):
        n = len(js)
        for h0 in range(0, MLA_H, ATT_HEADS):
            hs = range(h0, h0 + ATT_HEADS)
            sts = {h: [_dot_nt(k_ref[pl.ds(pl.multiple_of(j * tq, tq), tq), LANES * h:LANES * (h + 1)],
                               q_ref[:, LANES * h:LANES * (h + 1)]) for j in js] for h in hs}
            ps, alphas = {}, {}
            for h in hs:
                if last_masked:
                    sts[h][-1] = jnp.where(keep, sts[h][-1], -jnp.inf)
                m_old = m_ref[h]
                m_new = m_old
                for st in sts[h]:
                    m_new = jnp.maximum(m_new, jnp.max(st, axis=0, keepdims=True))
                alphas[h] = jnp.exp2(m_old - m_new)
                ps[h] = [jnp.exp2(st - m_new).astype(BF16) for st in sts[h]]
                m_ref[h] = m_new
            for h in hs:
                acc = alphas[h] * acc_ref[h]
                for t in range(n):
                    acc = acc + _dot(vt_ref[js[t], ATT_VROWS * h:ATT_VROWS * (h + 1), :], ps[h][t])
                acc_ref[h] = acc

    def body(jj, carry):
        tiles([ATT_GROUP * jj + t for t in range(ATT_GROUP)], False)
        return carry

    lax.fori_loop(0, i // ATT_GROUP, body, 0)
    for rem in range(ATT_GROUP):
        pl.when(i % ATT_GROUP == rem)(functools.partial(tiles, [i - rem + t for t in range(rem + 1)], True))

    for pair in range(MLA_H // 2):
        h0, h1 = 2 * pair, 2 * pair + 1
        ab = jnp.concatenate([acc_ref[h, 0:MLA_DV, :] * (1.0 / acc_ref[h, MLA_DV:MLA_DV + 1, :]) for h in (h0, h1)],
                             axis=0)
        o_ref[:, LANES * pair:LANES * (pair + 1)] = ab.T.astype(BF16)


def _mla_prompt(mq, mk, mvt, batch, seq, tq):
    nq = seq // tq
    return pl.pallas_call(
        functools.partial(_mla_prompt_kernel, tq=tq),
        grid=(batch, nq),
        in_specs=[pl.BlockSpec((tq, MLA_H * LANES), lambda b, i: (b * nq + i, 0)),
                  pl.BlockSpec((seq, MLA_H * LANES), lambda b, i: (b, 0)),
                  pl.BlockSpec((nq, MLA_H * ATT_VROWS, tq), lambda b, i: (b, 0, 0))],
        out_specs=pl.BlockSpec((tq, MLA_H * MLA_DV), lambda b, i: (b * nq + i, 0)),
        out_shape=jax.ShapeDtypeStruct((batch * seq, MLA_H * MLA_DV), BF16),
        scratch_shapes=[pltpu.VMEM((MLA_H, 1, tq), F32), pltpu.VMEM((MLA_H, ATT_VROWS, tq), F32)],
        compiler_params=_cparams(("parallel", "arbitrary")),
        name="mla_prompt",
    )(mq, mk, mvt)


def _mla_sample_kernel(pt_ref, q_ref, wkq_ref, wkt_ref, wv_ref, cn_ref, kn_ref, ckv_hbm, kpe_hbm,
                       o_ref, ckv_buf, kpe_buf, sem, *, n_pages, t_new, kpm):
    b = pl.program_id(0)
    nb = pl.num_programs(0)
    half = b % 2
    nrow = MLA_H * t_new

    def page_copies(seq, hf, p):
        page = pt_ref[seq * n_pages + p]
        return (pltpu.make_async_copy(ckv_hbm.at[page], ckv_buf.at[hf, p], sem.at[0, hf]),
                pltpu.make_async_copy(kpe_hbm.at[page], kpe_buf.at[hf, p], sem.at[1, hf]))

    def start_pages(seq, hf, pages):
        for p in pages:
            for cp in page_copies(seq, hf, p):
                cp.start()

    def wait_pages(hf):
        pltpu.make_async_copy(ckv_hbm.at[pl.ds(0, n_pages)], ckv_buf.at[hf], sem.at[0, hf]).wait()
        pltpu.make_async_copy(kpe_hbm.at[pl.ds(0, n_pages)], kpe_buf.at[hf], sem.at[1, hf]).wait()

    @pl.when(b == 0)
    def _():
        start_pages(0, 0, range(n_pages))

    nxt = (b + 1) % nb

    wait_pages(half)

    qa, qr = [], []
    for h in range(MLA_H):
        qh = q_ref[:, LANES * h:LANES * (h + 1)]
        qa.append(_dot(qh, wkq_ref[h]))
        qr.append(pltpu.roll(qh.astype(F32), LANES - _KPE_LANE, 1)[:, 0:MLA_ROPE])
    lq = jnp.concatenate([jnp.concatenate(qa, axis=0).astype(BF16), wkt_ref[...]], axis=0)
    qp = jnp.concatenate(qr, axis=0).astype(BF16)

    def scores(r, s_rope):
        rinv = []
        for h in range(MLA_H):
            kr = r[nrow + MLA_NOPE * h:nrow + MLA_NOPE * (h + 1)]
            ms = jnp.sum(kr * kr, axis=0, keepdims=True) * (1.0 / MLA_NOPE)
            rinv.append(jnp.broadcast_to(lax.rsqrt(ms + EPS), (t_new, r.shape[1])))
        return r[0:nrow] * jnp.concatenate(rinv, axis=0) + s_rope

    ngrp = n_pages // kpm
    ppb = kpm * SAMPLE_GROUPS
    xs, ss = [], []
    for g0 in range(0, ngrp, SAMPLE_GROUPS):
        gs = range(g0, min(g0 + SAMPLE_GROUPS, ngrp))
        xg = [jnp.concatenate([ckv_buf[half, g * kpm + j] for j in range(kpm)], axis=0).astype(BF16) for g in gs]
        rg = [_dot_nt(lq, x) for x in xg]
        pg = [_dot(qp, jnp.concatenate([kpe_buf[half, g * kpm + j] for j in range(kpm)], axis=1).astype(BF16))
              for g in gs]
        start_pages(nxt, 1 - half, range(g0 * kpm, min(g0 * kpm + ppb, n_pages)))
        xs.append(jnp.concatenate(xg, axis=0))
        ss.append(jnp.concatenate([scores(r, s_rope) for r, s_rope in zip(rg, pg)], axis=1))

    xn = cn_ref[...].astype(BF16)
    sn = scores(_dot_nt(lq, xn), _dot_nt(qp, kn_ref[...].astype(BF16)))
    tq = lax.broadcasted_iota(jnp.int32, (nrow, t_new), 0) % t_new
    ts = lax.broadcasted_iota(jnp.int32, (nrow, t_new), 1)
    xs.append(xn)
    ss.append(jnp.where(ts <= tq, sn, -jnp.inf))

    ms = [jnp.max(s, axis=-1, keepdims=True) for s in ss]
    ps = [jnp.exp2(s - mi) for s, mi in zip(ss, ms)]
    ls = [jnp.sum(p, axis=-1, keepdims=True) for p in ps]
    pvs = [_dot(p.astype(BF16), x) for p, x in zip(ps, xs)]
    m = ms[0]
    for mi in ms[1:]:
        m = jnp.maximum(m, mi)
    ws = [jnp.exp2(mi - m) for mi in ms]
    l = ws[0] * ls[0]
    oa = ws[0] * pvs[0]
    for wi, li, pv in zip(ws[1:], ls[1:], pvs[1:]):
        l = l + wi * li
        oa = oa + wi * pv
    oa = oa / l

    r = _dot(oa.astype(BF16), wv_ref[...])
    lane = lax.broadcasted_iota(jnp.int32, (1, MLA_H * MLA_DV), 1)
    out = jnp.zeros((t_new, MLA_H * MLA_DV), F32)
    for h in range(MLA_H):
        out = out + jnp.where(lane // MLA_DV == h, r[t_new * h:t_new * (h + 1)], 0.0)
    o_ref[...] = out.astype(BF16)

    @pl.when(b == nb - 1)
    def _():
        wait_pages(1 - half)


def _mla_sample(mq, ckv_new, kp_new, pool_ckv, pool_kpe_t, page_table, w, batch, t_new, kpm):
    n_pages = page_table.shape[1]
    pt = page_table.reshape(-1)
    tok = lambda width: pl.BlockSpec((None, t_new, width), lambda b, pt_ref: (b, 0, 0))
    g3 = lambda a: a.reshape(batch, t_new, a.shape[-1])
    const = lambda a: pl.BlockSpec(a.shape, lambda b, pt_ref: (0,) * a.ndim)
    hbm = pl.BlockSpec(memory_space=pl.ANY)
    grid_spec = pltpu.PrefetchScalarGridSpec(
        num_scalar_prefetch=1,
        grid=(batch,),
        in_specs=[tok(MLA_H * LANES), const(w['w_kq']), const(w['w_kt']), const(w['w_v']),
                  tok(MLA_KVR), tok(MLA_ROPE), hbm, hbm],
        out_specs=tok(MLA_H * MLA_DV),
        scratch_shapes=[pltpu.VMEM((2, n_pages, PAGE, MLA_KVR), F32),
                        pltpu.VMEM((2, n_pages, MLA_ROPE, PAGE), F32),
                        pltpu.SemaphoreType.DMA((2, 2))],
    )
    om = pl.pallas_call(
        functools.partial(_mla_sample_kernel, n_pages=n_pages, t_new=t_new, kpm=kpm),
        grid_spec=grid_spec,
        out_shape=jax.ShapeDtypeStruct((batch, t_new, MLA_H * MLA_DV), BF16),
        compiler_params=_cparams(("arbitrary",)),
        name="mla_sample",
    )(pt, g3(mq), w['w_kq'], w['w_kt'], w['w_v'], g3(ckv_new), g3(kp_new), pool_ckv, pool_kpe_t)
    return om.reshape(batch * t_new, MLA_H * MLA_DV)


def _mix_out_kernel(x_ref, og_ref, om_ref, wo_ref, gx_ref, wxq_ref, gxq_ref, x1_o, xq_o):
    x1 = x_ref[...] + _dot(og_ref[...], wo_ref[0:512, :]) + _dot(om_ref[...], wo_ref[512:1024, :])
    x1_o[...] = x1
    qf = _dot(_rms(x1, gx_ref[...]).astype(BF16), wxq_ref[...])
    g = gxq_ref[...]
    for h in range(X_H):
        hs = slice(X_D * h, X_D * (h + 1))
        xq_o[:, hs] = _rms(qf[:, hs], g).astype(BF16)


def _mix_out(x, og, om, w, tm):
    n = x.shape[0]
    row = lambda width: pl.BlockSpec((tm, width), lambda i: (i, 0))
    consts = [w['w_out'], w['g_x'], w['w_xq'], w['g_xq']]
    return pl.pallas_call(
        _mix_out_kernel,
        grid=(n // tm,),
        in_specs=[row(D_MODEL), row(512), row(512)] + [_full(c.shape) for c in consts],
        out_specs=[row(D_MODEL), row(X_H * X_D)],
        out_shape=[jax.ShapeDtypeStruct((n, D_MODEL), F32), jax.ShapeDtypeStruct((n, X_H * X_D), BF16)],
        compiler_params=_cparams(("parallel",)),
        name="mix_out",
    )(x, og, om, *consts)


def _cross_kernel(xq_ref, mk_ref, mv_ref, o_ref, *, n_mem):
    gb = xq_ref.shape[0]

    def head(ref, g, h):
        if len(ref.shape) == 4:
            return ref[g, h]
        return ref[pl.ds(g * n_mem * X_H + h, n_mem, stride=X_H), :].astype(BF16)

    units = [(g, h) for g in range(gb) for h in range(X_H)]
    ss = [_dot_nt(xq_ref[g, :, X_D * h:X_D * (h + 1)], head(mk_ref, g, h)) for g, h in units]
    ps = [jnp.exp2(s - jnp.max(s, axis=-1, keepdims=True)) for s in ss]
    for (g, h), p in zip(units, ps):
        o = _dot(p.astype(BF16), head(mv_ref, g, h))
        o_ref[g, :, X_D * h:X_D * (h + 1)] = (o * (1.0 / jnp.sum(p, axis=-1, keepdims=True))).astype(BF16)


def _cross(xq, mk, mv, batch, seq, n_mem, tq, gb):
    nq = seq // tq
    assert gb == 1 or nq == 1
    row = pl.BlockSpec((gb, tq, X_H * X_D), lambda b, i: (b * nq + i, 0, 0))
    if mk.ndim == 4:
        mem = pl.BlockSpec((gb,) + mk.shape[1:], lambda b, i: (b, 0, 0, 0))
    else:
        mem = pl.BlockSpec((gb * n_mem * X_H, X_D), lambda b, i: (b, 0))
    o = pl.pallas_call(
        functools.partial(_cross_kernel, n_mem=n_mem),
        grid=(batch // gb, nq),
        in_specs=[row, mem, mem],
        out_specs=row,
        out_shape=jax.ShapeDtypeStruct((batch * nq, tq, X_H * X_D), BF16),
        compiler_params=_cparams(("parallel", "arbitrary")),
        name="cross",
    )(xq.reshape(batch * nq, tq, X_H * X_D), mk, mv)
    return o.reshape(batch * seq, X_H * X_D)


def _ffn_kernel(x_ref, o_ref, wxo_ref, g_ref, w1_ref, w2_ref, y_o, *, fc):
    x = x_ref[...] + _dot(o_ref[...], wxo_ref[...])
    h = _rms(x, g_ref[...]).astype(BF16)
    acc = x
    for c in range(D_FF // fc):
        u = jnp.maximum(_dot(h, w1_ref[:, fc * c:fc * (c + 1)]), 0.0)
        acc = acc + _dot((u * u).astype(BF16), w2_ref[fc * c:fc * (c + 1), :])
    y_o[...] = acc


def _ffn(x, o, w, tm, fc):
    n = x.shape[0]
    row = lambda width: pl.BlockSpec((tm, width), lambda i: (i, 0))
    consts = [w['w_xo'], w['g_ff'], w['w_ff1'], w['w_ff2']]
    return pl.pallas_call(
        functools.partial(_ffn_kernel, fc=fc),
        grid=(n // tm,),
        in_specs=[row(D_MODEL), row(X_H * X_D)] + [_full(c.shape) for c in consts],
        out_specs=row(D_MODEL),
        out_shape=jax.ShapeDtypeStruct((n, D_MODEL), F32),
        compiler_params=_cparams(("parallel",)),
        name="ffn",
    )(x, o, *consts)


def _mem_kv_kernel(m_ref, gm_ref, wk_ref, wv_ref, gk_ref, k_o, v_o, kh_o, vh_o, *, n_mem):
    tm = m_ref.shape[0]
    m = _rms(m_ref[...], gm_ref[...]).astype(BF16)
    kf = _dot(m, wk_ref[...])
    vf = _dot(m, wv_ref[...])
    g = gk_ref[...]
    for h in range(X_H):
        hs = slice(X_D * h, X_D * (h + 1))
        kn = _rms(kf[:, hs], g)
        k_o[pl.ds(h, tm, stride=X_H), :] = kn
        v_o[pl.ds(h, tm, stride=X_H), :] = vf[:, hs]
        for b in range(kh_o.shape[0]):
            rs = slice(n_mem * b, n_mem * (b + 1))
            kh_o[b, h] = kn[rs].astype(BF16)
            vh_o[b, h] = vf[rs, hs].astype(BF16)


def _mem_kv(mem, w, n_mem, tm):
    n = mem.shape[0]
    bt = tm // n_mem
    row = lambda width: pl.BlockSpec((tm, width), lambda i: (i, 0))
    head = pl.BlockSpec((bt, X_H, n_mem, X_D), lambda i: (i, 0, 0, 0))
    consts = [w['g_mem'], w['w_xk'], w['w_xv'], w['g_xk']]
    return pl.pallas_call(
        functools.partial(_mem_kv_kernel, n_mem=n_mem),
        grid=(n // tm,),
        in_specs=[row(D_MODEL)] + [_full(c.shape) for c in consts],
        out_specs=[pl.BlockSpec((tm * X_H, X_D), lambda i: (i, 0))] * 2 + [head, head],
        out_shape=[jax.ShapeDtypeStruct((n * X_H, X_D), F32)] * 2
                  + [jax.ShapeDtypeStruct((n // n_mem, X_H, n_mem, X_D), BF16)] * 2,
        compiler_params=_cparams(("parallel",)),
        name="mem_kv",
    )(mem, *consts)


def _prep_weights(g_mix, w_in, w_gla_a2, b_gla_a, g_gla_o, g_mla_qa, w_mla_qb, g_mla_kva, w_mla_kvb,
                  g_q_nope, g_k_nope, g_q_rope, g_k_rope, w_out, g_x, g_mem, w_xq, w_xk, w_xv, g_xq, g_xk,
                  w_xo, g_ff, w_ff1, w_ff2):
    rowv = lambda g: g.reshape(1, -1).astype(F32)
    zc = lambda n: jnp.zeros((D_MODEL, n), F32)
    sizes = (256, 256, 512, 512, GLA_RANK, MLA_QR, MLA_KVR, MLA_ROPE)
    offs = [0]
    for s in sizes:
        offs.append(offs[-1] + s)
    q, k, v, r, a, qa, kva, kpe = [w_in[:, offs[i]:offs[i + 1]] for i in range(8)]
    w_in_p = jnp.concatenate(
        [q * (GLA_K ** -0.5), k, v, r, qa,
         a, zc(_KPE_LANE - GLA_RANK), kpe, zc(LANES - _KPE_LANE - MLA_ROPE), kva], axis=1)
    w_a2 = jnp.concatenate([w_gla_a2, jnp.zeros((LANES - GLA_RANK, GLA_H * GLA_K), F32)], axis=0)

    qb = w_mla_qb.reshape(MLA_QR, MLA_H, MLA_NOPE + MLA_ROPE)
    qb = jnp.pad(qb, ((0, 0), (0, 0), (0, LANES - MLA_NOPE - MLA_ROPE))).reshape(MLA_QR, MLA_H * LANES)
    kvb = w_mla_kvb.reshape(MLA_KVR, MLA_H, MLA_NOPE + MLA_DV)
    wk = kvb[:, :, :MLA_NOPE]
    wk_p = jnp.pad(wk, ((0, 0), (0, 0), (0, LANES - MLA_NOPE))).reshape(MLA_KVR, MLA_H * LANES)
    wv = kvb[:, :, MLA_NOPE:].reshape(MLA_KVR, MLA_H * MLA_DV)
    z32 = jnp.zeros((LANES - MLA_NOPE - MLA_ROPE,), F32)
    gq_row = jnp.tile(jnp.concatenate([g_q_nope, g_q_rope, z32]), MLA_H) * (MLA_SCALE * LOG2E)
    gk_row = jnp.tile(jnp.concatenate([g_k_nope, jnp.zeros((LANES - MLA_NOPE,), F32)]), MLA_H)
    g_kpe = jnp.concatenate([jnp.zeros((_KPE_LANE,), F32), g_k_rope, z32])

    li = jnp.arange(2 * LANES)
    seg = jnp.where(li % LANES < MLA_NOPE, 0, jnp.where(li % LANES < MLA_NOPE + MLA_ROPE, 1, 2))
    same = (li[:, None] // LANES == li[None, :] // LANES) & (seg[:, None] == seg[None, :]) & (seg[:, None] < 2)
    bd = jnp.where(same, jnp.where(seg[:, None] == 0, 1.0 / MLA_NOPE, 1.0 / MLA_ROPE), 0.0)

    wkq = jnp.transpose(wk, (1, 2, 0)) * g_k_nope[None, :, None]
    wkq = jnp.pad(wkq, ((0, 0), (0, LANES - MLA_NOPE), (0, 0)))
    wkt = jnp.transpose(wk, (1, 2, 0)).reshape(MLA_H * MLA_NOPE, MLA_KVR)

    hk = jnp.arange(GLA_H * GLA_K) // GLA_K
    hv = jnp.arange(GLA_H * GLA_V) // GLA_V
    bexp = (hk[:, None] == hv[None, :])
    return {
        'g_mix': rowv(g_mix), 'w_in': w_in_p.astype(BF16), 'w_a2': w_a2.astype(BF16), 'b_a': rowv(b_gla_a),
        'g_qa': rowv(g_mla_qa), 'w_qb': qb.astype(BF16), 'gq_row': rowv(gq_row), 'g_kva': rowv(g_mla_kva),
        'w_kv': jnp.concatenate([wk_p, wv], axis=1).astype(BF16), 'gk_row': rowv(gk_row),
        'w_v': wv.astype(BF16), 'g_kpe': rowv(g_kpe),
        'bd': bd.astype(BF16), 'w_kq': wkq.astype(BF16), 'w_kt': wkt.astype(BF16),
        'g_gla_o': rowv(g_gla_o), 'bexp': bexp.astype(BF16),
        'w_out': w_out.astype(BF16), 'g_x': rowv(g_x), 'w_xq': w_xq.astype(BF16),
        'g_xq': rowv(g_xq) * (X_SCALE * LOG2E),
        'g_mem': rowv(g_mem), 'w_xk': w_xk.astype(BF16), 'w_xv': w_xv.astype(BF16), 'g_xk': rowv(g_xk),
        'w_xo': w_xo.astype(BF16), 'g_ff': rowv(g_ff), 'w_ff1': w_ff1.astype(BF16), 'w_ff2': w_ff2.astype(BF16),
    }


def _rope_tables(pos):
    half = MLA_ROPE // 2
    inv = ROPE_THETA ** (-jnp.arange(half, dtype=F32) / half)
    ang = pos.astype(F32)[:, None] * inv[None, :]
    cos, sin = jnp.cos(ang), jnp.sin(ang)
    n = pos.shape[0]
    one = jnp.ones((n, _KPE_LANE), F32)
    z = lambda w_: jnp.zeros((n, w_), F32)
    tail = LANES - _KPE_LANE - MLA_ROPE
    c = jnp.concatenate([one, cos, cos, jnp.ones((n, tail), F32)], axis=1)
    s_up = jnp.concatenate([z(_KPE_LANE), -sin, z(half), z(tail)], axis=1)
    s_dn = jnp.concatenate([z(_KPE_LANE), z(half), sin, z(tail)], axis=1)
    return c, s_up, s_dn


def _tile_rows(n, cap):
    t = min(n, cap)
    while n % t:
        t //= 2
    return t


def _tail(x1, xq, mk, mv, w, batch, seq, n_mem):
    tq = _tile_rows(seq, 512)
    gb = _tile_rows(batch, max(1, CROSS_ROWS // seq)) if tq == seq else 1
    o = _cross(xq, mk, mv, batch, seq, n_mem, tq, gb)
    return _ffn(x1, o, w, _tile_rows(x1.shape[0], 512), 1024)


def _prompt_layer(x, mem, w):
    batch, seq, _ = x.shape
    n = batch * seq
    xf = x.reshape(n, D_MODEL)
    tm = _tile_rows(seq, 512)
    tabs = _rope_tables(jnp.arange(seq))
    gq, gk, gv, gr, la, mq, ckv, kp, mk_, mvt = _in_proj(xf, w, tabs, tm, True)
    c_len = GLA_CHUNK
    cps = _tile_rows(seq // c_len, GLA_CHUNKS_PER_STEP)
    s0 = jnp.zeros((batch, GLA_H * GLA_K, GLA_V), F32)
    og, s_fin = _gla(gq, gk, gv, gr, la, s0, w, batch, seq, c_len, min(GLA_SUB, c_len), cps,
                     _tile_rows(batch, GLA_SEQS_PER_STEP))
    om = _mla_prompt(mq, mk_, mvt, batch, seq, ATT_TILE)
    x1, xq = _mix_out(xf, og, om, w, tm)
    n_mem = mem.shape[1]
    memf = mem.reshape(-1, D_MODEL)
    xk, xv, xkh, xvh = _mem_kv(memf, w, n_mem, max(n_mem, _tile_rows(memf.shape[0], 512)))
    y = _tail(x1, xq, xkh, xvh, w, batch, seq, n_mem)
    return y, ckv, kp, xk, xv, s_fin


def _sample_layer(x, pool_ckv, pool_kpe, page_table, mem_k, mem_v, s_prev, w):
    batch, seq, _ = x.shape
    n = batch * seq
    xf = x.reshape(n, D_MODEL)
    tm = _tile_rows(n, 512)
    past = page_table.shape[1] * PAGE
    pos = past + (jnp.arange(tm) % seq)
    gq, gk, gv, gr, la, mq, ckv, kp = _in_proj(xf, w, _rope_tables(pos), tm, False)
    s0 = s_prev.reshape(batch, GLA_H * GLA_K, GLA_V)
    og, s_fin = _gla(gq, gk, gv, gr, la, s0, w, batch, seq, seq, seq, 1, _tile_rows(batch, GLA_SHORT_SEQS_PER_STEP))
    om = _mla_sample(mq, ckv, kp, pool_ckv, jnp.swapaxes(pool_kpe, 1, 2), page_table, w, batch, seq,
                     _tile_rows(page_table.shape[1], SAMPLE_PAGES_PER_DOT))
    x1, xq = _mix_out(xf, og, om, w, tm)
    y = _tail(x1, xq, mem_k.reshape(-1, X_D), mem_v.reshape(-1, X_D), w, batch, seq, mem_k.shape[1])
    return y, ckv, kp, s_fin


def kernel(x_prompt, x_sample, mem_prompt, cache_ckv, cache_kpe, cache_mem_k, cache_mem_v, state_gla, page_table, g_mix, w_in, w_gla_a2, b_gla_a, g_gla_o, g_mla_qa, w_mla_qb, g_mla_kva, w_mla_kvb, g_q_nope, g_k_nope, g_q_rope, g_k_rope, w_out, g_x, g_mem, w_xq, w_xk, w_xv, g_xq, g_xk, w_xo, g_ff, w_ff1, w_ff2):
    depth = w_in.shape[0]
    assert depth == 1, "one layer: prompt-group caches of layer l would feed layer l+1 otherwise unchanged"
    params = (g_mix, w_in, w_gla_a2, b_gla_a, g_gla_o, g_mla_qa, w_mla_qb, g_mla_kva, w_mla_kvb,
              g_q_nope, g_k_nope, g_q_rope, g_k_rope, w_out, g_x, g_mem, w_xq, w_xk, w_xv, g_xq, g_xk,
              w_xo, g_ff, w_ff1, w_ff2)
    w = _prep_weights(*[p[0] for p in params])
    bp, tp, _ = x_prompt.shape
    bs, tsq, _ = x_sample.shape
    yp, ckv_p, kp_p, xk, xv, gla_p = _prompt_layer(x_prompt, mem_prompt, w)
    ys, ckv_s, kp_s, gla_s = _sample_layer(x_sample, cache_ckv.reshape(cache_ckv.shape[1:]), cache_kpe.reshape(cache_kpe.shape[1:]), page_table,
                                            cache_mem_k[0], cache_mem_v[0], state_gla[0], w)
    n_mem = mem_prompt.shape[1]
    return (yp.reshape(bp, tp, D_MODEL), ys.reshape(bs, tsq, D_MODEL),
            ckv_p.reshape(1, bp, tp, MLA_KVR), kp_p.reshape(1, bp, tp, MLA_ROPE),
            xk.reshape(1, bp, n_mem, X_H, X_D), xv.reshape(1, bp, n_mem, X_H, X_D),
            gla_p.reshape(1, bp, GLA_H, GLA_K, GLA_V),
            ckv_s.reshape(1, bs, tsq, MLA_KVR), kp_s.reshape(1, bs, tsq, MLA_ROPE),
            gla_s.reshape(1, bs, GLA_H, GLA_K, GLA_V))
```

```python
import functools

import jax
import jax.numpy as jnp
from jax import lax
from jax.experimental import pallas as pl
from jax.experimental.pallas import tpu as pltpu

F32 = jnp.float32
BF16 = jnp.bfloat16

EPS = 1e-6
D_MODEL = 1024
GLA_H, GLA_K, GLA_V, GLA_RANK, GLA_TAU, GLA_CHUNK = 4, 64, 128, 16, 16.0, 64
GLA_SUB = 16
GLA_CHUNKS_PER_STEP = 4
GLA_SEQS_PER_STEP = 4
GLA_SHORT_SEQS_PER_STEP = 8
GLA_FAST_MAX = 40.0
MLA_H, MLA_DV, MLA_NOPE, MLA_ROPE, MLA_QR, MLA_KVR = 8, 64, 64, 32, 384, 256
MLA_SCALE = (MLA_NOPE + MLA_ROPE) ** -0.5
ROPE_THETA = 10000.0
X_H, X_D = 4, 128
X_SCALE = X_D ** -0.5
D_FF = 4096
PAGE = 128
LANES = 128
NEG = -1e30
LOG2E = 1.4426950408889634
ATT_TILE = 256
ATT_VROWS = 80
ATT_HEADS = 8
ATT_GROUP = 2
CROSS_ROWS = 32
SAMPLE_PAGES_PER_DOT = 4
SAMPLE_GROUPS = 4

_C_Q, _C_K, _C_V, _C_R = 0, 256, 512, 1024
_C_QA, _C_AK, _C_KVA, _C_END = 1536, 1920, 2048, 2304
_KPE_LANE = 64

VMEM_LIMIT = 56 * 1024 * 1024


def _cparams(sem):
    return pltpu.CompilerParams(dimension_semantics=sem, vmem_limit_bytes=VMEM_LIMIT)


def _dot(a, b):
    return jnp.dot(a, b, preferred_element_type=F32)


def _dot_nt(a, b):
    return lax.dot_general(a, b, (((1,), (1,)), ((), ())), preferred_element_type=F32)


def _rms(x, g):
    return x * lax.rsqrt(jnp.mean(x * x, axis=-1, keepdims=True) + EPS) * g


def _full(shape):
    n = len(shape)
    return pl.BlockSpec(shape, lambda *_: (0,) * n)


def _in_proj_kernel(x_ref, gmix_ref, win_ref, wa2_ref, ba_ref, gqa_ref, wqb_ref, gqrow_ref,
                    gkva_ref, wkv_ref, gkrow_ref, gkpe_ref, bd_ref, c_ref, s1_ref, s2_ref,
                    gq_o, gk_o, gv_o, gr_o, la_o, mq_o, ckv_o, kp_o, *kv_o):
    h = _rms(x_ref[...], gmix_ref[...]).astype(BF16)

    z2 = _dot(h, win_ref[:, _C_QA:_C_END])
    qa = z2[:, 0:_C_AK - _C_QA]
    zak = z2[:, _C_AK - _C_QA:_C_KVA - _C_QA]
    kva = z2[:, _C_KVA - _C_QA:]

    gate = _dot(zak.astype(BF16), wa2_ref[...]) + ba_ref[...]
    la_o[...] = (jnp.minimum(gate, 0.0) - jnp.log1p(jnp.exp(-jnp.abs(gate)))) * (1.0 / GLA_TAU)

    cos, s_up, s_dn = c_ref[...], s1_ref[...], s2_ref[...]

    def rope(t):
        return t * cos + pltpu.roll(t, LANES - 16, 1) * s_up + pltpu.roll(t, 16, 1) * s_dn

    lane = lax.broadcasted_iota(jnp.int32, (1, LANES), 1)
    zk = jnp.where((lane >= _KPE_LANE) & (lane < _KPE_LANE + MLA_ROPE), zak, 0.0)
    kpn = zk * lax.rsqrt(jnp.sum(zk * zk, axis=-1, keepdims=True) * (1.0 / MLA_ROPE) + EPS) * gkpe_ref[...]
    kpt = rope(kpn)
    kp_o[...] = pltpu.roll(kpt, LANES - _KPE_LANE, 1)[:, 0:MLA_ROPE]

    ckv = _rms(kva, gkva_ref[...])
    ckv_o[...] = ckv
    cb = ckv.astype(BF16)
    bd = bd_ref[...]

    def seg_norm(t, grow):
        ms = _dot((t * t).astype(BF16), bd)
        return t * lax.rsqrt(ms + EPS) * grow

    qf = _dot(_rms(qa, gqa_ref[...]).astype(BF16), wqb_ref[...])
    for c in range(4):
        sl = slice(2 * LANES * c, 2 * LANES * (c + 1))
        qn = seg_norm(qf[:, sl], gqrow_ref[:, sl])
        for j in range(2):
            o = 2 * LANES * c + LANES * j
            mq_o[:, o:o + LANES] = rope(qn[:, LANES * j:LANES * (j + 1)]).astype(BF16)

    if kv_o:
        mk_o, mvt_o = kv_o
        kvf = _dot(cb, wkv_ref[...])
        kf, mv = kvf[:, 0:MLA_H * LANES], kvf[:, MLA_H * LANES:]
        ones = jnp.ones((ATT_VROWS - MLA_DV, ATT_TILE), BF16)
        for c in range(mvt_o.shape[0]):
            mvt = mv[ATT_TILE * c:ATT_TILE * (c + 1), :].T.astype(BF16)
            for hh in range(MLA_H):
                mvt_o[c, ATT_VROWS * hh:ATT_VROWS * hh + MLA_DV, :] = mvt[MLA_DV * hh:MLA_DV * (hh + 1)]
                mvt_o[c, ATT_VROWS * hh + MLA_DV:ATT_VROWS * (hh + 1), :] = ones
        for c in range(4):
            sl = slice(2 * LANES * c, 2 * LANES * (c + 1))
            kn = seg_norm(kf[:, sl], gkrow_ref[:, sl])
            for j in range(2):
                o = 2 * LANES * c + LANES * j
                mk_o[:, o:o + LANES] = (kn[:, LANES * j:LANES * (j + 1)] + kpt).astype(BF16)

    z1 = _dot(h, win_ref[:, _C_Q:_C_QA])
    gq_o[...] = z1[:, _C_Q:_C_K].astype(BF16)
    gk_o[...] = z1[:, _C_K:_C_V].astype(BF16)
    gv_o[...] = z1[:, _C_V:_C_R].astype(BF16)
    gr_o[...] = z1[:, _C_R:_C_QA].astype(BF16)


def _in_proj(x, w, tabs, tm, with_kv):
    n = x.shape[0]
    nt = tabs[0].shape[0] // tm
    row = lambda width: pl.BlockSpec((tm, width), lambda i: (i, 0))
    tab = pl.BlockSpec((tm, LANES), lambda i: (i % nt, 0))
    consts = [w['g_mix'], w['w_in'], w['w_a2'], w['b_a'], w['g_qa'], w['w_qb'], w['gq_row'],
              w['g_kva'], w['w_kv'], w['gk_row'], w['g_kpe'], w['bd']]
    out_w = [(256, BF16), (256, BF16), (512, BF16), (512, BF16), (256, F32),
             (1024, BF16), (256, F32), (MLA_ROPE, F32)] + ([(1024, BF16)] if with_kv else [])
    out_specs = [row(wd) for wd, _ in out_w]
    out_shape = [jax.ShapeDtypeStruct((n, wd), dt) for wd, dt in out_w]
    if with_kv:
        out_specs.append(pl.BlockSpec((tm // ATT_TILE, MLA_H * ATT_VROWS, ATT_TILE), lambda i: (i, 0, 0)))
        out_shape.append(jax.ShapeDtypeStruct((n // ATT_TILE, MLA_H * ATT_VROWS, ATT_TILE), BF16))
    return pl.pallas_call(
        _in_proj_kernel,
        grid=(n // tm,),
        in_specs=[row(D_MODEL)] + [_full(c.shape) for c in consts] + [tab, tab, tab],
        out_specs=out_specs,
        out_shape=out_shape,
        compiler_params=_cparams(("parallel",)),
        name="in_proj",
    )(x, *consts, *tabs)


def _head_masks():
    lane = lax.broadcasted_iota(jnp.int32, (1, GLA_H * GLA_K), 1)
    return [(lane // GLA_K == h).astype(F32) for h in range(GLA_H)]


def _gla_fast_step(qs, ks, vs, bs, sts, keep, c_len, cps):
    masks = _head_masks()
    ts = c_len * cps
    n = len(qs)
    ebs = [jnp.exp(b) for b in bs]
    kts = [k * jnp.exp(-b) for k, b in zip(ks, bs)]
    lqs = [jnp.concatenate([q * eb * m for m in masks], axis=0).astype(BF16)
           for q, eb in zip(qs, ebs)]
    atts = [_dot_nt(lq, kt.astype(BF16)) for lq, kt in zip(lqs, kts)]
    vbs = [v.astype(BF16) for v in vs]
    o_intra = [[_dot((atts[g][ts * h:ts * (h + 1)] * keep).astype(BF16), vbs[g][:, GLA_V * h:GLA_V * (h + 1)])
                for h in range(GLA_H)] for g in range(n)]
    o_inter = [[] for _ in range(n)]
    sts = list(sts)
    for c in range(cps):
        cs = slice(c_len * c, c_len * (c + 1))
        for g in range(n):
            lqc = jnp.concatenate([lqs[g][ts * h + c_len * c:ts * h + c_len * (c + 1)] for h in range(GLA_H)],
                                  axis=0)
            o_inter[g].append(_dot_nt(lqc, sts[g].astype(BF16)))
            ebl = ebs[g][c_len * (c + 1) - 1:c_len * (c + 1), :]
            kl = kts[g][cs] * ebl
            vst = jnp.concatenate([vs[g][cs, GLA_V * h:GLA_V * (h + 1)] for h in range(GLA_H)], axis=0)
            kst = jnp.concatenate([kl * m for m in masks], axis=0).astype(BF16)
            sts[g] = sts[g] * ebl + _dot(vst.T.astype(BF16), kst)
    outs = [[o_intra[g][h] + jnp.concatenate([o_inter[g][c][c_len * h:c_len * (h + 1)] for c in range(cps)],
                                             axis=0) for h in range(GLA_H)] for g in range(n)]
    return outs, sts


def _gla_chunk(q, k, v, b, st, bexp, c_len, sub):
    masks = _head_masks()
    bl = b[c_len - 1:c_len, :]

    qh = q * jnp.exp(b)
    lq = jnp.concatenate([qh * m for m in masks], axis=0).astype(BF16)
    o_inter = _dot_nt(lq, st.astype(BF16))

    nsub = c_len // sub
    a_rows = [[] for _ in range(GLA_H)]
    col = lax.broadcasted_iota(jnp.int32, (1, c_len), 1)
    for i in range(1, nsub):
        ref = b[sub * i - 1:sub * i, :]
        qi = q[sub * i:sub * (i + 1)] * jnp.exp(b[sub * i:sub * (i + 1)] - ref)
        ki = k * jnp.exp(jnp.minimum(ref - b, 0.0))
        li = jnp.concatenate([qi * m for m in masks], axis=0).astype(BF16)
        ai = jnp.where(col < sub * i, _dot_nt(li, ki.astype(BF16)), 0.0)
        for h in range(GLA_H):
            a_rows[h].append(ai[sub * h:sub * (h + 1)])

    row = lax.broadcasted_iota(jnp.int32, (sub, 1), 0)
    o_diag = []
    for i in range(nsub):
        sl = slice(sub * i, sub * (i + 1))
        bb, qb, kb, vb = b[sl], q[sl], k[sl], v[sl]
        ps = []
        for s in range(sub):
            e = jnp.exp(jnp.where(row >= s, bb - bb[s:s + 1, :], NEG))
            ps.append(qb * e * kb[s:s + 1, :])
        r = _dot(jnp.concatenate(ps, axis=0).astype(BF16), bexp)
        od = r[0:sub] * vb[0:1, :]
        for s in range(1, sub):
            od = od + r[sub * s:sub * (s + 1)] * vb[s:s + 1, :]
        o_diag.append(od)
    o_diag = jnp.concatenate(o_diag, axis=0) if nsub > 1 else o_diag[0]

    outs = []
    for h in range(GLA_H):
        oh = o_inter[c_len * h:c_len * (h + 1)] + o_diag[:, GLA_V * h:GLA_V * (h + 1)]
        if nsub > 1:
            ah = jnp.concatenate([jnp.zeros((sub, c_len), F32)] + a_rows[h], axis=0)
            oh = oh + _dot(ah.astype(BF16), v[:, GLA_V * h:GLA_V * (h + 1)].astype(BF16))
        outs.append(oh)

    kl = k * jnp.exp(bl - b)
    vs = jnp.concatenate([v[:, GLA_V * h:GLA_V * (h + 1)] for h in range(GLA_H)], axis=0)
    ks = jnp.concatenate([kl * m for m in masks], axis=0).astype(BF16)
    st_new = st * jnp.exp(bl) + _dot(vs.T.astype(BF16), ks)
    return outs, st_new


def _gla_kernel(*refs, c_len, sub, cps, fast):
    if fast:
        (q_ref, k_ref, v_ref, r_ref, la_ref, s0_ref, g_ref, bexp_ref, tri_ref, keep_ref,
         og_o, sf_o, st_ref, b_ref) = refs
    else:
        q_ref, k_ref, v_ref, r_ref, la_ref, s0_ref, g_ref, bexp_ref, og_o, sf_o, st_ref, b_ref = refs
    t = pl.program_id(1)
    gb = q_ref.shape[0]

    @pl.when(t == 0)
    def _():
        for i in range(gb):
            st_ref[i] = s0_ref[i].T

    g = g_ref[...]

    def emit(i, outs, rs):
        rr = r_ref[i, rs, :].astype(F32)
        for h in range(GLA_H):
            hs = slice(GLA_V * h, GLA_V * (h + 1))
            rh = rr[:, hs]
            og_o[i, rs, hs] = (_rms(outs[h], g) * (rh / (1.0 + jnp.exp(-rh)))).astype(BF16)

    bs = []
    for i in range(gb):
        la = la_ref[i]
        if fast:
            la_hi = la.astype(BF16)
            la_lo = (la - la_hi.astype(F32)).astype(BF16)
            b = _dot(tri_ref[...], la_hi) + _dot(tri_ref[...], la_lo)
        else:
            rows = lax.broadcasted_iota(jnp.int32, (c_len, 1), 0)
            b = jnp.zeros_like(la)
            for j in range(c_len):
                b = b + jnp.where(rows >= j, la[j:j + 1, :], 0.0)
        b_ref[i] = b
        bs.append(b)

    def robust():
        bexp = bexp_ref[...]
        for i in range(gb):
            def body(c, carry, i=i):
                rs = pl.ds(pl.multiple_of(c * c_len, c_len), c_len)
                outs, st_new = _gla_chunk(q_ref[i, rs, :].astype(F32), k_ref[i, rs, :].astype(F32),
                                          v_ref[i, rs, :].astype(F32), b_ref[i, rs, :], st_ref[i], bexp, c_len, sub)
                st_ref[i] = st_new
                emit(i, outs, rs)
                return carry

            lax.fori_loop(0, cps, body, 0)

    if fast:
        low = jnp.min(bs[0])
        for b in bs[1:]:
            low = jnp.minimum(low, jnp.min(b))
        mild = low >= -GLA_FAST_MAX

        @pl.when(mild)
        def _():
            seqs = range(gb)
            outs, sts = _gla_fast_step([q_ref[i].astype(F32) for i in seqs], [k_ref[i].astype(F32) for i in seqs],
                                       [v_ref[i].astype(F32) for i in seqs], [b_ref[i] for i in seqs],
                                       [st_ref[i] for i in seqs], keep_ref[...], c_len, cps)
            for i in seqs:
                st_ref[i] = sts[i]
                emit(i, outs[i], slice(None))

        pl.when(jnp.logical_not(mild))(robust)
    else:
        robust()

    @pl.when(t == pl.num_programs(1) - 1)
    def _():
        for i in range(gb):
            sf_o[i] = st_ref[i].T


def _gla(gq, gk, gv, gr, la, s0, w, batch, seq, c_len, sub, cps, gb):
    ts = c_len * cps
    nt = seq // ts
    fast = c_len == GLA_CHUNK
    assert fast or cps == 1
    row = lambda width: pl.BlockSpec((gb, None, ts, width), lambda b, t: (b, t, 0, 0))
    g4 = lambda a: a.reshape(batch, nt, ts, a.shape[-1])
    st_spec = pl.BlockSpec((gb, GLA_H * GLA_K, GLA_V), lambda b, t: (b, 0, 0))
    consts = [w['g_gla_o'], w['bexp']]
    if fast:
        ti = jnp.arange(ts)
        keep = (ti[:, None] // c_len == ti[None, :] // c_len) & (ti[None, :] <= ti[:, None])
        consts += [keep.astype(BF16), keep.astype(F32)]
    og, s_fin = pl.pallas_call(
        functools.partial(_gla_kernel, c_len=c_len, sub=sub, cps=cps, fast=fast),
        grid=(batch // gb, nt),
        in_specs=[row(256), row(256), row(512), row(512), row(256), st_spec] + [_full(c.shape) for c in consts],
        out_specs=[row(512), st_spec],
        out_shape=[jax.ShapeDtypeStruct((batch, nt, ts, GLA_H * GLA_V), BF16),
                   jax.ShapeDtypeStruct((batch, GLA_H * GLA_K, GLA_V), F32)],
        scratch_shapes=[pltpu.VMEM((gb, GLA_V, GLA_H * GLA_K), F32), pltpu.VMEM((gb, ts, GLA_H * GLA_K), F32)],
        compiler_params=_cparams(("parallel", "arbitrary")),
        name="gla",
    )(g4(gq), g4(gk), g4(gv), g4(gr), g4(la), s0, *consts)
    return og.reshape(batch * seq, GLA_H * GLA_V), s_fin


def _mla_prompt_kernel(q_ref, k_ref, vt_ref, o_ref, m_ref, acc_ref, *, tq):
    i = pl.program_id(1)
    m_ref[...] = jnp.full(m_ref.shape, -jnp.inf, F32)
    acc_ref[...] = jnp.zeros(acc_ref.shape, F32)
    keep = (lax.broadcasted_iota(jnp.int32, (tq, tq), 0) <= lax.broadcasted_iota(jnp.int32, (tq, tq), 1))

    def tiles(js, last_masked):
        n = len(js)
        for h0 in range(0, MLA_H, ATT_HEADS):
            hs = range(h0, h0 + ATT_HEADS)
            sts = {h: [_dot_nt(k_ref[pl.ds(pl.multiple_of(j * tq, tq), tq), LANES * h:LANES * (h + 1)],
                               q_ref[:, LANES * h:LANES * (h + 1)]) for j in js] for h in hs}
            ps, alphas = {}, {}
            for h in hs:
                if last_masked:
                    sts[h][-1] = jnp.where(keep, sts[h][-1], -jnp.inf)
                m_old = m_ref[h]
                m_new = m_old
                for st in sts[h]:
                    m_new = jnp.maximum(m_new, jnp.max(st, axis=0, keepdims=True))
                alphas[h] = jnp.exp2(m_old - m_new)
                ps[h] = [jnp.exp2(st - m_new).astype(BF16) for st in sts[h]]
                m_ref[h] = m_new
            for h in hs:
                acc = alphas[h] * acc_ref[h]
                for t in range(n):
                    acc = acc + _dot(vt_ref[js[t], ATT_VROWS * h:ATT_VROWS * (h + 1), :], ps[h][t])
                acc_ref[h] = acc

    def body(jj, carry):
        tiles([ATT_GROUP * jj + t for t in range(ATT_GROUP)], False)
        return carry

    lax.fori_loop(0, i // ATT_GROUP, body, 0)
    for rem in range(ATT_GROUP):
        pl.when(i % ATT_GROUP == rem)(functools.partial(tiles, [i - rem + t for t in range(rem + 1)], True))

    for pair in range(MLA_H // 2):
        h0, h1 = 2 * pair, 2 * pair + 1
        ab = jnp.concatenate([acc_ref[h, 0:MLA_DV, :] * (1.0 / acc_ref[h, MLA_DV:MLA_DV + 1, :]) for h in (h0, h1)],
                             axis=0)
        o_ref[:, LANES * pair:LANES * (pair + 1)] = ab.T.astype(BF16)


def _mla_prompt(mq, mk, mvt, batch, seq, tq):
    nq = seq // tq
    return pl.pallas_call(
        functools.partial(_mla_prompt_kernel, tq=tq),
        grid=(batch, nq),
        in_specs=[pl.BlockSpec((tq, MLA_H * LANES), lambda b, i: (b * nq + i, 0)),
                  pl.BlockSpec((seq, MLA_H * LANES), lambda b, i: (b, 0)),
                  pl.BlockSpec((nq, MLA_H * ATT_VROWS, tq), lambda b, i: (b, 0, 0))],
        out_specs=pl.BlockSpec((tq, MLA_H * MLA_DV), lambda b, i: (b * nq + i, 0)),
        out_shape=jax.ShapeDtypeStruct((batch * seq, MLA_H * MLA_DV), BF16),
        scratch_shapes=[pltpu.VMEM((MLA_H, 1, tq), F32), pltpu.VMEM((MLA_H, ATT_VROWS, tq), F32)],
        compiler_params=_cparams(("parallel", "arbitrary")),
        name="mla_prompt",
    )(mq, mk, mvt)


def _mla_sample_kernel(pt_ref, q_ref, wkq_ref, wkt_ref, wv_ref, cn_ref, kn_ref, ckv_hbm, kpe_hbm,
                       o_ref, ckv_buf, kpe_buf, sem, *, n_pages, t_new, kpm):
    b = pl.program_id(0)
    nb = pl.num_programs(0)
    half = b % 2
    nrow = MLA_H * t_new

    def page_copies(seq, hf, p):
        page = pt_ref[seq * n_pages + p]
        return (pltpu.make_async_copy(ckv_hbm.at[page], ckv_buf.at[hf, p], sem.at[0, hf]),
                pltpu.make_async_copy(kpe_hbm.at[page], kpe_buf.at[hf, p], sem.at[1, hf]))

    def start_pages(seq, hf, pages):
        for p in pages:
            for cp in page_copies(seq, hf, p):
                cp.start()

    def wait_pages(hf):
        pltpu.make_async_copy(ckv_hbm.at[pl.ds(0, n_pages)], ckv_buf.at[hf], sem.at[0, hf]).wait()
        pltpu.make_async_copy(kpe_hbm.at[pl.ds(0, n_pages)], kpe_buf.at[hf], sem.at[1, hf]).wait()

    @pl.when(b == 0)
    def _():
        start_pages(0, 0, range(n_pages))

    nxt = (b + 1) % nb
    start_pages(nxt, 1 - half, range(n_pages))

    wait_pages(half)

    qa, qr = [], []
    for h in range(MLA_H):
        qh = q_ref[:, LANES * h:LANES * (h + 1)]
        qa.append(_dot(qh, wkq_ref[h]))
        qr.append(pltpu.roll(qh.astype(F32), LANES - _KPE_LANE, 1)[:, 0:MLA_ROPE])
    lq = jnp.concatenate([jnp.concatenate(qa, axis=0).astype(BF16), wkt_ref[...]], axis=0)
    qp = jnp.concatenate(qr, axis=0).astype(BF16)

    def scores(r, s_rope):
        rinv = []
        for h in range(MLA_H):
            kr = r[nrow + MLA_NOPE * h:nrow + MLA_NOPE * (h + 1)]
            ms = jnp.sum(kr * kr, axis=0, keepdims=True) * (1.0 / MLA_NOPE)
            rinv.append(jnp.broadcast_to(lax.rsqrt(ms + EPS), (t_new, r.shape[1])))
        return r[0:nrow] * jnp.concatenate(rinv, axis=0) + s_rope

    ngrp = n_pages // kpm
    xs, ss = [], []
    for g0 in range(0, ngrp, SAMPLE_GROUPS):
        gs = range(g0, min(g0 + SAMPLE_GROUPS, ngrp))
        xg = [jnp.concatenate([ckv_buf[half, g * kpm + j] for j in range(kpm)], axis=0).astype(BF16) for g in gs]
        rg = [_dot_nt(lq, x) for x in xg]
        pg = [_dot(qp, jnp.concatenate([kpe_buf[half, g * kpm + j] for j in range(kpm)], axis=1).astype(BF16))
              for g in gs]
        xs.append(jnp.concatenate(xg, axis=0))
        ss.append(jnp.concatenate([scores(r, s_rope) for r, s_rope in zip(rg, pg)], axis=1))

    xn = cn_ref[...].astype(BF16)
    sn = scores(_dot_nt(lq, xn), _dot_nt(qp, kn_ref[...].astype(BF16)))
    tq = lax.broadcasted_iota(jnp.int32, (nrow, t_new), 0) % t_new
    ts = lax.broadcasted_iota(jnp.int32, (nrow, t_new), 1)
    xs.append(xn)
    ss.append(jnp.where(ts <= tq, sn, -jnp.inf))

    ms = [jnp.max(s, axis=-1, keepdims=True) for s in ss]
    ps = [jnp.exp2(s - mi) for s, mi in zip(ss, ms)]
    ls = [jnp.sum(p, axis=-1, keepdims=True) for p in ps]
    pvs = [_dot(p.astype(BF16), x) for p, x in zip(ps, xs)]
    m = ms[0]
    for mi in ms[1:]:
        m = jnp.maximum(m, mi)
    ws = [jnp.exp2(mi - m) for mi in ms]
    l = ws[0] * ls[0]
    oa = ws[0] * pvs[0]
    for wi, li, pv in zip(ws[1:], ls[1:], pvs[1:]):
        l = l + wi * li
        oa = oa + wi * pv
    oa = oa / l

    r = _dot(oa.astype(BF16), wv_ref[...])
    lane = lax.broadcasted_iota(jnp.int32, (1, MLA_H * MLA_DV), 1)
    out = jnp.zeros((t_new, MLA_H * MLA_DV), F32)
    for h in range(MLA_H):
        out = out + jnp.where(lane // MLA_DV == h, r[t_new * h:t_new * (h + 1)], 0.0)
    o_ref[...] = out.astype(BF16)

    @pl.when(b == nb - 1)
    def _():
        wait_pages(1 - half)


def _mla_sample(mq, ckv_new, kp_new, pool_ckv, pool_kpe_t, page_table, w, batch, t_new, kpm):
    n_pages = page_table.shape[1]
    pt = page_table.reshape(-1)
    tok = lambda width: pl.BlockSpec((None, t_new, width), lambda b, pt_ref: (b, 0, 0))
    g3 = lambda a: a.reshape(batch, t_new, a.shape[-1])
    const = lambda a: pl.BlockSpec(a.shape, lambda b, pt_ref: (0,) * a.ndim)
    hbm = pl.BlockSpec(memory_space=pl.ANY)
    grid_spec = pltpu.PrefetchScalarGridSpec(
        num_scalar_prefetch=1,
        grid=(batch,),
        in_specs=[tok(MLA_H * LANES), const(w['w_kq']), const(w['w_kt']), const(w['w_v']),
                  tok(MLA_KVR), tok(MLA_ROPE), hbm, hbm],
        out_specs=tok(MLA_H * MLA_DV),
        scratch_shapes=[pltpu.VMEM((2, n_pages, PAGE, MLA_KVR), F32),
                        pltpu.VMEM((2, n_pages, MLA_ROPE, PAGE), F32),
                        pltpu.SemaphoreType.DMA((2, 2))],
    )
    om = pl.pallas_call(
        functools.partial(_mla_sample_kernel, n_pages=n_pages, t_new=t_new, kpm=kpm),
        grid_spec=grid_spec,
        out_shape=jax.ShapeDtypeStruct((batch, t_new, MLA_H * MLA_DV), BF16),
        compiler_params=_cparams(("arbitrary",)),
        name="mla_sample",
    )(pt, g3(mq), w['w_kq'], w['w_kt'], w['w_v'], g3(ckv_new), g3(kp_new), pool_ckv, pool_kpe_t)
    return om.reshape(batch * t_new, MLA_H * MLA_DV)


def _mix_out_kernel(x_ref, og_ref, om_ref, wo_ref, gx_ref, wxq_ref, gxq_ref, x1_o, xq_o):
    x1 = x_ref[...] + _dot(og_ref[...], wo_ref[0:512, :]) + _dot(om_ref[...], wo_ref[512:1024, :])
    x1_o[...] = x1
    qf = _dot(_rms(x1, gx_ref[...]).astype(BF16), wxq_ref[...])
    g = gxq_ref[...]
    for h in range(X_H):
        hs = slice(X_D * h, X_D * (h + 1))
        xq_o[:, hs] = _rms(qf[:, hs], g).astype(BF16)


def _mix_out(x, og, om, w, tm):
    n = x.shape[0]
    row = lambda width: pl.BlockSpec((tm, width), lambda i: (i, 0))
    consts = [w['w_out'], w['g_x'], w['w_xq'], w['g_xq']]
    return pl.pallas_call(
        _mix_out_kernel,
        grid=(n // tm,),
        in_specs=[row(D_MODEL), row(512), row(512)] + [_full(c.shape) for c in consts],
        out_specs=[row(D_MODEL), row(X_H * X_D)],
        out_shape=[jax.ShapeDtypeStruct((n, D_MODEL), F32), jax.ShapeDtypeStruct((n, X_H * X_D), BF16)],
        compiler_params=_cparams(("parallel",)),
        name="mix_out",
    )(x, og, om, *consts)


def _cross_kernel(xq_ref, mk_ref, mv_ref, o_ref, *, n_mem):
    gb = xq_ref.shape[0]

    def head(ref, g, h):
        if len(ref.shape) == 4:
            return ref[g, h]
        return ref[pl.ds(g * n_mem * X_H + h, n_mem, stride=X_H), :].astype(BF16)

    units = [(g, h) for g in range(gb) for h in range(X_H)]
    ss = [_dot_nt(xq_ref[g, :, X_D * h:X_D * (h + 1)], head(mk_ref, g, h)) for g, h in units]
    ps = [jnp.exp2(s - jnp.max(s, axis=-1, keepdims=True)) for s in ss]
    for (g, h), p in zip(units, ps):
        o = _dot(p.astype(BF16), head(mv_ref, g, h))
        o_ref[g, :, X_D * h:X_D * (h + 1)] = (o * (1.0 / jnp.sum(p, axis=-1, keepdims=True))).astype(BF16)


def _cross(xq, mk, mv, batch, seq, n_mem, tq, gb):
    nq = seq // tq
    assert gb == 1 or nq == 1
    row = pl.BlockSpec((gb, tq, X_H * X_D), lambda b, i: (b * nq + i, 0, 0))
    if mk.ndim == 4:
        mem = pl.BlockSpec((gb,) + mk.shape[1:], lambda b, i: (b, 0, 0, 0))
    else:
        mem = pl.BlockSpec((gb * n_mem * X_H, X_D), lambda b, i: (b, 0))
    o = pl.pallas_call(
        functools.partial(_cross_kernel, n_mem=n_mem),
        grid=(batch // gb, nq),
        in_specs=[row, mem, mem],
        out_specs=row,
        out_shape=jax.ShapeDtypeStruct((batch * nq, tq, X_H * X_D), BF16),
        compiler_params=_cparams(("parallel", "arbitrary")),
        name="cross",
    )(xq.reshape(batch * nq, tq, X_H * X_D), mk, mv)
    return o.reshape(batch * seq, X_H * X_D)


def _ffn_kernel(x_ref, o_ref, wxo_ref, g_ref, w1_ref, w2_ref, y_o, *, fc):
    x = x_ref[...] + _dot(o_ref[...], wxo_ref[...])
    h = _rms(x, g_ref[...]).astype(BF16)
    acc = x
    for c in range(D_FF // fc):
        u = jnp.maximum(_dot(h, w1_ref[:, fc * c:fc * (c + 1)]), 0.0)
        acc = acc + _dot((u * u).astype(BF16), w2_ref[fc * c:fc * (c + 1), :])
    y_o[...] = acc


def _ffn(x, o, w, tm, fc):
    n = x.shape[0]
    row = lambda width: pl.BlockSpec((tm, width), lambda i: (i, 0))
    consts = [w['w_xo'], w['g_ff'], w['w_ff1'], w['w_ff2']]
    return pl.pallas_call(
        functools.partial(_ffn_kernel, fc=fc),
        grid=(n // tm,),
        in_specs=[row(D_MODEL), row(X_H * X_D)] + [_full(c.shape) for c in consts],
        out_specs=row(D_MODEL),
        out_shape=jax.ShapeDtypeStruct((n, D_MODEL), F32),
        compiler_params=_cparams(("parallel",)),
        name="ffn",
    )(x, o, *consts)


def _mem_kv_kernel(m_ref, gm_ref, wk_ref, wv_ref, gk_ref, k_o, v_o, kh_o, vh_o, *, n_mem):
    tm = m_ref.shape[0]
    m = _rms(m_ref[...], gm_ref[...]).astype(BF16)
    kf = _dot(m, wk_ref[...])
    vf = _dot(m, wv_ref[...])
    g = gk_ref[...]
    for h in range(X_H):
        hs = slice(X_D * h, X_D * (h + 1))
        kn = _rms(kf[:, hs], g)
        k_o[pl.ds(h, tm, stride=X_H), :] = kn
        v_o[pl.ds(h, tm, stride=X_H), :] = vf[:, hs]
        for b in range(kh_o.shape[0]):
            rs = slice(n_mem * b, n_mem * (b + 1))
            kh_o[b, h] = kn[rs].astype(BF16)
            vh_o[b, h] = vf[rs, hs].astype(BF16)


def _mem_kv(mem, w, n_mem, tm):
    n = mem.shape[0]
    bt = tm // n_mem
    row = lambda width: pl.BlockSpec((tm, width), lambda i: (i, 0))
    head = pl.BlockSpec((bt, X_H, n_mem, X_D), lambda i: (i, 0, 0, 0))
    consts = [w['g_mem'], w['w_xk'], w['w_xv'], w['g_xk']]
    return pl.pallas_call(
        functools.partial(_mem_kv_kernel, n_mem=n_mem),
        grid=(n // tm,),
        in_specs=[row(D_MODEL)] + [_full(c.shape) for c in consts],
        out_specs=[pl.BlockSpec((tm * X_H, X_D), lambda i: (i, 0))] * 2 + [head, head],
        out_shape=[jax.ShapeDtypeStruct((n * X_H, X_D), F32)] * 2
                  + [jax.ShapeDtypeStruct((n // n_mem, X_H, n_mem, X_D), BF16)] * 2,
        compiler_params=_cparams(("parallel",)),
        name="mem_kv",
    )(mem, *consts)


def _prep_weights(g_mix, w_in, w_gla_a2, b_gla_a, g_gla_o, g_mla_qa, w_mla_qb, g_mla_kva, w_mla_kvb,
                  g_q_nope, g_k_nope, g_q_rope, g_k_rope, w_out, g_x, g_mem, w_xq, w_xk, w_xv, g_xq, g_xk,
                  w_xo, g_ff, w_ff1, w_ff2):
    rowv = lambda g: g.reshape(1, -1).astype(F32)
    zc = lambda n: jnp.zeros((D_MODEL, n), F32)
    sizes = (256, 256, 512, 512, GLA_RANK, MLA_QR, MLA_KVR, MLA_ROPE)
    offs = [0]
    for s in sizes:
        offs.append(offs[-1] + s)
    q, k, v, r, a, qa, kva, kpe = [w_in[:, offs[i]:offs[i + 1]] for i in range(8)]
    w_in_p = jnp.concatenate(
        [q * (GLA_K ** -0.5), k, v, r, qa,
         a, zc(_KPE_LANE - GLA_RANK), kpe, zc(LANES - _KPE_LANE - MLA_ROPE), kva], axis=1)
    w_a2 = jnp.concatenate([w_gla_a2, jnp.zeros((LANES - GLA_RANK, GLA_H * GLA_K), F32)], axis=0)

    qb = w_mla_qb.reshape(MLA_QR, MLA_H, MLA_NOPE + MLA_ROPE)
    qb = jnp.pad(qb, ((0, 0), (0, 0), (0, LANES - MLA_NOPE - MLA_ROPE))).reshape(MLA_QR, MLA_H * LANES)
    kvb = w_mla_kvb.reshape(MLA_KVR, MLA_H, MLA_NOPE + MLA_DV)
    wk = kvb[:, :, :MLA_NOPE]
    wk_p = jnp.pad(wk, ((0, 0), (0, 0), (0, LANES - MLA_NOPE))).reshape(MLA_KVR, MLA_H * LANES)
    wv = kvb[:, :, MLA_NOPE:].reshape(MLA_KVR, MLA_H * MLA_DV)
    z32 = jnp.zeros((LANES - MLA_NOPE - MLA_ROPE,), F32)
    gq_row = jnp.tile(jnp.concatenate([g_q_nope, g_q_rope, z32]), MLA_H) * (MLA_SCALE * LOG2E)
    gk_row = jnp.tile(jnp.concatenate([g_k_nope, jnp.zeros((LANES - MLA_NOPE,), F32)]), MLA_H)
    g_kpe = jnp.concatenate([jnp.zeros((_KPE_LANE,), F32), g_k_rope, z32])

    li = jnp.arange(2 * LANES)
    seg = jnp.where(li % LANES < MLA_NOPE, 0, jnp.where(li % LANES < MLA_NOPE + MLA_ROPE, 1, 2))
    same = (li[:, None] // LANES == li[None, :] // LANES) & (seg[:, None] == seg[None, :]) & (seg[:, None] < 2)
    bd = jnp.where(same, jnp.where(seg[:, None] == 0, 1.0 / MLA_NOPE, 1.0 / MLA_ROPE), 0.0)

    wkq = jnp.transpose(wk, (1, 2, 0)) * g_k_nope[None, :, None]
    wkq = jnp.pad(wkq, ((0, 0), (0, LANES - MLA_NOPE), (0, 0)))
    wkt = jnp.transpose(wk, (1, 2, 0)).reshape(MLA_H * MLA_NOPE, MLA_KVR)

    hk = jnp.arange(GLA_H * GLA_K) // GLA_K
    hv = jnp.arange(GLA_H * GLA_V) // GLA_V
    bexp = (hk[:, None] == hv[None, :])
    return {
        'g_mix': rowv(g_mix), 'w_in': w_in_p.astype(BF16), 'w_a2': w_a2.astype(BF16), 'b_a': rowv(b_gla_a),
        'g_qa': rowv(g_mla_qa), 'w_qb': qb.astype(BF16), 'gq_row': rowv(gq_row), 'g_kva': rowv(g_mla_kva),
        'w_kv': jnp.concatenate([wk_p, wv], axis=1).astype(BF16), 'gk_row': rowv(gk_row),
        'w_v': wv.astype(BF16), 'g_kpe': rowv(g_kpe),
        'bd': bd.astype(BF16), 'w_kq': wkq.astype(BF16), 'w_kt': wkt.astype(BF16),
        'g_gla_o': rowv(g_gla_o), 'bexp': bexp.astype(BF16),
        'w_out': w_out.astype(BF16), 'g_x': rowv(g_x), 'w_xq': w_xq.astype(BF16),
        'g_xq': rowv(g_xq) * (X_SCALE * LOG2E),
        'g_mem': rowv(g_mem), 'w_xk': w_xk.astype(BF16), 'w_xv': w_xv.astype(BF16), 'g_xk': rowv(g_xk),
        'w_xo': w_xo.astype(BF16), 'g_ff': rowv(g_ff), 'w_ff1': w_ff1.astype(BF16), 'w_ff2': w_ff2.astype(BF16),
    }


def _rope_tables(pos):
    half = MLA_ROPE // 2
    inv = ROPE_THETA ** (-jnp.arange(half, dtype=F32) / half)
    ang = pos.astype(F32)[:, None] * inv[None, :]
    cos, sin = jnp.cos(ang), jnp.sin(ang)
    n = pos.shape[0]
    one = jnp.ones((n, _KPE_LANE), F32)
    z = lambda w_: jnp.zeros((n, w_), F32)
    tail = LANES - _KPE_LANE - MLA_ROPE
    c = jnp.concatenate([one, cos, cos, jnp.ones((n, tail), F32)], axis=1)
    s_up = jnp.concatenate([z(_KPE_LANE), -sin, z(half), z(tail)], axis=1)
    s_dn = jnp.concatenate([z(_KPE_LANE), z(half), sin, z(tail)], axis=1)
    return c, s_up, s_dn


def _tile_rows(n, cap):
    t = min(n, cap)
    while n % t:
        t //= 2
    return t


def _tail(x1, xq, mk, mv, w, batch, seq, n_mem):
    tq = _tile_rows(seq, 512)
    gb = _tile_rows(batch, max(1, CROSS_ROWS // seq)) if tq == seq else 1
    o = _cross(xq, mk, mv, batch, seq, n_mem, tq, gb)
    return _ffn(x1, o, w, _tile_rows(x1.shape[0], 512), 1024)


def _prompt_layer(x, mem, w):
    batch, seq, _ = x.shape
    n = batch * seq
    xf = x.reshape(n, D_MODEL)
    tm = _tile_rows(seq, 512)
    tabs = _rope_tables(jnp.arange(seq))
    gq, gk, gv, gr, la, mq, ckv, kp, mk_, mvt = _in_proj(xf, w, tabs, tm, True)
    c_len = GLA_CHUNK
    cps = _tile_rows(seq // c_len, GLA_CHUNKS_PER_STEP)
    s0 = jnp.zeros((batch, GLA_H * GLA_K, GLA_V), F32)
    og, s_fin = _gla(gq, gk, gv, gr, la, s0, w, batch, seq, c_len, min(GLA_SUB, c_len), cps,
                     _tile_rows(batch, GLA_SEQS_PER_STEP))
    om = _mla_prompt(mq, mk_, mvt, batch, seq, ATT_TILE)
    x1, xq = _mix_out(xf, og, om, w, tm)
    n_mem = mem.shape[1]
    memf = mem.reshape(-1, D_MODEL)
    xk, xv, xkh, xvh = _mem_kv(memf, w, n_mem, max(n_mem, _tile_rows(memf.shape[0], 512)))
    y = _tail(x1, xq, xkh, xvh, w, batch, seq, n_mem)
    return y, ckv, kp, xk, xv, s_fin


def _sample_layer(x, pool_ckv, pool_kpe, page_table, mem_k, mem_v, s_prev, w):
    batch, seq, _ = x.shape
    n = batch * seq
    xf = x.reshape(n, D_MODEL)
    tm = _tile_rows(n, 512)
    past = page_table.shape[1] * PAGE
    pos = past + (jnp.arange(tm) % seq)
    gq, gk, gv, gr, la, mq, ckv, kp = _in_proj(xf, w, _rope_tables(pos), tm, False)
    s0 = s_prev.reshape(batch, GLA_H * GLA_K, GLA_V)
    og, s_fin = _gla(gq, gk, gv, gr, la, s0, w, batch, seq, seq, seq, 1, _tile_rows(batch, GLA_SHORT_SEQS_PER_STEP))
    om = _mla_sample(mq, ckv, kp, pool_ckv, jnp.swapaxes(pool_kpe, 1, 2), page_table, w, batch, seq,
                     _tile_rows(page_table.shape[1], SAMPLE_PAGES_PER_DOT))
    x1, xq = _mix_out(xf, og, om, w, tm)
    y = _tail(x1, xq, mem_k.reshape(-1, X_D), mem_v.reshape(-1, X_D), w, batch, seq, mem_k.shape[1])
    return y, ckv, kp, s_fin


def kernel(x_prompt, x_sample, mem_prompt, cache_ckv, cache_kpe, cache_mem_k, cache_mem_v, state_gla, page_table, g_mix, w_in, w_gla_a2, b_gla_a, g_gla_o, g_mla_qa, w_mla_qb, g_mla_kva, w_mla_kvb, g_q_nope, g_k_nope, g_q_rope, g_k_rope, w_out, g_x, g_mem, w_xq, w_xk, w_xv, g_xq, g_xk, w_xo, g_ff, w_ff1, w_ff2):
    depth = w_in.shape[0]
    assert depth == 1, "one layer: prompt-group caches of layer l would feed layer l+1 otherwise unchanged"
    params = (g_mix, w_in, w_gla_a2, b_gla_a, g_gla_o, g_mla_qa, w_mla_qb, g_mla_kva, w_mla_kvb,
              g_q_nope, g_k_nope, g_q_rope, g_k_rope, w_out, g_x, g_mem, w_xq, w_xk, w_xv, g_xq, g_xk,
              w_xo, g_ff, w_ff1, w_ff2)
    w = _prep_weights(*[p[0] for p in params])
    bp, tp, _ = x_prompt.shape
    bs, tsq, _ = x_sample.shape
    yp, ckv_p, kp_p, xk, xv, gla_p = _prompt_layer(x_prompt, mem_prompt, w)
    ys, ckv_s, kp_s, gla_s = _sample_layer(x_sample, cache_ckv.reshape(cache_ckv.shape[1:]), cache_kpe.reshape(cache_kpe.shape[1:]), page_table,
                                            cache_mem_k[0], cache_mem_v[0], state_gla[0], w)
    n_mem = mem_prompt.shape[1]
    return (yp.reshape(bp, tp, D_MODEL), ys.reshape(bs, tsq, D_MODEL),
            ckv_p.reshape(1, bp, tp, MLA_KVR), kp_p.reshape(1, bp, tp, MLA_ROPE),
            xk.reshape(1, bp, n_mem, X_H, X_D), xv.reshape(1, bp, n_mem, X_H, X_D),
            gla_p.reshape(1, bp, GLA_H, GLA_K, GLA_V),
            ckv_s.reshape(1, bs, tsq, MLA_KVR), kp_s.reshape(1, bs, tsq, MLA_ROPE),
            gla_s.reshape(1, bs, GLA_H, GLA_K, GLA_V))
```

```python
import functools

import jax
import jax.numpy as jnp
from jax import lax
from jax.experimental import pallas as pl
from jax.experimental.pallas import tpu as pltpu

F32 = jnp.float32
BF16 = jnp.bfloat16

EPS = 1e-6
D_MODEL = 1024
GLA_H, GLA_K, GLA_V, GLA_RANK, GLA_TAU, GLA_CHUNK = 4, 64, 128, 16, 16.0, 64
GLA_SUB = 16
GLA_CHUNKS_PER_STEP = 4
GLA_SEQS_PER_STEP = 8
GLA_SHORT_SEQS_PER_STEP = 8
GLA_FAST_MAX = 40.0
MLA_H, MLA_DV, MLA_NOPE, MLA_ROPE, MLA_QR, MLA_KVR = 8, 64, 64, 32, 384, 256
MLA_SCALE = (MLA_NOPE + MLA_ROPE) ** -0.5
ROPE_THETA = 10000.0
X_H, X_D = 4, 128
X_SCALE = X_D ** -0.5
D_FF = 4096
PAGE = 128
LANES = 128
NEG = -1e30
LOG2E = 1.4426950408889634
ATT_TILE = 256
ATT_VROWS = 80
ATT_HEADS = 8
ATT_GROUP = 2
CROSS_ROWS = 32
SAMPLE_PAGES_PER_DOT = 4
SAMPLE_GROUPS = 4

_C_Q, _C_K, _C_V, _C_R = 0, 256, 512, 1024
_C_QA, _C_AK, _C_KVA, _C_END = 1536, 1920, 2048, 2304
_KPE_LANE = 64

VMEM_LIMIT = 56 * 1024 * 1024


def _cparams(sem):
    return pltpu.CompilerParams(dimension_semantics=sem, vmem_limit_bytes=VMEM_LIMIT)


def _dot(a, b):
    return jnp.dot(a, b, preferred_element_type=F32)


def _dot_nt(a, b):
    return lax.dot_general(a, b, (((1,), (1,)), ((), ())), preferred_element_type=F32)


def _rms(x, g):
    return x * lax.rsqrt(jnp.mean(x * x, axis=-1, keepdims=True) + EPS) * g


def _full(shape):
    n = len(shape)
    return pl.BlockSpec(shape, lambda *_: (0,) * n)


def _in_proj_kernel(x_ref, gmix_ref, win_ref, wa2_ref, ba_ref, gqa_ref, wqb_ref, gqrow_ref,
                    gkva_ref, wkv_ref, gkrow_ref, gkpe_ref, bd_ref, c_ref, s1_ref, s2_ref,
                    gq_o, gk_o, gv_o, gr_o, la_o, mq_o, ckv_o, kp_o, *kv_o):
    h = _rms(x_ref[...], gmix_ref[...]).astype(BF16)

    z2 = _dot(h, win_ref[:, _C_QA:_C_END])
    qa = z2[:, 0:_C_AK - _C_QA]
    zak = z2[:, _C_AK - _C_QA:_C_KVA - _C_QA]
    kva = z2[:, _C_KVA - _C_QA:]

    gate = _dot(zak.astype(BF16), wa2_ref[...]) + ba_ref[...]
    la_o[...] = (jnp.minimum(gate, 0.0) - jnp.log1p(jnp.exp(-jnp.abs(gate)))) * (1.0 / GLA_TAU)

    cos, s_up, s_dn = c_ref[...], s1_ref[...], s2_ref[...]

    def rope(t):
        return t * cos + pltpu.roll(t, LANES - 16, 1) * s_up + pltpu.roll(t, 16, 1) * s_dn

    lane = lax.broadcasted_iota(jnp.int32, (1, LANES), 1)
    zk = jnp.where((lane >= _KPE_LANE) & (lane < _KPE_LANE + MLA_ROPE), zak, 0.0)
    kpn = zk * lax.rsqrt(jnp.sum(zk * zk, axis=-1, keepdims=True) * (1.0 / MLA_ROPE) + EPS) * gkpe_ref[...]
    kpt = rope(kpn)
    kp_o[...] = pltpu.roll(kpt, LANES - _KPE_LANE, 1)[:, 0:MLA_ROPE]

    ckv = _rms(kva, gkva_ref[...])
    ckv_o[...] = ckv
    cb = ckv.astype(BF16)
    bd = bd_ref[...]

    def seg_norm(t, grow):
        ms = _dot((t * t).astype(BF16), bd)
        return t * lax.rsqrt(ms + EPS) * grow

    qf = _dot(_rms(qa, gqa_ref[...]).astype(BF16), wqb_ref[...])
    for c in range(4):
        sl = slice(2 * LANES * c, 2 * LANES * (c + 1))
        qn = seg_norm(qf[:, sl], gqrow_ref[:, sl])
        for j in range(2):
            o = 2 * LANES * c + LANES * j
            mq_o[:, o:o + LANES] = rope(qn[:, LANES * j:LANES * (j + 1)]).astype(BF16)

    if kv_o:
        mk_o, mvt_o = kv_o
        kvf = _dot(cb, wkv_ref[...])
        kf, mv = kvf[:, 0:MLA_H * LANES], kvf[:, MLA_H * LANES:]
        ones = jnp.ones((ATT_VROWS - MLA_DV, ATT_TILE), BF16)
        for c in range(mvt_o.shape[0]):
            mvt = mv[ATT_TILE * c:ATT_TILE * (c + 1), :].T.astype(BF16)
            for hh in range(MLA_H):
                mvt_o[c, ATT_VROWS * hh:ATT_VROWS * hh + MLA_DV, :] = mvt[MLA_DV * hh:MLA_DV * (hh + 1)]
                mvt_o[c, ATT_VROWS * hh + MLA_DV:ATT_VROWS * (hh + 1), :] = ones
        for c in range(4):
            sl = slice(2 * LANES * c, 2 * LANES * (c + 1))
            kn = seg_norm(kf[:, sl], gkrow_ref[:, sl])
            for j in range(2):
                o = 2 * LANES * c + LANES * j
                mk_o[:, o:o + LANES] = (kn[:, LANES * j:LANES * (j + 1)] + kpt).astype(BF16)

    z1 = _dot(h, win_ref[:, _C_Q:_C_QA])
    gq_o[...] = z1[:, _C_Q:_C_K].astype(BF16)
    gk_o[...] = z1[:, _C_K:_C_V].astype(BF16)
    gv_o[...] = z1[:, _C_V:_C_R].astype(BF16)
    gr_o[...] = z1[:, _C_R:_C_QA].astype(BF16)


def _in_proj(x, w, tabs, tm, with_kv):
    n = x.shape[0]
    nt = tabs[0].shape[0] // tm
    row = lambda width: pl.BlockSpec((tm, width), lambda i: (i, 0))
    tab = pl.BlockSpec((tm, LANES), lambda i: (i % nt, 0))
    consts = [w['g_mix'], w['w_in'], w['w_a2'], w['b_a'], w['g_qa'], w['w_qb'], w['gq_row'],
              w['g_kva'], w['w_kv'], w['gk_row'], w['g_kpe'], w['bd']]
    out_w = [(256, BF16), (256, BF16), (512, BF16), (512, BF16), (256, F32),
             (1024, BF16), (256, F32), (MLA_ROPE, F32)] + ([(1024, BF16)] if with_kv else [])
    out_specs = [row(wd) for wd, _ in out_w]
    out_shape = [jax.ShapeDtypeStruct((n, wd), dt) for wd, dt in out_w]
    if with_kv:
        out_specs.append(pl.BlockSpec((tm // ATT_TILE, MLA_H * ATT_VROWS, ATT_TILE), lambda i: (i, 0, 0)))
        out_shape.append(jax.ShapeDtypeStruct((n // ATT_TILE, MLA_H * ATT_VROWS, ATT_TILE), BF16))
    return pl.pallas_call(
        _in_proj_kernel,
        grid=(n // tm,),
        in_specs=[row(D_MODEL)] + [_full(c.shape) for c in consts] + [tab, tab, tab],
        out_specs=out_specs,
        out_shape=out_shape,
        compiler_params=_cparams(("parallel",)),
        name="in_proj",
    )(x, *consts, *tabs)


def _head_masks():
    lane = lax.broadcasted_iota(jnp.int32, (1, GLA_H * GLA_K), 1)
    return [(lane // GLA_K == h).astype(F32) for h in range(GLA_H)]


def _gla_fast_step(qs, ks, vs, bs, sts, keep, c_len, cps):
    masks = _head_masks()
    ts = c_len * cps
    n = len(qs)
    ebs = [jnp.exp(b) for b in bs]
    kts = [k * jnp.exp(-b) for k, b in zip(ks, bs)]
    lqs = [jnp.concatenate([q * eb * m for m in masks], axis=0).astype(BF16)
           for q, eb in zip(qs, ebs)]
    atts = [_dot_nt(lq, kt.astype(BF16)) for lq, kt in zip(lqs, kts)]
    vbs = [v.astype(BF16) for v in vs]
    o_intra = [[_dot((atts[g][ts * h:ts * (h + 1)] * keep).astype(BF16), vbs[g][:, GLA_V * h:GLA_V * (h + 1)])
                for h in range(GLA_H)] for g in range(n)]
    o_inter = [[] for _ in range(n)]
    sts = list(sts)
    for c in range(cps):
        cs = slice(c_len * c, c_len * (c + 1))
        for g in range(n):
            lqc = jnp.concatenate([lqs[g][ts * h + c_len * c:ts * h + c_len * (c + 1)] for h in range(GLA_H)],
                                  axis=0)
            o_inter[g].append(_dot_nt(lqc, sts[g].astype(BF16)))
            ebl = ebs[g][c_len * (c + 1) - 1:c_len * (c + 1), :]
            kl = kts[g][cs] * ebl
            vst = jnp.concatenate([vs[g][cs, GLA_V * h:GLA_V * (h + 1)] for h in range(GLA_H)], axis=0)
            kst = jnp.concatenate([kl * m for m in masks], axis=0).astype(BF16)
            sts[g] = sts[g] * ebl + _dot(vst.T.astype(BF16), kst)
    outs = [[o_intra[g][h] + jnp.concatenate([o_inter[g][c][c_len * h:c_len * (h + 1)] for c in range(cps)],
                                             axis=0) for h in range(GLA_H)] for g in range(n)]
    return outs, sts


def _gla_chunk(q, k, v, b, st, bexp, c_len, sub):
    masks = _head_masks()
    bl = b[c_len - 1:c_len, :]

    qh = q * jnp.exp(b)
    lq = jnp.concatenate([qh * m for m in masks], axis=0).astype(BF16)
    o_inter = _dot_nt(lq, st.astype(BF16))

    nsub = c_len // sub
    a_rows = [[] for _ in range(GLA_H)]
    col = lax.broadcasted_iota(jnp.int32, (1, c_len), 1)
    for i in range(1, nsub):
        ref = b[sub * i - 1:sub * i, :]
        qi = q[sub * i:sub * (i + 1)] * jnp.exp(b[sub * i:sub * (i + 1)] - ref)
        ki = k * jnp.exp(jnp.minimum(ref - b, 0.0))
        li = jnp.concatenate([qi * m for m in masks], axis=0).astype(BF16)
        ai = jnp.where(col < sub * i, _dot_nt(li, ki.astype(BF16)), 0.0)
        for h in range(GLA_H):
            a_rows[h].append(ai[sub * h:sub * (h + 1)])

    row = lax.broadcasted_iota(jnp.int32, (sub, 1), 0)
    o_diag = []
    for i in range(nsub):
        sl = slice(sub * i, sub * (i + 1))
        bb, qb, kb, vb = b[sl], q[sl], k[sl], v[sl]
        ps = []
        for s in range(sub):
            e = jnp.exp(jnp.where(row >= s, bb - bb[s:s + 1, :], NEG))
            ps.append(qb * e * kb[s:s + 1, :])
        r = _dot(jnp.concatenate(ps, axis=0).astype(BF16), bexp)
        od = r[0:sub] * vb[0:1, :]
        for s in range(1, sub):
            od = od + r[sub * s:sub * (s + 1)] * vb[s:s + 1, :]
        o_diag.append(od)
    o_diag = jnp.concatenate(o_diag, axis=0) if nsub > 1 else o_diag[0]

    outs = []
    for h in range(GLA_H):
        oh = o_inter[c_len * h:c_len * (h + 1)] + o_diag[:, GLA_V * h:GLA_V * (h + 1)]
        if nsub > 1:
            ah = jnp.concatenate([jnp.zeros((sub, c_len), F32)] + a_rows[h], axis=0)
            oh = oh + _dot(ah.astype(BF16), v[:, GLA_V * h:GLA_V * (h + 1)].astype(BF16))
        outs.append(oh)

    kl = k * jnp.exp(bl - b)
    vs = jnp.concatenate([v[:, GLA_V * h:GLA_V * (h + 1)] for h in range(GLA_H)], axis=0)
    ks = jnp.concatenate([kl * m for m in masks], axis=0).astype(BF16)
    st_new = st * jnp.exp(bl) + _dot(vs.T.astype(BF16), ks)
    return outs, st_new


def _gla_kernel(*refs, c_len, sub, cps, fast):
    if fast:
        (q_ref, k_ref, v_ref, r_ref, la_ref, s0_ref, g_ref, bexp_ref, tri_ref, keep_ref,
         og_o, sf_o, st_ref, b_ref) = refs
    else:
        q_ref, k_ref, v_ref, r_ref, la_ref, s0_ref, g_ref, bexp_ref, og_o, sf_o, st_ref, b_ref = refs
    t = pl.program_id(1)
    gb = q_ref.shape[0]

    @pl.when(t == 0)
    def _():
        for i in range(gb):
            st_ref[i] = s0_ref[i].T

    g = g_ref[...]

    def emit(i, outs, rs):
        rr = r_ref[i, rs, :].astype(F32)
        for h in range(GLA_H):
            hs = slice(GLA_V * h, GLA_V * (h + 1))
            rh = rr[:, hs]
            og_o[i, rs, hs] = (_rms(outs[h], g) * (rh / (1.0 + jnp.exp(-rh)))).astype(BF16)

    bs = []
    for i in range(gb):
        la = la_ref[i]
        if fast:
            la_hi = la.astype(BF16)
            la_lo = (la - la_hi.astype(F32)).astype(BF16)
            b = _dot(tri_ref[...], la_hi) + _dot(tri_ref[...], la_lo)
        else:
            rows = lax.broadcasted_iota(jnp.int32, (c_len, 1), 0)
            b = jnp.zeros_like(la)
            for j in range(c_len):
                b = b + jnp.where(rows >= j, la[j:j + 1, :], 0.0)
        b_ref[i] = b
        bs.append(b)

    def robust():
        bexp = bexp_ref[...]
        for i in range(gb):
            def body(c, carry, i=i):
                rs = pl.ds(pl.multiple_of(c * c_len, c_len), c_len)
                outs, st_new = _gla_chunk(q_ref[i, rs, :].astype(F32), k_ref[i, rs, :].astype(F32),
                                          v_ref[i, rs, :].astype(F32), b_ref[i, rs, :], st_ref[i], bexp, c_len, sub)
                st_ref[i] = st_new
                emit(i, outs, rs)
                return carry

            lax.fori_loop(0, cps, body, 0)

    if fast:
        low = jnp.min(bs[0])
        for b in bs[1:]:
            low = jnp.minimum(low, jnp.min(b))
        mild = low >= -GLA_FAST_MAX

        @pl.when(mild)
        def _():
            seqs = range(gb)
            outs, sts = _gla_fast_step([q_ref[i].astype(F32) for i in seqs], [k_ref[i].astype(F32) for i in seqs],
                                       [v_ref[i].astype(F32) for i in seqs], [b_ref[i] for i in seqs],
                                       [st_ref[i] for i in seqs], keep_ref[...], c_len, cps)
            for i in seqs:
                st_ref[i] = sts[i]
                emit(i, outs[i], slice(None))

        pl.when(jnp.logical_not(mild))(robust)
    else:
        robust()

    @pl.when(t == pl.num_programs(1) - 1)
    def _():
        for i in range(gb):
            sf_o[i] = st_ref[i].T


def _gla(gq, gk, gv, gr, la, s0, w, batch, seq, c_len, sub, cps, gb):
    ts = c_len * cps
    nt = seq // ts
    fast = c_len == GLA_CHUNK
    assert fast or cps == 1
    row = lambda width: pl.BlockSpec((gb, None, ts, width), lambda b, t: (b, t, 0, 0))
    g4 = lambda a: a.reshape(batch, nt, ts, a.shape[-1])
    st_spec = pl.BlockSpec((gb, GLA_H * GLA_K, GLA_V), lambda b, t: (b, 0, 0))
    consts = [w['g_gla_o'], w['bexp']]
    if fast:
        ti = jnp.arange(ts)
        keep = (ti[:, None] // c_len == ti[None, :] // c_len) & (ti[None, :] <= ti[:, None])
        consts += [keep.astype(BF16), keep.astype(F32)]
    og, s_fin = pl.pallas_call(
        functools.partial(_gla_kernel, c_len=c_len, sub=sub, cps=cps, fast=fast),
        grid=(batch // gb, nt),
        in_specs=[row(256), row(256), row(512), row(512), row(256), st_spec] + [_full(c.shape) for c in consts],
        out_specs=[row(512), st_spec],
        out_shape=[jax.ShapeDtypeStruct((batch, nt, ts, GLA_H * GLA_V), BF16),
                   jax.ShapeDtypeStruct((batch, GLA_H * GLA_K, GLA_V), F32)],
        scratch_shapes=[pltpu.VMEM((gb, GLA_V, GLA_H * GLA_K), F32), pltpu.VMEM((gb, ts, GLA_H * GLA_K), F32)],
        compiler_params=_cparams(("parallel", "arbitrary")),
        name="gla",
    )(g4(gq), g4(gk), g4(gv), g4(gr), g4(la), s0, *consts)
    return og.reshape(batch * seq, GLA_H * GLA_V), s_fin


def _mla_prompt_kernel(q_ref, k_ref, vt_ref, o_ref, m_ref, acc_ref, *, tq):
    i = pl.program_id(1)
    m_ref[...] = jnp.full(m_ref.shape, -jnp.inf, F32)
    acc_ref[...] = jnp.zeros(acc_ref.shape, F32)
    keep = (lax.broadcasted_iota(jnp.int32, (tq, tq), 0) <= lax.broadcasted_iota(jnp.int32, (tq, tq), 1))

    def tiles(js, last_masked):
        n = len(js)
        for h0 in range(0, MLA_H, ATT_HEADS):
            hs = range(h0, h0 + ATT_HEADS)
            sts = {h: [_dot_nt(k_ref[pl.ds(pl.multiple_of(j * tq, tq), tq), LANES * h:LANES * (h + 1)],
                               q_ref[:, LANES * h:LANES * (h + 1)]) for j in js] for h in hs}
            ps, alphas = {}, {}
            for h in hs:
                if last_masked:
                    sts[h][-1] = jnp.where(keep, sts[h][-1], -jnp.inf)
                m_old = m_ref[h]
                m_new = m_old
                for st in sts[h]:
                    m_new = jnp.maximum(m_new, jnp.max(st, axis=0, keepdims=True))
                alphas[h] = jnp.exp2(m_old - m_new)
                ps[h] = [jnp.exp2(st - m_new).astype(BF16) for st in sts[h]]
                m_ref[h] = m_new
            for h in hs:
                acc = alphas[h] * acc_ref[h]
                for t in range(n):
                    acc = acc + _dot(vt_ref[js[t], ATT_VROWS * h:ATT_VROWS * (h + 1), :], ps[h][t])
                acc_ref[h] = acc

    def body(jj, carry):
        tiles([ATT_GROUP * jj + t for t in range(ATT_GROUP)], False)
        return carry

    lax.fori_loop(0, i // ATT_GROUP, body, 0)
    for rem in range(ATT_GROUP):
        pl.when(i % ATT_GROUP == rem)(functools.partial(tiles, [i - rem + t for t in range(rem + 1)], True))

    for pair in range(MLA_H // 2):
        h0, h1 = 2 * pair, 2 * pair + 1
        ab = jnp.concatenate([acc_ref[h, 0:MLA_DV, :] * (1.0 / acc_ref[h, MLA_DV:MLA_DV + 1, :]) for h in (h0, h1)],
                             axis=0)
        o_ref[:, LANES * pair:LANES * (pair + 1)] = ab.T.astype(BF16)


def _mla_prompt(mq, mk, mvt, batch, seq, tq):
    nq = seq // tq
    return pl.pallas_call(
        functools.partial(_mla_prompt_kernel, tq=tq),
        grid=(batch, nq),
        in_specs=[pl.BlockSpec((tq, MLA_H * LANES), lambda b, i: (b * nq + i, 0)),
                  pl.BlockSpec((seq, MLA_H * LANES), lambda b, i: (b, 0)),
                  pl.BlockSpec((nq, MLA_H * ATT_VROWS, tq), lambda b, i: (b, 0, 0))],
        out_specs=pl.BlockSpec((tq, MLA_H * MLA_DV), lambda b, i: (b * nq + i, 0)),
        out_shape=jax.ShapeDtypeStruct((batch * seq, MLA_H * MLA_DV), BF16),
        scratch_shapes=[pltpu.VMEM((MLA_H, 1, tq), F32), pltpu.VMEM((MLA_H, ATT_VROWS, tq), F32)],
        compiler_params=_cparams(("parallel", "arbitrary")),
        name="mla_prompt",
    )(mq, mk, mvt)


def _mla_sample_kernel(pt_ref, q_ref, wkq_ref, wkt_ref, wv_ref, cn_ref, kn_ref, ckv_hbm, kpe_hbm,
                       o_ref, ckv_buf, kpe_buf, sem, *, n_pages, t_new, kpm):
    b = pl.program_id(0)
    nb = pl.num_programs(0)
    half = b % 2
    nrow = MLA_H * t_new

    def page_copies(seq, hf, p):
        page = pt_ref[seq * n_pages + p]
        return (pltpu.make_async_copy(ckv_hbm.at[page], ckv_buf.at[hf, p], sem.at[0, hf]),
                pltpu.make_async_copy(kpe_hbm.at[page], kpe_buf.at[hf, p], sem.at[1, hf]))

    def start_pages(seq, hf, pages):
        for p in pages:
            for cp in page_copies(seq, hf, p):
                cp.start()

    def wait_pages(hf):
        pltpu.make_async_copy(ckv_hbm.at[pl.ds(0, n_pages)], ckv_buf.at[hf], sem.at[0, hf]).wait()
        pltpu.make_async_copy(kpe_hbm.at[pl.ds(0, n_pages)], kpe_buf.at[hf], sem.at[1, hf]).wait()

    @pl.when(b == 0)
    def _():
        start_pages(0, 0, range(n_pages))

    nxt = (b + 1) % nb

    wait_pages(half)

    qa, qr = [], []
    for h in range(MLA_H):
        qh = q_ref[:, LANES * h:LANES * (h + 1)]
        qa.append(_dot(qh, wkq_ref[h]))
        qr.append(pltpu.roll(qh.astype(F32), LANES - _KPE_LANE, 1)[:, 0:MLA_ROPE])
    lq = jnp.concatenate([jnp.concatenate(qa, axis=0).astype(BF16), wkt_ref[...]], axis=0)
    qp = jnp.concatenate(qr, axis=0).astype(BF16)

    def scores(r, s_rope):
        rinv = []
        for h in range(MLA_H):
            kr = r[nrow + MLA_NOPE * h:nrow + MLA_NOPE * (h + 1)]
            ms = jnp.sum(kr * kr, axis=0, keepdims=True) * (1.0 / MLA_NOPE)
            rinv.append(jnp.broadcast_to(lax.rsqrt(ms + EPS), (t_new, r.shape[1])))
        return r[0:nrow] * jnp.concatenate(rinv, axis=0) + s_rope

    ngrp = n_pages // kpm
    ppb = kpm * SAMPLE_GROUPS
    xs, ss = [], []
    for g0 in range(0, ngrp, SAMPLE_GROUPS):
        gs = range(g0, min(g0 + SAMPLE_GROUPS, ngrp))
        start_pages(nxt, 1 - half, range(g0 * kpm, min(g0 * kpm + ppb, n_pages)))
        xg = [jnp.concatenate([ckv_buf[half, g * kpm + j] for j in range(kpm)], axis=0).astype(BF16) for g in gs]
        rg = [_dot_nt(lq, x) for x in xg]
        pg = [_dot(qp, jnp.concatenate([kpe_buf[half, g * kpm + j] for j in range(kpm)], axis=1).astype(BF16))
              for g in gs]
        xs.append(jnp.concatenate(xg, axis=0))
        ss.append(jnp.concatenate([scores(r, s_rope) for r, s_rope in zip(rg, pg)], axis=1))

    xn = cn_ref[...].astype(BF16)
    sn = scores(_dot_nt(lq, xn), _dot_nt(qp, kn_ref[...].astype(BF16)))
    tq = lax.broadcasted_iota(jnp.int32, (nrow, t_new), 0) % t_new
    ts = lax.broadcasted_iota(jnp.int32, (nrow, t_new), 1)
    xs.append(xn)
    ss.append(jnp.where(ts <= tq, sn, -jnp.inf))

    ms = [jnp.max(s, axis=-1, keepdims=True) for s in ss]
    ps = [jnp.exp2(s - mi) for s, mi in zip(ss, ms)]
    ls = [jnp.sum(p, axis=-1, keepdims=True) for p in ps]
    pvs = [_dot(p.astype(BF16), x) for p, x in zip(ps, xs)]
    m = ms[0]
    for mi in ms[1:]:
        m = jnp.maximum(m, mi)
    ws = [jnp.exp2(mi - m) for mi in ms]
    l = ws[0] * ls[0]
    oa = ws[0] * pvs[0]
    for wi, li, pv in zip(ws[1:], ls[1:], pvs[1:]):
        l = l + wi * li
        oa = oa + wi * pv
    oa = oa / l

    r = _dot(oa.astype(BF16), wv_ref[...])
    lane = lax.broadcasted_iota(jnp.int32, (1, MLA_H * MLA_DV), 1)
    out = jnp.zeros((t_new, MLA_H * MLA_DV), F32)
    for h in range(MLA_H):
        out = out + jnp.where(lane // MLA_DV == h, r[t_new * h:t_new * (h + 1)], 0.0)
    o_ref[...] = out.astype(BF16)

    @pl.when(b == nb - 1)
    def _():
        wait_pages(1 - half)


def _mla_sample(mq, ckv_new, kp_new, pool_ckv, pool_kpe_t, page_table, w, batch, t_new, kpm):
    n_pages = page_table.shape[1]
    pt = page_table.reshape(-1)
    tok = lambda width: pl.BlockSpec((None, t_new, width), lambda b, pt_ref: (b, 0, 0))
    g3 = lambda a: a.reshape(batch, t_new, a.shape[-1])
    const = lambda a: pl.BlockSpec(a.shape, lambda b, pt_ref: (0,) * a.ndim)
    hbm = pl.BlockSpec(memory_space=pl.ANY)
    grid_spec = pltpu.PrefetchScalarGridSpec(
        num_scalar_prefetch=1,
        grid=(batch,),
        in_specs=[tok(MLA_H * LANES), const(w['w_kq']), const(w['w_kt']), const(w['w_v']),
                  tok(MLA_KVR), tok(MLA_ROPE), hbm, hbm],
        out_specs=tok(MLA_H * MLA_DV),
        scratch_shapes=[pltpu.VMEM((2, n_pages, PAGE, MLA_KVR), F32),
                        pltpu.VMEM((2, n_pages, MLA_ROPE, PAGE), F32),
                        pltpu.SemaphoreType.DMA((2, 2))],
    )
    om = pl.pallas_call(
        functools.partial(_mla_sample_kernel, n_pages=n_pages, t_new=t_new, kpm=kpm),
        grid_spec=grid_spec,
        out_shape=jax.ShapeDtypeStruct((batch, t_new, MLA_H * MLA_DV), BF16),
        compiler_params=_cparams(("arbitrary",)),
        name="mla_sample",
    )(pt, g3(mq), w['w_kq'], w['w_kt'], w['w_v'], g3(ckv_new), g3(kp_new), pool_ckv, pool_kpe_t)
    return om.reshape(batch * t_new, MLA_H * MLA_DV)


def _mix_out_kernel(x_ref, og_ref, om_ref, wo_ref, gx_ref, wxq_ref, gxq_ref, x1_o, xq_o):
    x1 = x_ref[...] + _dot(og_ref[...], wo_ref[0:512, :]) + _dot(om_ref[...], wo_ref[512:1024, :])
    x1_o[...] = x1
    qf = _dot(_rms(x1, gx_ref[...]).astype(BF16), wxq_ref[...])
    g = gxq_ref[...]
    for h in range(X_H):
        hs = slice(X_D * h, X_D * (h + 1))
        xq_o[:, hs] = _rms(qf[:, hs], g).astype(BF16)


def _mix_out(x, og, om, w, tm):
    n = x.shape[0]
    row = lambda width: pl.BlockSpec((tm, width), lambda i: (i, 0))
    consts = [w['w_out'], w['g_x'], w['w_xq'], w['g_xq']]
    return pl.pallas_call(
        _mix_out_kernel,
        grid=(n // tm,),
        in_specs=[row(D_MODEL), row(512), row(512)] + [_full(c.shape) for c in consts],
        out_specs=[row(D_MODEL), row(X_H * X_D)],
        out_shape=[jax.ShapeDtypeStruct((n, D_MODEL), F32), jax.ShapeDtypeStruct((n, X_H * X_D), BF16)],
        compiler_params=_cparams(("parallel",)),
        name="mix_out",
    )(x, og, om, *consts)


def _cross_kernel(xq_ref, mk_ref, mv_ref, o_ref, *, n_mem):
    gb = xq_ref.shape[0]

    def head(ref, g, h):
        if len(ref.shape) == 4:
            return ref[g, h]
        return ref[pl.ds(g * n_mem * X_H + h, n_mem, stride=X_H), :].astype(BF16)

    units = [(g, h) for g in range(gb) for h in range(X_H)]
    ss = [_dot_nt(xq_ref[g, :, X_D * h:X_D * (h + 1)], head(mk_ref, g, h)) for g, h in units]
    ps = [jnp.exp2(s - jnp.max(s, axis=-1, keepdims=True)) for s in ss]
    for (g, h), p in zip(units, ps):
        o = _dot(p.astype(BF16), head(mv_ref, g, h))
        o_ref[g, :, X_D * h:X_D * (h + 1)] = (o * (1.0 / jnp.sum(p, axis=-1, keepdims=True))).astype(BF16)


def _cross(xq, mk, mv, batch, seq, n_mem, tq, gb):
    nq = seq // tq
    assert gb == 1 or nq == 1
    row = pl.BlockSpec((gb, tq, X_H * X_D), lambda b, i: (b * nq + i, 0, 0))
    if mk.ndim == 4:
        mem = pl.BlockSpec((gb,) + mk.shape[1:], lambda b, i: (b, 0, 0, 0))
    else:
        mem = pl.BlockSpec((gb * n_mem * X_H, X_D), lambda b, i: (b, 0))
    o = pl.pallas_call(
        functools.partial(_cross_kernel, n_mem=n_mem),
        grid=(batch // gb, nq),
        in_specs=[row, mem, mem],
        out_specs=row,
        out_shape=jax.ShapeDtypeStruct((batch * nq, tq, X_H * X_D), BF16),
        compiler_params=_cparams(("parallel", "arbitrary")),
        name="cross",
    )(xq.reshape(batch * nq, tq, X_H * X_D), mk, mv)
    return o.reshape(batch * seq, X_H * X_D)


def _ffn_kernel(x_ref, o_ref, wxo_ref, g_ref, w1_ref, w2_ref, y_o, *, fc):
    x = x_ref[...] + _dot(o_ref[...], wxo_ref[...])
    h = _rms(x, g_ref[...]).astype(BF16)
    acc = x
    for c in range(D_FF // fc):
        u = jnp.maximum(_dot(h, w1_ref[:, fc * c:fc * (c + 1)]), 0.0)
        acc = acc + _dot((u * u).astype(BF16), w2_ref[fc * c:fc * (c + 1), :])
    y_o[...] = acc


def _ffn(x, o, w, tm, fc):
    n = x.shape[0]
    row = lambda width: pl.BlockSpec((tm, width), lambda i: (i, 0))
    consts = [w['w_xo'], w['g_ff'], w['w_ff1'], w['w_ff2']]
    return pl.pallas_call(
        functools.partial(_ffn_kernel, fc=fc),
        grid=(n // tm,),
        in_specs=[row(D_MODEL), row(X_H * X_D)] + [_full(c.shape) for c in consts],
        out_specs=row(D_MODEL),
        out_shape=jax.ShapeDtypeStruct((n, D_MODEL), F32),
        compiler_params=_cparams(("parallel",)),
        name="ffn",
    )(x, o, *consts)


def _mem_kv_kernel(m_ref, gm_ref, wk_ref, wv_ref, gk_ref, k_o, v_o, kh_o, vh_o, *, n_mem):
    tm = m_ref.shape[0]
    m = _rms(m_ref[...], gm_ref[...]).astype(BF16)
    kf = _dot(m, wk_ref[...])
    vf = _dot(m, wv_ref[...])
    g = gk_ref[...]
    for h in range(X_H):
        hs = slice(X_D * h, X_D * (h + 1))
        kn = _rms(kf[:, hs], g)
        k_o[pl.ds(h, tm, stride=X_H), :] = kn
        v_o[pl.ds(h, tm, stride=X_H), :] = vf[:, hs]
        for b in range(kh_o.shape[0]):
            rs = slice(n_mem * b, n_mem * (b + 1))
            kh_o[b, h] = kn[rs].astype(BF16)
            vh_o[b, h] = vf[rs, hs].astype(BF16)


def _mem_kv(mem, w, n_mem, tm):
    n = mem.shape[0]
    bt = tm // n_mem
    row = lambda width: pl.BlockSpec((tm, width), lambda i: (i, 0))
    head = pl.BlockSpec((bt, X_H, n_mem, X_D), lambda i: (i, 0, 0, 0))
    consts = [w['g_mem'], w['w_xk'], w['w_xv'], w['g_xk']]
    return pl.pallas_call(
        functools.partial(_mem_kv_kernel, n_mem=n_mem),
        grid=(n // tm,),
        in_specs=[row(D_MODEL)] + [_full(c.shape) for c in consts],
        out_specs=[pl.BlockSpec((tm * X_H, X_D), lambda i: (i, 0))] * 2 + [head, head],
        out_shape=[jax.ShapeDtypeStruct((n * X_H, X_D), F32)] * 2
                  + [jax.ShapeDtypeStruct((n // n_mem, X_H, n_mem, X_D), BF16)] * 2,
        compiler_params=_cparams(("parallel",)),
        name="mem_kv",
    )(mem, *consts)


def _prep_weights(g_mix, w_in, w_gla_a2, b_gla_a, g_gla_o, g_mla_qa, w_mla_qb, g_mla_kva, w_mla_kvb,
                  g_q_nope, g_k_nope, g_q_rope, g_k_rope, w_out, g_x, g_mem, w_xq, w_xk, w_xv, g_xq, g_xk,
                  w_xo, g_ff, w_ff1, w_ff2):
    rowv = lambda g: g.reshape(1, -1).astype(F32)
    zc = lambda n: jnp.zeros((D_MODEL, n), F32)
    sizes = (256, 256, 512, 512, GLA_RANK, MLA_QR, MLA_KVR, MLA_ROPE)
    offs = [0]
    for s in sizes:
        offs.append(offs[-1] + s)
    q, k, v, r, a, qa, kva, kpe = [w_in[:, offs[i]:offs[i + 1]] for i in range(8)]
    w_in_p = jnp.concatenate(
        [q * (GLA_K ** -0.5), k, v, r, qa,
         a, zc(_KPE_LANE - GLA_RANK), kpe, zc(LANES - _KPE_LANE - MLA_ROPE), kva], axis=1)
    w_a2 = jnp.concatenate([w_gla_a2, jnp.zeros((LANES - GLA_RANK, GLA_H * GLA_K), F32)], axis=0)

    qb = w_mla_qb.reshape(MLA_QR, MLA_H, MLA_NOPE + MLA_ROPE)
    qb = jnp.pad(qb, ((0, 0), (0, 0), (0, LANES - MLA_NOPE - MLA_ROPE))).reshape(MLA_QR, MLA_H * LANES)
    kvb = w_mla_kvb.reshape(MLA_KVR, MLA_H, MLA_NOPE + MLA_DV)
    wk = kvb[:, :, :MLA_NOPE]
    wk_p = jnp.pad(wk, ((0, 0), (0, 0), (0, LANES - MLA_NOPE))).reshape(MLA_KVR, MLA_H * LANES)
    wv = kvb[:, :, MLA_NOPE:].reshape(MLA_KVR, MLA_H * MLA_DV)
    z32 = jnp.zeros((LANES - MLA_NOPE - MLA_ROPE,), F32)
    gq_row = jnp.tile(jnp.concatenate([g_q_nope, g_q_rope, z32]), MLA_H) * (MLA_SCALE * LOG2E)
    gk_row = jnp.tile(jnp.concatenate([g_k_nope, jnp.zeros((LANES - MLA_NOPE,), F32)]), MLA_H)
    g_kpe = jnp.concatenate([jnp.zeros((_KPE_LANE,), F32), g_k_rope, z32])

    li = jnp.arange(2 * LANES)
    seg = jnp.where(li % LANES < MLA_NOPE, 0, jnp.where(li % LANES < MLA_NOPE + MLA_ROPE, 1, 2))
    same = (li[:, None] // LANES == li[None, :] // LANES) & (seg[:, None] == seg[None, :]) & (seg[:, None] < 2)
    bd = jnp.where(same, jnp.where(seg[:, None] == 0, 1.0 / MLA_NOPE, 1.0 / MLA_ROPE), 0.0)

    wkq = jnp.transpose(wk, (1, 2, 0)) * g_k_nope[None, :, None]
    wkq = jnp.pad(wkq, ((0, 0), (0, LANES - MLA_NOPE), (0, 0)))
    wkt = jnp.transpose(wk, (1, 2, 0)).reshape(MLA_H * MLA_NOPE, MLA_KVR)

    hk = jnp.arange(GLA_H * GLA_K) // GLA_K
    hv = jnp.arange(GLA_H * GLA_V) // GLA_V
    bexp = (hk[:, None] == hv[None, :])
    return {
        'g_mix': rowv(g_mix), 'w_in': w_in_p.astype(BF16), 'w_a2': w_a2.astype(BF16), 'b_a': rowv(b_gla_a),
        'g_qa': rowv(g_mla_qa), 'w_qb': qb.astype(BF16), 'gq_row': rowv(gq_row), 'g_kva': rowv(g_mla_kva),
        'w_kv': jnp.concatenate([wk_p, wv], axis=1).astype(BF16), 'gk_row': rowv(gk_row),
        'w_v': wv.astype(BF16), 'g_kpe': rowv(g_kpe),
        'bd': bd.astype(BF16), 'w_kq': wkq.astype(BF16), 'w_kt': wkt.astype(BF16),
        'g_gla_o': rowv(g_gla_o), 'bexp': bexp.astype(BF16),
        'w_out': w_out.astype(BF16), 'g_x': rowv(g_x), 'w_xq': w_xq.astype(BF16),
        'g_xq': rowv(g_xq) * (X_SCALE * LOG2E),
        'g_mem': rowv(g_mem), 'w_xk': w_xk.astype(BF16), 'w_xv': w_xv.astype(BF16), 'g_xk': rowv(g_xk),
        'w_xo': w_xo.astype(BF16), 'g_ff': rowv(g_ff), 'w_ff1': w_ff1.astype(BF16), 'w_ff2': w_ff2.astype(BF16),
    }


def _rope_tables(pos):
    half = MLA_ROPE // 2
    inv = ROPE_THETA ** (-jnp.arange(half, dtype=F32) / half)
    ang = pos.astype(F32)[:, None] * inv[None, :]
    cos, sin = jnp.cos(ang), jnp.sin(ang)
    n = pos.shape[0]
    one = jnp.ones((n, _KPE_LANE), F32)
    z = lambda w_: jnp.zeros((n, w_), F32)
    tail = LANES - _KPE_LANE - MLA_ROPE
    c = jnp.concatenate([one, cos, cos, jnp.ones((n, tail), F32)], axis=1)
    s_up = jnp.concatenate([z(_KPE_LANE), -sin, z(half), z(tail)], axis=1)
    s_dn = jnp.concatenate([z(_KPE_LANE), z(half), sin, z(tail)], axis=1)
    return c, s_up, s_dn


def _tile_rows(n, cap):
    t = min(n, cap)
    while n % t:
        t //= 2
    return t


def _tail(x1, xq, mk, mv, w, batch, seq, n_mem):
    tq = _tile_rows(seq, 512)
    gb = _tile_rows(batch, max(1, CROSS_ROWS // seq)) if tq == seq else 1
    o = _cross(xq, mk, mv, batch, seq, n_mem, tq, gb)
    return _ffn(x1, o, w, _tile_rows(x1.shape[0], 512), 1024)


def _prompt_layer(x, mem, w):
    batch, seq, _ = x.shape
    n = batch * seq
    xf = x.reshape(n, D_MODEL)
    tm = _tile_rows(seq, 512)
    tabs = _rope_tables(jnp.arange(seq))
    gq, gk, gv, gr, la, mq, ckv, kp, mk_, mvt = _in_proj(xf, w, tabs, tm, True)
    c_len = GLA_CHUNK
    cps = _tile_rows(seq // c_len, GLA_CHUNKS_PER_STEP)
    s0 = jnp.zeros((batch, GLA_H * GLA_K, GLA_V), F32)
    og, s_fin = _gla(gq, gk, gv, gr, la, s0, w, batch, seq, c_len, min(GLA_SUB, c_len), cps,
                     _tile_rows(batch, GLA_SEQS_PER_STEP))
    om = _mla_prompt(mq, mk_, mvt, batch, seq, ATT_TILE)
    x1, xq = _mix_out(xf, og, om, w, tm)
    n_mem = mem.shape[1]
    memf = mem.reshape(-1, D_MODEL)
    xk, xv, xkh, xvh = _mem_kv(memf, w, n_mem, max(n_mem, _tile_rows(memf.shape[0], 512)))
    y = _tail(x1, xq, xkh, xvh, w, batch, seq, n_mem)
    return y, ckv, kp, xk, xv, s_fin


def _sample_layer(x, pool_ckv, pool_kpe, page_table, mem_k, mem_v, s_prev, w):
    batch, seq, _ = x.shape
    n = batch * seq
    xf = x.reshape(n, D_MODEL)
    tm = _tile_rows(n, 512)
    past = page_table.shape[1] * PAGE
    pos = past + (jnp.arange(tm) % seq)
    gq, gk, gv, gr, la, mq, ckv, kp = _in_proj(xf, w, _rope_tables(pos), tm, False)
    s0 = s_prev.reshape(batch, GLA_H * GLA_K, GLA_V)
    og, s_fin = _gla(gq, gk, gv, gr, la, s0, w, batch, seq, seq, seq, 1, _tile_rows(batch, GLA_SHORT_SEQS_PER_STEP))
    om = _mla_sample(mq, ckv, kp, pool_ckv, jnp.swapaxes(pool_kpe, 1, 2), page_table, w, batch, seq,
                     _tile_rows(page_table.shape[1], SAMPLE_PAGES_PER_DOT))
    x1, xq = _mix_out(xf, og, om, w, tm)
    y = _tail(x1, xq, mem_k.reshape(-1, X_D), mem_v.reshape(-1, X_D), w, batch, seq, mem_k.shape[1])
    return y, ckv, kp, s_fin


def kernel(x_prompt, x_sample, mem_prompt, cache_ckv, cache_kpe, cache_mem_k, cache_mem_v, state_gla, page_table, g_mix, w_in, w_gla_a2, b_gla_a, g_gla_o, g_mla_qa, w_mla_qb, g_mla_kva, w_mla_kvb, g_q_nope, g_k_nope, g_q_rope, g_k_rope, w_out, g_x, g_mem, w_xq, w_xk, w_xv, g_xq, g_xk, w_xo, g_ff, w_ff1, w_ff2):
    depth = w_in.shape[0]
    assert depth == 1, "one layer: prompt-group caches of layer l would feed layer l+1 otherwise unchanged"
    params = (g_mix, w_in, w_gla_a2, b_gla_a, g_gla_o, g_mla_qa, w_mla_qb, g_mla_kva, w_mla_kvb,
              g_q_nope, g_k_nope, g_q_rope, g_k_rope, w_out, g_x, g_mem, w_xq, w_xk, w_xv, g_xq, g_xk,
              w_xo, g_ff, w_ff1, w_ff2)
    w = _prep_weights(*[p[0] for p in params])
    bp, tp, _ = x_prompt.shape
    bs, tsq, _ = x_sample.shape
    yp, ckv_p, kp_p, xk, xv, gla_p = _prompt_layer(x_prompt, mem_prompt, w)
    ys, ckv_s, kp_s, gla_s = _sample_layer(x_sample, cache_ckv.reshape(cache_ckv.shape[1:]), cache_kpe.reshape(cache_kpe.shape[1:]), page_table,
                                            cache_mem_k[0], cache_mem_v[0], state_gla[0], w)
    n_mem = mem_prompt.shape[1]
    return (yp.reshape(bp, tp, D_MODEL), ys.reshape(bs, tsq, D_MODEL),
            ckv_p.reshape(1, bp, tp, MLA_KVR), kp_p.reshape(1, bp, tp, MLA_ROPE),
            xk.reshape(1, bp, n_mem, X_H, X_D), xv.reshape(1, bp, n_mem, X_H, X_D),
            gla_p.reshape(1, bp, GLA_H, GLA_K, GLA_V),
            ckv_s.reshape(1, bs, tsq, MLA_KVR), kp_s.reshape(1, bs, tsq, MLA_ROPE),
            gla_s.reshape(1, bs, GLA_H, GLA_K, GLA_V))
```

```python
import functools

import jax
import jax.numpy as jnp
from jax import lax
from jax.experimental import pallas as pl
from jax.experimental.pallas import tpu as pltpu

F32 = jnp.float32
BF16 = jnp.bfloat16

EPS = 1e-6
D_MODEL = 1024
GLA_H, GLA_K, GLA_V, GLA_RANK, GLA_TAU, GLA_CHUNK = 4, 64, 128, 16, 16.0, 64
GLA_SUB = 16
GLA_CHUNKS_PER_STEP = 4
GLA_SEQS_PER_STEP = 8
GLA_SHORT_SEQS_PER_STEP = 16
GLA_FAST_MAX = 40.0
MLA_H, MLA_DV, MLA_NOPE, MLA_ROPE, MLA_QR, MLA_KVR = 8, 64, 64, 32, 384, 256
MLA_SCALE = (MLA_NOPE + MLA_ROPE) ** -0.5
ROPE_THETA = 10000.0
X_H, X_D = 4, 128
X_SCALE = X_D ** -0.5
D_FF = 4096
PAGE = 128
LANES = 128
NEG = -1e30
LOG2E = 1.4426950408889634
ATT_TILE = 256
ATT_VROWS = 80
ATT_HEADS = 8
ATT_GROUP = 2
CROSS_ROWS = 64
SAMPLE_PAGES_PER_DOT = 4
SAMPLE_GROUPS = 4

_C_Q, _C_K, _C_V, _C_R = 0, 256, 512, 1024
_C_QA, _C_AK, _C_KVA, _C_END = 1536, 1920, 2048, 2304
_KPE_LANE = 64

VMEM_LIMIT = 56 * 1024 * 1024


def _cparams(sem):
    return pltpu.CompilerParams(dimension_semantics=sem, vmem_limit_bytes=VMEM_LIMIT)


def _dot(a, b):
    return jnp.dot(a, b, preferred_element_type=F32)


def _dot_nt(a, b):
    return lax.dot_general(a, b, (((1,), (1,)), ((), ())), preferred_element_type=F32)


def _rms(x, g):
    return x * lax.rsqrt(jnp.mean(x * x, axis=-1, keepdims=True) + EPS) * g


def _full(shape):
    n = len(shape)
    return pl.BlockSpec(shape, lambda *_: (0,) * n)


def _in_proj_kernel(x_ref, gmix_ref, win_ref, wa2_ref, ba_ref, gqa_ref, wqb_ref, gqrow_ref,
                    gkva_ref, wkv_ref, gkrow_ref, gkpe_ref, bd_ref, c_ref, s1_ref, s2_ref,
                    gq_o, gk_o, gv_o, gr_o, la_o, mq_o, ckv_o, kp_o, *kv_o):
    h = _rms(x_ref[...], gmix_ref[...]).astype(BF16)

    z2 = _dot(h, win_ref[:, _C_QA:_C_END])
    qa = z2[:, 0:_C_AK - _C_QA]
    zak = z2[:, _C_AK - _C_QA:_C_KVA - _C_QA]
    kva = z2[:, _C_KVA - _C_QA:]

    gate = _dot(zak.astype(BF16), wa2_ref[...]) + ba_ref[...]
    la_o[...] = (jnp.minimum(gate, 0.0) - jnp.log1p(jnp.exp(-jnp.abs(gate)))) * (1.0 / GLA_TAU)

    cos, s_up, s_dn = c_ref[...], s1_ref[...], s2_ref[...]

    def rope(t):
        return t * cos + pltpu.roll(t, LANES - 16, 1) * s_up + pltpu.roll(t, 16, 1) * s_dn

    lane = lax.broadcasted_iota(jnp.int32, (1, LANES), 1)
    zk = jnp.where((lane >= _KPE_LANE) & (lane < _KPE_LANE + MLA_ROPE), zak, 0.0)
    kpn = zk * lax.rsqrt(jnp.sum(zk * zk, axis=-1, keepdims=True) * (1.0 / MLA_ROPE) + EPS) * gkpe_ref[...]
    kpt = rope(kpn)
    if kv_o:
        kp_o[...] = kpt.T[_KPE_LANE:_KPE_LANE + MLA_ROPE, :]
    else:
        kp_o[...] = pltpu.roll(kpt, LANES - _KPE_LANE, 1)[:, 0:MLA_ROPE]

    ckv = _rms(kva, gkva_ref[...])
    ckv_o[...] = ckv
    cb = ckv.astype(BF16)
    bd = bd_ref[...]

    def seg_norm(t, grow):
        ms = _dot((t * t).astype(BF16), bd)
        return t * lax.rsqrt(ms + EPS) * grow

    qf = _dot(_rms(qa, gqa_ref[...]).astype(BF16), wqb_ref[...])
    for c in range(4):
        sl = slice(2 * LANES * c, 2 * LANES * (c + 1))
        qn = seg_norm(qf[:, sl], gqrow_ref[:, sl])
        for j in range(2):
            o = 2 * LANES * c + LANES * j
            mq_o[:, o:o + LANES] = rope(qn[:, LANES * j:LANES * (j + 1)]).astype(BF16)

    if kv_o:
        mk_o, mvt_o = kv_o
        kvf = _dot(cb, wkv_ref[...])
        kf, mv = kvf[:, 0:MLA_H * LANES], kvf[:, MLA_H * LANES:]
        ones = jnp.ones((ATT_VROWS - MLA_DV, ATT_TILE), BF16)
        for c in range(mvt_o.shape[0]):
            mvt = mv[ATT_TILE * c:ATT_TILE * (c + 1), :].T.astype(BF16)
            for hh in range(MLA_H):
                mvt_o[c, ATT_VROWS * hh:ATT_VROWS * hh + MLA_DV, :] = mvt[MLA_DV * hh:MLA_DV * (hh + 1)]
                mvt_o[c, ATT_VROWS * hh + MLA_DV:ATT_VROWS * (hh + 1), :] = ones
        for c in range(4):
            sl = slice(2 * LANES * c, 2 * LANES * (c + 1))
            kn = seg_norm(kf[:, sl], gkrow_ref[:, sl])
            for j in range(2):
                o = 2 * LANES * c + LANES * j
                mk_o[:, o:o + LANES] = (kn[:, LANES * j:LANES * (j + 1)] + kpt).astype(BF16)

    z1 = _dot(h, win_ref[:, _C_Q:_C_QA])
    gq_o[...] = z1[:, _C_Q:_C_K].astype(BF16)
    gk_o[...] = z1[:, _C_K:_C_V].astype(BF16)
    gv_o[...] = z1[:, _C_V:_C_R].astype(BF16)
    gr_o[...] = z1[:, _C_R:_C_QA].astype(BF16)


def _in_proj(x, w, tabs, tm, with_kv):
    n = x.shape[0]
    nt = tabs[0].shape[0] // tm
    row = lambda width: pl.BlockSpec((tm, width), lambda i: (i, 0))
    tab = pl.BlockSpec((tm, LANES), lambda i: (i % nt, 0))
    consts = [w['g_mix'], w['w_in'], w['w_a2'], w['b_a'], w['g_qa'], w['w_qb'], w['gq_row'],
              w['g_kva'], w['w_kv'], w['gk_row'], w['g_kpe'], w['bd']]
    out_w = [(256, BF16), (256, BF16), (512, BF16), (512, BF16), (256, F32),
             (1024, BF16), (256, F32), (MLA_ROPE, F32)] + ([(1024, BF16)] if with_kv else [])
    out_specs = [row(wd) for wd, _ in out_w]
    out_shape = [jax.ShapeDtypeStruct((n, wd), dt) for wd, dt in out_w]
    if with_kv:
        out_specs[7] = pl.BlockSpec((None, MLA_ROPE, tm), lambda i: (i // nt, 0, i % nt))
        out_shape[7] = jax.ShapeDtypeStruct((n // (nt * tm), MLA_ROPE, nt * tm), F32)
        out_specs.append(pl.BlockSpec((tm // ATT_TILE, MLA_H * ATT_VROWS, ATT_TILE), lambda i: (i, 0, 0)))
        out_shape.append(jax.ShapeDtypeStruct((n // ATT_TILE, MLA_H * ATT_VROWS, ATT_TILE), BF16))
    return pl.pallas_call(
        _in_proj_kernel,
        grid=(n // tm,),
        in_specs=[row(D_MODEL)] + [_full(c.shape) for c in consts] + [tab, tab, tab],
        out_specs=out_specs,
        out_shape=out_shape,
        compiler_params=_cparams(("parallel",)),
        name="in_proj",
    )(x, *consts, *tabs)


def _head_masks():
    lane = lax.broadcasted_iota(jnp.int32, (1, GLA_H * GLA_K), 1)
    return [(lane // GLA_K == h).astype(F32) for h in range(GLA_H)]


def _gla_fast_step(qs, ks, vs, bs, sts, keep, c_len, cps):
    masks = _head_masks()
    ts = c_len * cps
    n = len(qs)
    ebs = [jnp.exp(b) for b in bs]
    kts = [k * jnp.exp(-b) for k, b in zip(ks, bs)]
    lqs = [jnp.concatenate([q * eb * m for m in masks], axis=0).astype(BF16)
           for q, eb in zip(qs, ebs)]
    atts = [_dot_nt(lq, kt.astype(BF16)) for lq, kt in zip(lqs, kts)]
    vbs = [v.astype(BF16) for v in vs]
    o_intra = [[_dot((atts[g][ts * h:ts * (h + 1)] * keep).astype(BF16), vbs[g][:, GLA_V * h:GLA_V * (h + 1)])
                for h in range(GLA_H)] for g in range(n)]
    o_inter = [[] for _ in range(n)]
    sts = list(sts)
    for c in range(cps):
        cs = slice(c_len * c, c_len * (c + 1))
        for g in range(n):
            lqc = jnp.concatenate([lqs[g][ts * h + c_len * c:ts * h + c_len * (c + 1)] for h in range(GLA_H)],
                                  axis=0)
            o_inter[g].append(_dot_nt(lqc, sts[g].astype(BF16)))
            ebl = ebs[g][c_len * (c + 1) - 1:c_len * (c + 1), :]
            kl = kts[g][cs] * ebl
            vst = jnp.concatenate([vs[g][cs, GLA_V * h:GLA_V * (h + 1)] for h in range(GLA_H)], axis=0)
            kst = jnp.concatenate([kl * m for m in masks], axis=0).astype(BF16)
            sts[g] = sts[g] * ebl + _dot(vst.T.astype(BF16), kst)
    outs = [[o_intra[g][h] + jnp.concatenate([o_inter[g][c][c_len * h:c_len * (h + 1)] for c in range(cps)],
                                             axis=0) for h in range(GLA_H)] for g in range(n)]
    return outs, sts


def _gla_chunk(q, k, v, b, st, bexp, c_len, sub):
    masks = _head_masks()
    bl = b[c_len - 1:c_len, :]

    qh = q * jnp.exp(b)
    lq = jnp.concatenate([qh * m for m in masks], axis=0).astype(BF16)
    o_inter = _dot_nt(lq, st.astype(BF16))

    nsub = c_len // sub
    a_rows = [[] for _ in range(GLA_H)]
    col = lax.broadcasted_iota(jnp.int32, (1, c_len), 1)
    for i in range(1, nsub):
        ref = b[sub * i - 1:sub * i, :]
        qi = q[sub * i:sub * (i + 1)] * jnp.exp(b[sub * i:sub * (i + 1)] - ref)
        ki = k * jnp.exp(jnp.minimum(ref - b, 0.0))
        li = jnp.concatenate([qi * m for m in masks], axis=0).astype(BF16)
        ai = jnp.where(col < sub * i, _dot_nt(li, ki.astype(BF16)), 0.0)
        for h in range(GLA_H):
            a_rows[h].append(ai[sub * h:sub * (h + 1)])

    row = lax.broadcasted_iota(jnp.int32, (sub, 1), 0)
    o_diag = []
    for i in range(nsub):
        sl = slice(sub * i, sub * (i + 1))
        bb, qb, kb, vb = b[sl], q[sl], k[sl], v[sl]
        ps = []
        for s in range(sub):
            e = jnp.exp(jnp.where(row >= s, bb - bb[s:s + 1, :], NEG))
            ps.append(qb * e * kb[s:s + 1, :])
        r = _dot(jnp.concatenate(ps, axis=0).astype(BF16), bexp)
        od = r[0:sub] * vb[0:1, :]
        for s in range(1, sub):
            od = od + r[sub * s:sub * (s + 1)] * vb[s:s + 1, :]
        o_diag.append(od)
    o_diag = jnp.concatenate(o_diag, axis=0) if nsub > 1 else o_diag[0]

    outs = []
    for h in range(GLA_H):
        oh = o_inter[c_len * h:c_len * (h + 1)] + o_diag[:, GLA_V * h:GLA_V * (h + 1)]
        if nsub > 1:
            ah = jnp.concatenate([jnp.zeros((sub, c_len), F32)] + a_rows[h], axis=0)
            oh = oh + _dot(ah.astype(BF16), v[:, GLA_V * h:GLA_V * (h + 1)].astype(BF16))
        outs.append(oh)

    kl = k * jnp.exp(bl - b)
    vs = jnp.concatenate([v[:, GLA_V * h:GLA_V * (h + 1)] for h in range(GLA_H)], axis=0)
    ks = jnp.concatenate([kl * m for m in masks], axis=0).astype(BF16)
    st_new = st * jnp.exp(bl) + _dot(vs.T.astype(BF16), ks)
    return outs, st_new


def _gla_kernel(*refs, c_len, sub, cps, fast):
    if fast:
        (q_ref, k_ref, v_ref, r_ref, la_ref, s0_ref, g_ref, bexp_ref, tri_ref, keep_ref,
         og_o, sf_o, st_ref, b_ref) = refs
    else:
        q_ref, k_ref, v_ref, r_ref, la_ref, s0_ref, g_ref, bexp_ref, og_o, sf_o, st_ref, b_ref = refs
    t = pl.program_id(1)
    gb = q_ref.shape[0]

    @pl.when(t == 0)
    def _():
        for i in range(gb):
            st_ref[i] = s0_ref[i].T

    g = g_ref[...]

    def emit(i, outs, rs):
        rr = r_ref[i, rs, :].astype(F32)
        for h in range(GLA_H):
            hs = slice(GLA_V * h, GLA_V * (h + 1))
            rh = rr[:, hs]
            og_o[i, rs, hs] = (_rms(outs[h], g) * (rh / (1.0 + jnp.exp(-rh)))).astype(BF16)

    bs = []
    for i in range(gb):
        la = la_ref[i]
        if fast:
            la_hi = la.astype(BF16)
            la_lo = (la - la_hi.astype(F32)).astype(BF16)
            b = _dot(tri_ref[...], la_hi) + _dot(tri_ref[...], la_lo)
        else:
            rows = lax.broadcasted_iota(jnp.int32, (c_len, 1), 0)
            b = jnp.zeros_like(la)
            for j in range(c_len):
                b = b + jnp.where(rows >= j, la[j:j + 1, :], 0.0)
        b_ref[i] = b
        bs.append(b)

    def robust():
        bexp = bexp_ref[...]
        for i in range(gb):
            def body(c, carry, i=i):
                rs = pl.ds(pl.multiple_of(c * c_len, c_len), c_len)
                outs, st_new = _gla_chunk(q_ref[i, rs, :].astype(F32), k_ref[i, rs, :].astype(F32),
                                          v_ref[i, rs, :].astype(F32), b_ref[i, rs, :], st_ref[i], bexp, c_len, sub)
                st_ref[i] = st_new
                emit(i, outs, rs)
                return carry

            lax.fori_loop(0, cps, body, 0)

    if fast:
        low = jnp.min(bs[0])
        for b in bs[1:]:
            low = jnp.minimum(low, jnp.min(b))
        mild = low >= -GLA_FAST_MAX

        @pl.when(mild)
        def _():
            seqs = range(gb)
            outs, sts = _gla_fast_step([q_ref[i].astype(F32) for i in seqs], [k_ref[i].astype(F32) for i in seqs],
                                       [v_ref[i].astype(F32) for i in seqs], [b_ref[i] for i in seqs],
                                       [st_ref[i] for i in seqs], keep_ref[...], c_len, cps)
            for i in seqs:
                st_ref[i] = sts[i]
                emit(i, outs[i], slice(None))

        pl.when(jnp.logical_not(mild))(robust)
    else:
        robust()

    @pl.when(t == pl.num_programs(1) - 1)
    def _():
        for i in range(gb):
            sf_o[i] = st_ref[i].T


def _gla(gq, gk, gv, gr, la, s0, w, batch, seq, c_len, sub, cps, gb):
    ts = c_len * cps
    nt = seq // ts
    fast = c_len == GLA_CHUNK
    assert fast or cps == 1
    row = lambda width: pl.BlockSpec((gb, None, ts, width), lambda b, t: (b, t, 0, 0))
    g4 = lambda a: a.reshape(batch, nt, ts, a.shape[-1])
    st_spec = pl.BlockSpec((gb, GLA_H * GLA_K, GLA_V), lambda b, t: (b, 0, 0))
    consts = [w['g_gla_o'], w['bexp']]
    if fast:
        ti = jnp.arange(ts)
        keep = (ti[:, None] // c_len == ti[None, :] // c_len) & (ti[None, :] <= ti[:, None])
        consts += [keep.astype(BF16), keep.astype(F32)]
    og, s_fin = pl.pallas_call(
        functools.partial(_gla_kernel, c_len=c_len, sub=sub, cps=cps, fast=fast),
        grid=(batch // gb, nt),
        in_specs=[row(256), row(256), row(512), row(512), row(256), st_spec] + [_full(c.shape) for c in consts],
        out_specs=[row(512), st_spec],
        out_shape=[jax.ShapeDtypeStruct((batch, nt, ts, GLA_H * GLA_V), BF16),
                   jax.ShapeDtypeStruct((batch, GLA_H * GLA_K, GLA_V), F32)],
        scratch_shapes=[pltpu.VMEM((gb, GLA_V, GLA_H * GLA_K), F32), pltpu.VMEM((gb, ts, GLA_H * GLA_K), F32)],
        compiler_params=_cparams(("parallel", "arbitrary")),
        name="gla",
    )(g4(gq), g4(gk), g4(gv), g4(gr), g4(la), s0, *consts)
    return og.reshape(batch * seq, GLA_H * GLA_V), s_fin


def _mla_prompt_kernel(q_ref, k_ref, vt_ref, o_ref, m_ref, acc_ref, *, tq):
    i = pl.program_id(1)
    m_ref[...] = jnp.full(m_ref.shape, -jnp.inf, F32)
    acc_ref[...] = jnp.zeros(acc_ref.shape, F32)
    keep = (lax.broadcasted_iota(jnp.int32, (tq, tq), 0) <= lax.broadcasted_iota(jnp.int32, (tq, tq), 1))

    def tiles(js, last_masked):
        n = len(js)
        for h0 in range(0, MLA_H, ATT_HEADS):
            hs = range(h0, h0 + ATT_HEADS)
            sts = {h: [_dot_nt(k_ref[pl.ds(pl.multiple_of(j * tq, tq), tq), LANES * h:LANES * (h + 1)],
                               q_ref[:, LANES * h:LANES * (h + 1)]) for j in js] for h in hs}
            ps, alphas = {}, {}
            for h in hs:
                if last_masked:
                    sts[h][-1] = jnp.where(keep, sts[h][-1], -jnp.inf)
                m_old = m_ref[h]
                m_new = m_old
                for st in sts[h]:
                    m_new = jnp.maximum(m_new, jnp.max(st, axis=0, keepdims=True))
                alphas[h] = jnp.exp2(m_old - m_new)
                ps[h] = [jnp.exp2(st - m_new).astype(BF16) for st in sts[h]]
                m_ref[h] = m_new
            for h in hs:
                acc = alphas[h] * acc_ref[h]
                for t in range(n):
                    acc = acc + _dot(vt_ref[js[t], ATT_VROWS * h:ATT_VROWS * (h + 1), :], ps[h][t])
                acc_ref[h] = acc

    def body(jj, carry):
        tiles([ATT_GROUP * jj + t for t in range(ATT_GROUP)], False)
        return carry

    lax.fori_loop(0, i // ATT_GROUP, body, 0)
    for rem in range(ATT_GROUP):
        pl.when(i % ATT_GROUP == rem)(functools.partial(tiles, [i - rem + t for t in range(rem + 1)], True))

    for pair in range(MLA_H // 2):
        h0, h1 = 2 * pair, 2 * pair + 1
        ab = jnp.concatenate([acc_ref[h, 0:MLA_DV, :] * (1.0 / acc_ref[h, MLA_DV:MLA_DV + 1, :]) for h in (h0, h1)],
                             axis=0)
        o_ref[:, LANES * pair:LANES * (pair + 1)] = ab.T.astype(BF16)


def _mla_prompt(mq, mk, mvt, batch, seq, tq):
    nq = seq // tq
    return pl.pallas_call(
        functools.partial(_mla_prompt_kernel, tq=tq),
        grid=(batch, nq),
        in_specs=[pl.BlockSpec((tq, MLA_H * LANES), lambda b, i: (b * nq + i, 0)),
                  pl.BlockSpec((seq, MLA_H * LANES), lambda b, i: (b, 0)),
                  pl.BlockSpec((nq, MLA_H * ATT_VROWS, tq), lambda b, i: (b, 0, 0))],
        out_specs=pl.BlockSpec((tq, MLA_H * MLA_DV), lambda b, i: (b * nq + i, 0)),
        out_shape=jax.ShapeDtypeStruct((batch * seq, MLA_H * MLA_DV), BF16),
        scratch_shapes=[pltpu.VMEM((MLA_H, 1, tq), F32), pltpu.VMEM((MLA_H, ATT_VROWS, tq), F32)],
        compiler_params=_cparams(("parallel", "arbitrary")),
        name="mla_prompt",
    )(mq, mk, mvt)


def _mla_sample_kernel(pt_ref, q_ref, wkq_ref, wkt_ref, wv_ref, cn_ref, kn_ref, ckv_hbm, kpe_hbm,
                       o_ref, ckv_buf, kpe_buf, sem, *, n_pages, t_new, kpm):
    b = pl.program_id(0)
    nb = pl.num_programs(0)
    half = b % 2
    nrow = MLA_H * t_new

    def page_copies(seq, hf, p):
        page = pt_ref[seq * n_pages + p]
        return (pltpu.make_async_copy(ckv_hbm.at[page], ckv_buf.at[hf, p], sem.at[0, hf]),
                pltpu.make_async_copy(kpe_hbm.at[page], kpe_buf.at[hf, p], sem.at[1, hf]))

    def start_pages(seq, hf, pages):
        for p in pages:
            for cp in page_copies(seq, hf, p):
                cp.start()

    def wait_pages(hf):
        pltpu.make_async_copy(ckv_hbm.at[pl.ds(0, n_pages)], ckv_buf.at[hf], sem.at[0, hf]).wait()
        pltpu.make_async_copy(kpe_hbm.at[pl.ds(0, n_pages)], kpe_buf.at[hf], sem.at[1, hf]).wait()

    @pl.when(b == 0)
    def _():
        start_pages(0, 0, range(n_pages))

    nxt = (b + 1) % nb

    wait_pages(half)

    qa, qr = [], []
    for h in range(MLA_H):
        qh = q_ref[:, LANES * h:LANES * (h + 1)]
        qa.append(_dot(qh, wkq_ref[h]))
        qr.append(pltpu.roll(qh.astype(F32), LANES - _KPE_LANE, 1)[:, 0:MLA_ROPE])
    lq = jnp.concatenate([jnp.concatenate(qa, axis=0).astype(BF16), wkt_ref[...]], axis=0)
    qp = jnp.concatenate(qr, axis=0).astype(BF16)

    def scores(r, s_rope):
        rinv = []
        for h in range(MLA_H):
            kr = r[nrow + MLA_NOPE * h:nrow + MLA_NOPE * (h + 1)]
            ms = jnp.sum(kr * kr, axis=0, keepdims=True) * (1.0 / MLA_NOPE)
            rinv.append(jnp.broadcast_to(lax.rsqrt(ms + EPS), (t_new, r.shape[1])))
        return r[0:nrow] * jnp.concatenate(rinv, axis=0) + s_rope

    ngrp = n_pages // kpm
    ppb = kpm * SAMPLE_GROUPS
    xs, ss = [], []
    for g0 in range(0, ngrp, SAMPLE_GROUPS):
        gs = range(g0, min(g0 + SAMPLE_GROUPS, ngrp))
        start_pages(nxt, 1 - half, range(g0 * kpm, min(g0 * kpm + ppb, n_pages)))
        xg = [jnp.concatenate([ckv_buf[half, g * kpm + j] for j in range(kpm)], axis=0).astype(BF16) for g in gs]
        rg = [_dot_nt(lq, x) for x in xg]
        pg = [_dot(qp, jnp.concatenate([kpe_buf[half, g * kpm + j] for j in range(kpm)], axis=1).astype(BF16))
              for g in gs]
        xs.append(jnp.concatenate(xg, axis=0))
        ss.append(jnp.concatenate([scores(r, s_rope) for r, s_rope in zip(rg, pg)], axis=1))

    xn = cn_ref[...].astype(BF16)
    sn = scores(_dot_nt(lq, xn), _dot_nt(qp, kn_ref[...].astype(BF16)))
    tq = lax.broadcasted_iota(jnp.int32, (nrow, t_new), 0) % t_new
    ts = lax.broadcasted_iota(jnp.int32, (nrow, t_new), 1)
    xs.append(xn)
    ss.append(jnp.where(ts <= tq, sn, -jnp.inf))

    ms = [jnp.max(s, axis=-1, keepdims=True) for s in ss]
    ps = [jnp.exp2(s - mi) for s, mi in zip(ss, ms)]
    ls = [jnp.sum(p, axis=-1, keepdims=True) for p in ps]
    pvs = [_dot(p.astype(BF16), x) for p, x in zip(ps, xs)]
    m = ms[0]
    for mi in ms[1:]:
        m = jnp.maximum(m, mi)
    ws = [jnp.exp2(mi - m) for mi in ms]
    l = ws[0] * ls[0]
    oa = ws[0] * pvs[0]
    for wi, li, pv in zip(ws[1:], ls[1:], pvs[1:]):
        l = l + wi * li
        oa = oa + wi * pv
    oa = oa / l

    r = _dot(oa.astype(BF16), wv_ref[...])
    lane = lax.broadcasted_iota(jnp.int32, (1, MLA_H * MLA_DV), 1)
    out = jnp.zeros((t_new, MLA_H * MLA_DV), F32)
    for h in range(MLA_H):
        out = out + jnp.where(lane // MLA_DV == h, r[t_new * h:t_new * (h + 1)], 0.0)
    o_ref[...] = out.astype(BF16)

    @pl.when(b == nb - 1)
    def _():
        wait_pages(1 - half)


def _mla_sample(mq, ckv_new, kp_new, pool_ckv, pool_kpe_t, page_table, w, batch, t_new, kpm):
    n_pages = page_table.shape[1]
    pt = page_table.reshape(-1)
    tok = lambda width: pl.BlockSpec((None, t_new, width), lambda b, pt_ref: (b, 0, 0))
    g3 = lambda a: a.reshape(batch, t_new, a.shape[-1])
    const = lambda a: pl.BlockSpec(a.shape, lambda b, pt_ref: (0,) * a.ndim)
    hbm = pl.BlockSpec(memory_space=pl.ANY)
    grid_spec = pltpu.PrefetchScalarGridSpec(
        num_scalar_prefetch=1,
        grid=(batch,),
        in_specs=[tok(MLA_H * LANES), const(w['w_kq']), const(w['w_kt']), const(w['w_v']),
                  tok(MLA_KVR), tok(MLA_ROPE), hbm, hbm],
        out_specs=tok(MLA_H * MLA_DV),
        scratch_shapes=[pltpu.VMEM((2, n_pages, PAGE, MLA_KVR), F32),
                        pltpu.VMEM((2, n_pages, MLA_ROPE, PAGE), F32),
                        pltpu.SemaphoreType.DMA((2, 2))],
    )
    om = pl.pallas_call(
        functools.partial(_mla_sample_kernel, n_pages=n_pages, t_new=t_new, kpm=kpm),
        grid_spec=grid_spec,
        out_shape=jax.ShapeDtypeStruct((batch, t_new, MLA_H * MLA_DV), BF16),
        compiler_params=_cparams(("arbitrary",)),
        name="mla_sample",
    )(pt, g3(mq), w['w_kq'], w['w_kt'], w['w_v'], g3(ckv_new), g3(kp_new), pool_ckv, pool_kpe_t)
    return om.reshape(batch * t_new, MLA_H * MLA_DV)


def _mix_out_kernel(x_ref, og_ref, om_ref, wo_ref, gx_ref, wxq_ref, gxq_ref, x1_o, xq_o):
    x1 = x_ref[...] + _dot(og_ref[...], wo_ref[0:512, :]) + _dot(om_ref[...], wo_ref[512:1024, :])
    x1_o[...] = x1
    qf = _dot(_rms(x1, gx_ref[...]).astype(BF16), wxq_ref[...])
    g = gxq_ref[...]
    for h in range(X_H):
        hs = slice(X_D * h, X_D * (h + 1))
        xq_o[:, hs] = _rms(qf[:, hs], g).astype(BF16)


def _mix_out(x, og, om, w, tm):
    n = x.shape[0]
    row = lambda width: pl.BlockSpec((tm, width), lambda i: (i, 0))
    consts = [w['w_out'], w['g_x'], w['w_xq'], w['g_xq']]
    return pl.pallas_call(
        _mix_out_kernel,
        grid=(n // tm,),
        in_specs=[row(D_MODEL), row(512), row(512)] + [_full(c.shape) for c in consts],
        out_specs=[row(D_MODEL), row(X_H * X_D)],
        out_shape=[jax.ShapeDtypeStruct((n, D_MODEL), F32), jax.ShapeDtypeStruct((n, X_H * X_D), BF16)],
        compiler_params=_cparams(("parallel",)),
        name="mix_out",
    )(x, og, om, *consts)


def _cross_kernel(xq_ref, mk_ref, mv_ref, o_ref, *, n_mem):
    gb = xq_ref.shape[0]

    def head(ref, g, h):
        if len(ref.shape) == 4:
            return ref[g, h]
        return ref[pl.ds(g * n_mem * X_H + h, n_mem, stride=X_H), :].astype(BF16)

    units = [(g, h) for g in range(gb) for h in range(X_H)]
    ss = [_dot_nt(xq_ref[g, :, X_D * h:X_D * (h + 1)], head(mk_ref, g, h)) for g, h in units]
    ps = [jnp.exp2(s - jnp.max(s, axis=-1, keepdims=True)) for s in ss]
    for (g, h), p in zip(units, ps):
        o = _dot(p.astype(BF16), head(mv_ref, g, h))
        o_ref[g, :, X_D * h:X_D * (h + 1)] = (o * (1.0 / jnp.sum(p, axis=-1, keepdims=True))).astype(BF16)


def _cross(xq, mk, mv, batch, seq, n_mem, tq, gb):
    nq = seq // tq
    assert gb == 1 or nq == 1
    row = pl.BlockSpec((gb, tq, X_H * X_D), lambda b, i: (b * nq + i, 0, 0))
    if mk.ndim == 4:
        mem = pl.BlockSpec((gb,) + mk.shape[1:], lambda b, i: (b, 0, 0, 0))
    else:
        mem = pl.BlockSpec((gb * n_mem * X_H, X_D), lambda b, i: (b, 0))
    o = pl.pallas_call(
        functools.partial(_cross_kernel, n_mem=n_mem),
        grid=(batch // gb, nq),
        in_specs=[row, mem, mem],
        out_specs=row,
        out_shape=jax.ShapeDtypeStruct((batch * nq, tq, X_H * X_D), BF16),
        compiler_params=_cparams(("parallel", "arbitrary")),
        name="cross",
    )(xq.reshape(batch * nq, tq, X_H * X_D), mk, mv)
    return o.reshape(batch * seq, X_H * X_D)


def _ffn_kernel(x_ref, o_ref, wxo_ref, g_ref, w1_ref, w2_ref, y_o, *, fc):
    x = x_ref[...] + _dot(o_ref[...], wxo_ref[...])
    h = _rms(x, g_ref[...]).astype(BF16)
    acc = x
    for c in range(D_FF // fc):
        u = jnp.maximum(_dot(h, w1_ref[:, fc * c:fc * (c + 1)]), 0.0)
        acc = acc + _dot((u * u).astype(BF16), w2_ref[fc * c:fc * (c + 1), :])
    y_o[...] = acc


def _ffn(x, o, w, tm, fc):
    n = x.shape[0]
    row = lambda width: pl.BlockSpec((tm, width), lambda i: (i, 0))
    consts = [w['w_xo'], w['g_ff'], w['w_ff1'], w['w_ff2']]
    return pl.pallas_call(
        functools.partial(_ffn_kernel, fc=fc),
        grid=(n // tm,),
        in_specs=[row(D_MODEL), row(X_H * X_D)] + [_full(c.shape) for c in consts],
        out_specs=row(D_MODEL),
        out_shape=jax.ShapeDtypeStruct((n, D_MODEL), F32),
        compiler_params=_cparams(("parallel",)),
        name="ffn",
    )(x, o, *consts)


def _mem_kv_kernel(m_ref, gm_ref, wk_ref, wv_ref, gk_ref, k_o, v_o, kh_o, vh_o, *, n_mem):
    tm = m_ref.shape[0]
    m = _rms(m_ref[...], gm_ref[...]).astype(BF16)
    kf = _dot(m, wk_ref[...])
    vf = _dot(m, wv_ref[...])
    g = gk_ref[...]
    for h in range(X_H):
        hs = slice(X_D * h, X_D * (h + 1))
        kn = _rms(kf[:, hs], g)
        k_o[pl.ds(h, tm, stride=X_H), :] = kn
        v_o[pl.ds(h, tm, stride=X_H), :] = vf[:, hs]
        for b in range(kh_o.shape[0]):
            rs = slice(n_mem * b, n_mem * (b + 1))
            kh_o[b, h] = kn[rs].astype(BF16)
            vh_o[b, h] = vf[rs, hs].astype(BF16)


def _mem_kv(mem, w, n_mem, tm):
    n = mem.shape[0]
    bt = tm // n_mem
    row = lambda width: pl.BlockSpec((tm, width), lambda i: (i, 0))
    head = pl.BlockSpec((bt, X_H, n_mem, X_D), lambda i: (i, 0, 0, 0))
    consts = [w['g_mem'], w['w_xk'], w['w_xv'], w['g_xk']]
    return pl.pallas_call(
        functools.partial(_mem_kv_kernel, n_mem=n_mem),
        grid=(n // tm,),
        in_specs=[row(D_MODEL)] + [_full(c.shape) for c in consts],
        out_specs=[pl.BlockSpec((tm * X_H, X_D), lambda i: (i, 0))] * 2 + [head, head],
        out_shape=[jax.ShapeDtypeStruct((n * X_H, X_D), F32)] * 2
                  + [jax.ShapeDtypeStruct((n // n_mem, X_H, n_mem, X_D), BF16)] * 2,
        compiler_params=_cparams(("parallel",)),
        name="mem_kv",
    )(mem, *consts)


def _prep_weights(g_mix, w_in, w_gla_a2, b_gla_a, g_gla_o, g_mla_qa, w_mla_qb, g_mla_kva, w_mla_kvb,
                  g_q_nope, g_k_nope, g_q_rope, g_k_rope, w_out, g_x, g_mem, w_xq, w_xk, w_xv, g_xq, g_xk,
                  w_xo, g_ff, w_ff1, w_ff2):
    rowv = lambda g: g.reshape(1, -1).astype(F32)
    zc = lambda n: jnp.zeros((D_MODEL, n), F32)
    sizes = (256, 256, 512, 512, GLA_RANK, MLA_QR, MLA_KVR, MLA_ROPE)
    offs = [0]
    for s in sizes:
        offs.append(offs[-1] + s)
    q, k, v, r, a, qa, kva, kpe = [w_in[:, offs[i]:offs[i + 1]] for i in range(8)]
    w_in_p = jnp.concatenate(
        [q * (GLA_K ** -0.5), k, v, r, qa,
         a, zc(_KPE_LANE - GLA_RANK), kpe, zc(LANES - _KPE_LANE - MLA_ROPE), kva], axis=1)
    w_a2 = jnp.concatenate([w_gla_a2, jnp.zeros((LANES - GLA_RANK, GLA_H * GLA_K), F32)], axis=0)

    qb = w_mla_qb.reshape(MLA_QR, MLA_H, MLA_NOPE + MLA_ROPE)
    qb = jnp.pad(qb, ((0, 0), (0, 0), (0, LANES - MLA_NOPE - MLA_ROPE))).reshape(MLA_QR, MLA_H * LANES)
    kvb = w_mla_kvb.reshape(MLA_KVR, MLA_H, MLA_NOPE + MLA_DV)
    wk = kvb[:, :, :MLA_NOPE]
    wk_p = jnp.pad(wk, ((0, 0), (0, 0), (0, LANES - MLA_NOPE))).reshape(MLA_KVR, MLA_H * LANES)
    wv = kvb[:, :, MLA_NOPE:].reshape(MLA_KVR, MLA_H * MLA_DV)
    z32 = jnp.zeros((LANES - MLA_NOPE - MLA_ROPE,), F32)
    gq_row = jnp.tile(jnp.concatenate([g_q_nope, g_q_rope, z32]), MLA_H) * (MLA_SCALE * LOG2E)
    gk_row = jnp.tile(jnp.concatenate([g_k_nope, jnp.zeros((LANES - MLA_NOPE,), F32)]), MLA_H)
    g_kpe = jnp.concatenate([jnp.zeros((_KPE_LANE,), F32), g_k_rope, z32])

    li = jnp.arange(2 * LANES)
    seg = jnp.where(li % LANES < MLA_NOPE, 0, jnp.where(li % LANES < MLA_NOPE + MLA_ROPE, 1, 2))
    same = (li[:, None] // LANES == li[None, :] // LANES) & (seg[:, None] == seg[None, :]) & (seg[:, None] < 2)
    bd = jnp.where(same, jnp.where(seg[:, None] == 0, 1.0 / MLA_NOPE, 1.0 / MLA_ROPE), 0.0)

    wkq = jnp.transpose(wk, (1, 2, 0)) * g_k_nope[None, :, None]
    wkq = jnp.pad(wkq, ((0, 0), (0, LANES - MLA_NOPE), (0, 0)))
    wkt = jnp.transpose(wk, (1, 2, 0)).reshape(MLA_H * MLA_NOPE, MLA_KVR)

    hk = jnp.arange(GLA_H * GLA_K) // GLA_K
    hv = jnp.arange(GLA_H * GLA_V) // GLA_V
    bexp = (hk[:, None] == hv[None, :])
    return {
        'g_mix': rowv(g_mix), 'w_in': w_in_p.astype(BF16), 'w_a2': w_a2.astype(BF16), 'b_a': rowv(b_gla_a),
        'g_qa': rowv(g_mla_qa), 'w_qb': qb.astype(BF16), 'gq_row': rowv(gq_row), 'g_kva': rowv(g_mla_kva),
        'w_kv': jnp.concatenate([wk_p, wv], axis=1).astype(BF16), 'gk_row': rowv(gk_row),
        'w_v': wv.astype(BF16), 'g_kpe': rowv(g_kpe),
        'bd': bd.astype(BF16), 'w_kq': wkq.astype(BF16), 'w_kt': wkt.astype(BF16),
        'g_gla_o': rowv(g_gla_o), 'bexp': bexp.astype(BF16),
        'w_out': w_out.astype(BF16), 'g_x': rowv(g_x), 'w_xq': w_xq.astype(BF16),
        'g_xq': rowv(g_xq) * (X_SCALE * LOG2E),
        'g_mem': rowv(g_mem), 'w_xk': w_xk.astype(BF16), 'w_xv': w_xv.astype(BF16), 'g_xk': rowv(g_xk),
        'w_xo': w_xo.astype(BF16), 'g_ff': rowv(g_ff), 'w_ff1': w_ff1.astype(BF16), 'w_ff2': w_ff2.astype(BF16),
    }


def _rope_tables(pos):
    half = MLA_ROPE // 2
    inv = ROPE_THETA ** (-jnp.arange(half, dtype=F32) / half)
    ang = pos.astype(F32)[:, None] * inv[None, :]
    cos, sin = jnp.cos(ang), jnp.sin(ang)
    n = pos.shape[0]
    one = jnp.ones((n, _KPE_LANE), F32)
    z = lambda w_: jnp.zeros((n, w_), F32)
    tail = LANES - _KPE_LANE - MLA_ROPE
    c = jnp.concatenate([one, cos, cos, jnp.ones((n, tail), F32)], axis=1)
    s_up = jnp.concatenate([z(_KPE_LANE), -sin, z(half), z(tail)], axis=1)
    s_dn = jnp.concatenate([z(_KPE_LANE), z(half), sin, z(tail)], axis=1)
    return c, s_up, s_dn


def _tile_rows(n, cap):
    t = min(n, cap)
    while n % t:
        t //= 2
    return t


def _tail(x1, xq, mk, mv, w, batch, seq, n_mem):
    tq = _tile_rows(seq, 512)
    gb = _tile_rows(batch, max(1, CROSS_ROWS // seq)) if tq == seq else 1
    o = _cross(xq, mk, mv, batch, seq, n_mem, tq, gb)
    return _ffn(x1, o, w, _tile_rows(x1.shape[0], 512), 1024)


def _prompt_layer(x, mem, w):
    batch, seq, _ = x.shape
    n = batch * seq
    xf = x.reshape(n, D_MODEL)
    tm = _tile_rows(seq, 512)
    tabs = _rope_tables(jnp.arange(seq))
    gq, gk, gv, gr, la, mq, ckv, kpt, mk_, mvt = _in_proj(xf, w, tabs, tm, True)
    kp = jnp.swapaxes(kpt, 1, 2)
    c_len = GLA_CHUNK
    cps = _tile_rows(seq // c_len, GLA_CHUNKS_PER_STEP)
    s0 = jnp.zeros((batch, GLA_H * GLA_K, GLA_V), F32)
    og, s_fin = _gla(gq, gk, gv, gr, la, s0, w, batch, seq, c_len, min(GLA_SUB, c_len), cps,
                     _tile_rows(batch, GLA_SEQS_PER_STEP))
    om = _mla_prompt(mq, mk_, mvt, batch, seq, ATT_TILE)
    x1, xq = _mix_out(xf, og, om, w, tm)
    n_mem = mem.shape[1]
    memf = mem.reshape(-1, D_MODEL)
    xk, xv, xkh, xvh = _mem_kv(memf, w, n_mem, max(n_mem, _tile_rows(memf.shape[0], 512)))
    y = _tail(x1, xq, xkh, xvh, w, batch, seq, n_mem)
    return y, ckv, kp, xk, xv, s_fin


def _sample_layer(x, pool_ckv, pool_kpe, page_table, mem_k, mem_v, s_prev, w):
    batch, seq, _ = x.shape
    n = batch * seq
    xf = x.reshape(n, D_MODEL)
    tm = _tile_rows(n, 512)
    past = page_table.shape[1] * PAGE
    pos = past + (jnp.arange(tm) % seq)
    gq, gk, gv, gr, la, mq, ckv, kp = _in_proj(xf, w, _rope_tables(pos), tm, False)
    s0 = s_prev.reshape(batch, GLA_H * GLA_K, GLA_V)
    og, s_fin = _gla(gq, gk, gv, gr, la, s0, w, batch, seq, seq, seq, 1, _tile_rows(batch, GLA_SHORT_SEQS_PER_STEP))
    om = _mla_sample(mq, ckv, kp, pool_ckv, jnp.swapaxes(pool_kpe, 1, 2), page_table, w, batch, seq,
                     _tile_rows(page_table.shape[1], SAMPLE_PAGES_PER_DOT))
    x1, xq = _mix_out(xf, og, om, w, tm)
    y = _tail(x1, xq, mem_k.reshape(-1, X_D), mem_v.reshape(-1, X_D), w, batch, seq, mem_k.shape[1])
    return y, ckv, kp, s_fin


def kernel(x_prompt, x_sample, mem_prompt, cache_ckv, cache_kpe, cache_mem_k, cache_mem_v, state_gla, page_table, g_mix, w_in, w_gla_a2, b_gla_a, g_gla_o, g_mla_qa, w_mla_qb, g_mla_kva, w_mla_kvb, g_q_nope, g_k_nope, g_q_rope, g_k_rope, w_out, g_x, g_mem, w_xq, w_xk, w_xv, g_xq, g_xk, w_xo, g_ff, w_ff1, w_ff2):
    depth = w_in.shape[0]
    assert depth == 1, "one layer: prompt-group caches of layer l would feed layer l+1 otherwise unchanged"
    params = (g_mix, w_in, w_gla_a2, b_gla_a, g_gla_o, g_mla_qa, w_mla_qb, g_mla_kva, w_mla_kvb,
              g_q_nope, g_k_nope, g_q_rope, g_k_rope, w_out, g_x, g_mem, w_xq, w_xk, w_xv, g_xq, g_xk,
              w_xo, g_ff, w_ff1, w_ff2)
    w = _prep_weights(*[p[0] for p in params])
    bp, tp, _ = x_prompt.shape
    bs, tsq, _ = x_sample.shape
    yp, ckv_p, kp_p, xk, xv, gla_p = _prompt_layer(x_prompt, mem_prompt, w)
    ys, ckv_s, kp_s, gla_s = _sample_layer(x_sample, cache_ckv.reshape(cache_ckv.shape[1:]), cache_kpe.reshape(cache_kpe.shape[1:]), page_table,
                                            cache_mem_k[0], cache_mem_v[0], state_gla[0], w)
    n_mem = mem_prompt.shape[1]
    return (yp.reshape(bp, tp, D_MODEL), ys.reshape(bs, tsq, D_MODEL),
            ckv_p.reshape(1, bp, tp, MLA_KVR), kp_p.reshape(1, bp, tp, MLA_ROPE),
            xk.reshape(1, bp, n_mem, X_H, X_D), xv.reshape(1, bp, n_mem, X_H, X_D),
            gla_p.reshape(1, bp, GLA_H, GLA_K, GLA_V),
            ckv_s.reshape(1, bs, tsq, MLA_KVR), kp_s.reshape(1, bs, tsq, MLA_ROPE),
            gla_s.reshape(1, bs, GLA_H, GLA_K, GLA_V))
```
